```python
import jax
import jax.numpy as jnp
from jax import lax
import numpy as np

D_MODEL = 1024
BATCH = 4
SEQ = 4096
DEPTH = 1

MEM_LEN = 256
SSM_GROUP = 16
SSM_WIDTH = 768
SSM_GROUPS = SSM_WIDTH // SSM_GROUP
SSM_STATE = 64
SSM_DT_MIN = 0.001
SSM_DT_MAX = 0.1
ATT_HEAD_DIM = 64
ATT_HEADS_PER_GROUP = 4
DILATION_PATTERN = ((128, 1), (512, 4), (2048, 16))
ATT_GROUPS = len(DILATION_PATTERN)
ATT_HEADS = ATT_GROUPS * ATT_HEADS_PER_GROUP
ATT_WIDTH = ATT_HEADS * ATT_HEAD_DIM
ATT_MERGED = ATT_HEADS_PER_GROUP * ATT_HEAD_DIM
ATT_SCALE = ATT_HEAD_DIM ** -0.5
ROT_DIM = ATT_HEAD_DIM // 4
ROPE_THETA = 500000.0
XATT_HEADS = 4
XATT_HEAD_DIM = D_MODEL // XATT_HEADS
XATT_SCALE = XATT_HEAD_DIM ** -0.5
D_FF = 4 * D_MODEL
DEEPNORM_ALPHA = (2 * DEPTH) ** 0.25
DEEPNORM_BETA = (8 * DEPTH) ** -0.25
LN_EPS = 1e-5
NEG_INF = -1e30
OFF_U = 0
OFF_Q = OFF_U + SSM_WIDTH
OFF_K = OFF_Q + ATT_WIDTH
OFF_V = OFF_K + ATT_WIDTH
OFF_GS = OFF_V + ATT_WIDTH
OFF_GA = OFF_GS + D_MODEL
IN_COLS = OFF_GA + D_MODEL

kernel_name = 'hybrid_s5_dilated_attn_block'


def layer_norm(x, g, b):
    xf = x.astype(jnp.float32)
    mu = jnp.mean(xf, axis=-1, keepdims=True)
    var = jnp.mean(jnp.square(xf - mu), axis=-1, keepdims=True)
    y = (xf - mu) * lax.rsqrt(var + LN_EPS) * g.astype(jnp.float32) + b.astype(jnp.float32)
    return y.astype(x.dtype)


def rope_partial(t, cos, sin):
    half = ROT_DIM // 2
    rot = t[..., :ROT_DIM].astype(jnp.float32)
    x1, x2 = rot[..., :half], rot[..., half:]
    c = cos[:, :, None, :]
    s = sin[:, :, None, :]
    rot = jnp.concatenate([x1 * c - x2 * s, x2 * c + x1 * s], axis=-1).astype(t.dtype)
    return jnp.concatenate([rot, t[..., ROT_DIM:]], axis=-1)


def s5_ssm(u, log_dt, a_re, a_im, b_re, b_im, c_re, c_im, d):
    f32 = jnp.float32
    bsz, s, _ = u.shape
    uf = u.astype(f32)
    ug = uf.reshape(bsz, s, SSM_GROUPS, SSM_GROUP)
    a_re = a_re.astype(f32)
    a_im = a_im.astype(f32)
    dt = jnp.exp(log_dt.astype(f32))[:, None]
    mag = jnp.exp(a_re * dt)
    ab_re = mag * jnp.cos(a_im * dt)
    ab_im = mag * jnp.sin(a_im * dt)
    den = jnp.square(a_re) + jnp.square(a_im)
    nr = ab_re - 1.0
    f_re = (nr * a_re + ab_im * a_im) / den
    f_im = (ab_im * a_re - nr * a_im) / den
    b_re = b_re.astype(f32)
    b_im = b_im.astype(f32)
    bb_re = f_re[..., None] * b_re - f_im[..., None] * b_im
    bb_im = f_re[..., None] * b_im + f_im[..., None] * b_re
    w_re = jnp.einsum('bsgc,gnc->bsgn', ug, bb_re)
    w_im = jnp.einsum('bsgc,gnc->bsgn', ug, bb_im)
    ar = jnp.broadcast_to(ab_re, w_re.shape)
    ai = jnp.broadcast_to(ab_im, w_im.shape)

    def combine(e1, e2):
        a1r, a1i, b1r, b1i = e1
        a2r, a2i, b2r, b2i = e2
        return (a2r * a1r - a2i * a1i,
                a2r * a1i + a2i * a1r,
                a2r * b1r - a2i * b1i + b2r,
                a2r * b1i + a2i * b1r + b2i)

    _, _, h_re, h_im = lax.associative_scan(combine, (ar, ai, w_re, w_im), axis=1)
    y = (jnp.einsum('bsgn,gcn->bsgc', h_re, c_re.astype(f32))
         - jnp.einsum('bsgn,gcn->bsgc', h_im, c_im.astype(f32)))
    y = y.reshape(bsz, s, SSM_WIDTH) + d.astype(f32) * uf
    return y.astype(u.dtype)


def dilated_window_attention(q, k, v, window, dilation):
    bsz, s, h, dh = q.shape
    span = window // dilation
    blk = span
    unit = blk * dilation
    length = -(-s // unit) * unit
    n_blk = length // unit
    pad = length - s

    def arrange(t):
        t = jnp.pad(t, ((0, 0), (0, pad), (0, 0), (0, 0)))
        t = t.reshape(bsz, length // dilation, dilation, h, dh)
        t = t.transpose(0, 2, 1, 3, 4)
        return t.reshape(bsz, dilation, n_blk, blk, h, dh)

    def with_prev(t):
        prev = jnp.pad(t, ((0, 0), (0, 0), (1, 0), (0, 0), (0, 0), (0, 0)))[:, :, :-1]
        return jnp.concatenate([prev, t], axis=3)

    qb = arrange(q)
    kw = with_prev(arrange(k))
    vw = with_prev(arrange(v))
    scores = jnp.einsum('brnqhd,brnkhd->brnhqk', qb, kw).astype(jnp.float32) * ATT_SCALE
    qi = jnp.arange(blk)[:, None]
    ki = jnp.arange(2 * blk)[None, :]
    steps = qi + blk - ki
    band = (steps >= 0) & (steps <= span)
    has_prev = (jnp.arange(n_blk) > 0)[:, None, None]
    valid = band[None] & (has_prev | (ki >= blk)[None])
    scores = jnp.where(valid[None, None, :, None], scores, NEG_INF)
    m = jnp.max(scores, axis=-1, keepdims=True)
    p = jnp.exp(scores - m)
    den = jnp.sum(p, axis=-1, keepdims=True)
    lse = (m + jnp.log(den))[..., 0]
    out = jnp.einsum('brnhqk,brnkhd->brnhqd', p, vw.astype(jnp.float32)) / den
    out = out.transpose(0, 1, 2, 4, 3, 5).reshape(bsz, dilation, length // dilation, h, dh)
    out = out.transpose(0, 2, 1, 3, 4).reshape(bsz, length, h, dh)[:, :s]
    lse = lse.transpose(0, 1, 2, 4, 3).reshape(bsz, dilation, length // dilation, h)
    lse = lse.transpose(0, 2, 1, 3).reshape(bsz, length, h)[:, :s]
    return out, lse


def memory_cross_attention(h, mem, w_xq, w_xkv, w_xo):
    bsz, s, _ = h.shape
    q = (h @ w_xq).reshape(bsz, s, XATT_HEADS, XATT_HEAD_DIM)
    kv = mem @ w_xkv
    k = kv[..., :D_MODEL].reshape(bsz, -1, XATT_HEADS, XATT_HEAD_DIM)
    v = kv[..., D_MODEL:].reshape(bsz, -1, XATT_HEADS, XATT_HEAD_DIM)
    scores = jnp.einsum('bshd,bmhd->bhsm', q, k).astype(jnp.float32) * XATT_SCALE
    p = jax.nn.softmax(scores, axis=-1)
    o = jnp.einsum('bhsm,bmhd->bshd', p, v.astype(jnp.float32)).astype(h.dtype)
    return o.reshape(bsz, s, D_MODEL) @ w_xo


def setup_inputs(seed: int = 0) -> dict:
    key = jax.random.key(seed)
    ks = jax.random.split(key, 40)
    f32 = jnp.float32
    L, D, G, N, C = DEPTH, D_MODEL, SSM_GROUPS, SSM_STATE, SSM_GROUP

    def nrm(k, shape, scale):
        return jax.random.normal(k, shape, f32) * scale

    def gain(k, shape):
        return 1.0 + nrm(k, shape, 0.05)

    n_idx = jnp.arange(N, dtype=f32)
    inp = {
        'x': nrm(ks[0], (BATCH, SEQ, D), 1.0),
        'mem': nrm(ks[1], (BATCH, MEM_LEN, D), 1.0),
        'positions': jnp.broadcast_to(jnp.arange(SEQ, dtype=jnp.int32)[None, :], (BATCH, SEQ)),
        'ln_in_g': gain(ks[2], (D,)),
        'ln_in_b': nrm(ks[3], (D,), 0.02),
        'w_in': nrm(ks[4], (L, D, IN_COLS), D ** -0.5),
        'b_in': nrm(ks[5], (L, IN_COLS), 0.02),
        'ssm_log_dt': jax.random.uniform(ks[6], (L, G), f32, np.log(SSM_DT_MIN), np.log(SSM_DT_MAX)),
        'ssm_a_re': -0.5 + nrm(ks[7], (L, G, N), 0.01),
        'ssm_a_im': jnp.pi * n_idx + nrm(ks[8], (L, G, N), 0.01),
        'ssm_b_re': nrm(ks[9], (L, G, N, C), (0.5 / C) ** 0.5),
        'ssm_b_im': nrm(ks[10], (L, G, N, C), (0.5 / C) ** 0.5),
        'ssm_c_re': nrm(ks[11], (L, G, C, N), (0.5 / N) ** 0.5),
        'ssm_c_im': nrm(ks[12], (L, G, C, N), (0.5 / N) ** 0.5),
        'ssm_d': nrm(ks[13], (L, SSM_WIDTH), 1.0),
        'w_glu': nrm(ks[14], (L, SSM_WIDTH, 2 * D), SSM_WIDTH ** -0.5),
        'b_glu': nrm(ks[15], (L, 2 * D), 0.02),
        'w_att_up': nrm(ks[16], (L, ATT_MERGED, D), ATT_MERGED ** -0.5),
        'w_mix_out': nrm(ks[17], (L, D, D), DEEPNORM_BETA * D ** -0.5),
        'b_mix_out': nrm(ks[18], (L, D), 0.02),
        'ln1_g': gain(ks[19], (L, D)),
        'ln1_b': nrm(ks[20], (L, D), 0.02),
        'w_xq': nrm(ks[21], (L, D, D), D ** -0.5),
        'w_xkv': nrm(ks[22], (L, D, 2 * D), D ** -0.5),
        'w_xo': nrm(ks[23], (L, D, D), DEEPNORM_BETA * D ** -0.5),
        'ln2_g': gain(ks[24], (L, D)),
        'ln2_b': nrm(ks[25], (L, D), 0.02),
        'w_ff1': nrm(ks[26], (L, D, D_FF), D ** -0.5),
        'b_ff1': nrm(ks[27], (L, D_FF), 0.02),
        'w_ff2': nrm(ks[28], (L, D_FF, D), DEEPNORM_BETA * D_FF ** -0.5),
        'b_ff2': nrm(ks[29], (L, D), 0.02),
        'ln3_g': gain(ks[30], (L, D)),
        'ln3_b': nrm(ks[31], (L, D), 0.02),
    }
    return inp


def reference(x, mem, positions, ln_in_g, ln_in_b, w_in, b_in, ssm_log_dt, ssm_a_re, ssm_a_im,
              ssm_b_re, ssm_b_im, ssm_c_re, ssm_c_im, ssm_d, w_glu, b_glu, w_att_up, w_mix_out,
              b_mix_out, ln1_g, ln1_b, w_xq, w_xkv, w_xo, ln2_g, ln2_b, w_ff1, b_ff1, w_ff2, b_ff2,
              ln3_g, ln3_b):
    bsz, s, _ = x.shape
    inv_freq = ROPE_THETA ** (-jnp.arange(0, ROT_DIM, 2, dtype=jnp.float32) / ROT_DIM)
    ang = positions.astype(jnp.float32)[..., None] * inv_freq
    cos, sin = jnp.cos(ang), jnp.sin(ang)

    h = layer_norm(x, ln_in_g, ln_in_b)
    for l in range(DEPTH):
        proj = h @ w_in[l] + b_in[l]
        u = proj[..., OFF_U:OFF_U + SSM_WIDTH]
        q = proj[..., OFF_Q:OFF_Q + ATT_WIDTH].reshape(bsz, s, ATT_HEADS, ATT_HEAD_DIM)
        k = proj[..., OFF_K:OFF_K + ATT_WIDTH].reshape(bsz, s, ATT_HEADS, ATT_HEAD_DIM)
        v = proj[..., OFF_V:OFF_V + ATT_WIDTH].reshape(bsz, s, ATT_HEADS, ATT_HEAD_DIM)
        g_ssm = proj[..., OFF_GS:OFF_GS + D_MODEL]
        g_att = proj[..., OFF_GA:OFF_GA + D_MODEL]

        y = s5_ssm(u, ssm_log_dt[l], ssm_a_re[l], ssm_a_im[l], ssm_b_re[l], ssm_b_im[l],
                   ssm_c_re[l], ssm_c_im[l], ssm_d[l])
        z = jax.nn.gelu(y) @ w_glu[l] + b_glu[l]
        b_ssm = z[..., :D_MODEL] * jax.nn.sigmoid(z[..., D_MODEL:])

        q = rope_partial(q, cos, sin)
        k = rope_partial(k, cos, sin)
        outs, lses = [], []
        for gi, (win, dil) in enumerate(DILATION_PATTERN):
            sl = slice(gi * ATT_HEADS_PER_GROUP, (gi + 1) * ATT_HEADS_PER_GROUP)
            o_g, lse_g = dilated_window_attention(q[:, :, sl], k[:, :, sl], v[:, :, sl], win, dil)
            outs.append(o_g)
            lses.append(lse_g)
        wts = jax.nn.softmax(jnp.stack(lses, axis=0), axis=0)
        att = jnp.einsum('gbsh,gbshd->bshd', wts, jnp.stack(outs, axis=0)).astype(h.dtype)
        b_att = att.reshape(bsz, s, ATT_MERGED) @ w_att_up[l]

        mixed = jax.nn.sigmoid(g_ssm) * b_ssm + jax.nn.sigmoid(g_att) * b_att
        h = layer_norm(DEEPNORM_ALPHA * h + (mixed @ w_mix_out[l] + b_mix_out[l]), ln1_g[l], ln1_b[l])

        xo = memory_cross_attention(h, mem, w_xq[l], w_xkv[l], w_xo[l])
        h = layer_norm(DEEPNORM_ALPHA * h + xo, ln2_g[l], ln2_b[l])

        ff = jnp.square(jax.nn.relu(h @ w_ff1[l] + b_ff1[l])) @ w_ff2[l] + b_ff2[l]
        h = layer_norm(DEEPNORM_ALPHA * h + ff, ln3_g[l], ln3_b[l])
    return h
```

```python
import functools

import jax
import jax.numpy as jnp
import numpy as np
from jax import lax
from jax.experimental import pallas as pl
from jax.experimental.pallas import tpu as pltpu

F32 = jnp.float32
BF16 = jnp.bfloat16

D_MODEL = 1024
SSM_GROUP = 16
SSM_WIDTH = 768
SSM_GROUPS = SSM_WIDTH // SSM_GROUP
SSM_STATE = 64
ATT_HEAD_DIM = 64
ATT_HEADS_PER_GROUP = 4
DILATIONS = (1, 4, 16)
ATT_SPAN = 128
ATT_WIDTH = 768
ATT_MERGED = ATT_HEADS_PER_GROUP * ATT_HEAD_DIM
ATT_SCALE = ATT_HEAD_DIM ** -0.5
ROT_DIM = ATT_HEAD_DIM // 4
ROPE_THETA = 500000.0
XATT_HEADS = 4
XATT_HEAD_DIM = D_MODEL // XATT_HEADS
XATT_SCALE = XATT_HEAD_DIM ** -0.5
D_FF = 4 * D_MODEL
DEPTH = 1
DEEPNORM_ALPHA = (2 * DEPTH) ** 0.25
LN_EPS = 1e-5
NEG_INF = -1e30
IN_COLS = SSM_WIDTH + 3 * ATT_WIDTH + 2 * D_MODEL

CHUNK = 16
SG_GROUPS = 8
N_SG = SSM_GROUPS // SG_GROUPS
SG_LANES = SG_GROUPS * SSM_GROUP
FLAT = CHUNK * SG_LANES
SG_STATE = SG_GROUPS * SSM_STATE
MXU_TILE = 256
N_TT = FLAT // MXU_TILE

ROW_TILE = 256
VMEM_LIMIT = 56 * 1024 * 1024


def _cparams(n_axes):
    return pltpu.CompilerParams(dimension_semantics=("parallel",) * n_axes,
                                vmem_limit_bytes=VMEM_LIMIT)


def _layer_norm(x, g, b):
    mu = jnp.mean(x, axis=-1, keepdims=True)
    xc = x - mu
    var = jnp.mean(xc * xc, axis=-1, keepdims=True)
    return xc * lax.rsqrt(var + LN_EPS) * g + b


def _bdot(a, b):
    return jnp.dot(a.astype(BF16), b.astype(BF16), preferred_element_type=F32)


def _ln_proj_kernel(x_ref, g_ref, b_ref, w_ref, bi_ref, c_ref, s1_ref, s2_ref,
                    h_ref, u_ref, q_ref, k_ref, v_ref, gs_ref, ga_ref):
    h = _layer_norm(x_ref[...], g_ref[...], b_ref[...])
    h_ref[...] = h
    proj = jnp.dot(h.astype(BF16), w_ref[...], preferred_element_type=F32) + bi_ref[...]
    u_ref[...] = proj[:, 0:SSM_WIDTH].astype(BF16)
    cc, s1, s2 = c_ref[...], s1_ref[...], s2_ref[...]

    def rope(off, scale, out_ref):
        for i in range(ATT_WIDTH // 128):
            t = proj[:, off + i * 128: off + (i + 1) * 128]
            r = t * cc + pltpu.roll(t, 128 - ROT_DIM // 2, 1) * s1 + pltpu.roll(t, ROT_DIM // 2, 1) * s2
            out_ref[:, i * 128:(i + 1) * 128] = (r * scale).astype(BF16)

    rope(SSM_WIDTH, ATT_SCALE, q_ref)
    rope(SSM_WIDTH + ATT_WIDTH, 1.0, k_ref)
    off_v = SSM_WIDTH + 2 * ATT_WIDTH
    v_ref[...] = proj[:, off_v:off_v + ATT_WIDTH].astype(BF16)
    off_g = off_v + ATT_WIDTH
    gs_ref[...] = jax.nn.sigmoid(proj[:, off_g:off_g + D_MODEL]).astype(BF16)
    ga_ref[...] = jax.nn.sigmoid(proj[:, off_g + D_MODEL:off_g + 2 * D_MODEL]).astype(BF16)


def _ln_proj(x2, g, b, w, bi, cc, s1, s2):
    t = x2.shape[0]
    tm = ROW_TILE
    row = lambda n: pl.BlockSpec((tm, n), lambda i: (i, 0))
    full = lambda a: pl.BlockSpec(a.shape, lambda i: (0,) * a.ndim)
    outs = [jax.ShapeDtypeStruct((t, D_MODEL), F32)]
    outs += [jax.ShapeDtypeStruct((t, SSM_WIDTH), BF16)] * 4
    outs += [jax.ShapeDtypeStruct((t, D_MODEL), BF16)] * 2
    return pl.pallas_call(
        _ln_proj_kernel,
        grid=(t // tm,),
        in_specs=[row(D_MODEL), full(g), full(b), full(w), full(bi), row(128), row(128), row(128)],
        out_specs=[row(D_MODEL)] + [row(SSM_WIDTH)] * 4 + [row(D_MODEL)] * 2,
        out_shape=outs,
        compiler_params=_cparams(1),
        name="ln_proj",
    )(x2, g, b, w, bi, cc, s1, s2)


def _ssm_p_kernel(u_ref, wp_ref, p_ref):
    p_ref[0] = jnp.dot(u_ref[0], wp_ref[0], preferred_element_type=F32)


def _ssm_p(uflat, wp):
    nc = uflat.shape[1]
    tn = ROW_TILE
    return pl.pallas_call(
        _ssm_p_kernel,
        grid=(N_SG, nc // tn),
        in_specs=[pl.BlockSpec((1, tn, FLAT), lambda s, i: (s, i, 0)),
                  pl.BlockSpec((1, FLAT, 2 * SG_STATE), lambda s, i: (s, 0, 0))],
        out_specs=pl.BlockSpec((1, tn, 2 * SG_STATE), lambda s, i: (s, i, 0)),
        out_shape=jax.ShapeDtypeStruct((N_SG, nc, 2 * SG_STATE), F32),
        compiler_params=_cparams(2),
        name="ssm_p",
    )(uflat, wp)


def _ssm_scan_kernel(p_ref, ar_ref, ai_ref, h_ref, *, steps):
    ar = ar_ref[0]
    ai = ai_ref[0]

    def body(c, carry):
        hr, hi = carry
        h_ref[0, pl.ds(c, 1), 0:SG_STATE] = hr
        h_ref[0, pl.ds(c, 1), SG_STATE:2 * SG_STATE] = hi
        pr = p_ref[0, pl.ds(c, 1), 0:SG_STATE]
        pi = p_ref[0, pl.ds(c, 1), SG_STATE:2 * SG_STATE]
        return ar * hr - ai * hi + pr, ar * hi + ai * hr + pi

    z = jnp.zeros((1, SG_STATE), F32)
    lax.fori_loop(0, steps, body, (z, z))


def _ssm_scan(p, ar, ai, bsz):
    nc = p.shape[1]
    steps = nc // bsz
    return pl.pallas_call(
        functools.partial(_ssm_scan_kernel, steps=steps),
        grid=(N_SG, bsz),
        in_specs=[pl.BlockSpec((1, steps, 2 * SG_STATE), lambda s, b: (s, b, 0)),
                  pl.BlockSpec((1, 1, SG_STATE), lambda s, b: (s, 0, 0)),
                  pl.BlockSpec((1, 1, SG_STATE), lambda s, b: (s, 0, 0))],
        out_specs=pl.BlockSpec((1, steps, 2 * SG_STATE), lambda s, b: (s, b, 0)),
        out_shape=jax.ShapeDtypeStruct(p.shape, F32),
        compiler_params=_cparams(2),
        name="ssm_scan",
    )(p, ar, ai)


def _ssm_y_kernel(u_ref, t_ref, h_ref, wc_ref, y_ref):
    hb = h_ref[0].astype(BF16)
    for jo in range(N_TT):
        cols = slice(jo * MXU_TILE, (jo + 1) * MXU_TILE)
        acc = jnp.dot(hb, wc_ref[0, :, cols], preferred_element_type=F32)
        for ji in range(jo + 1):
            acc += jnp.dot(u_ref[0, :, ji * MXU_TILE:(ji + 1) * MXU_TILE], t_ref[0, jo - ji],
                           preferred_element_type=F32)
        y_ref[0, :, cols] = jax.nn.gelu(acc, approximate=True).astype(BF16)


def _ssm_y(uflat, tt, hprev, wc):
    nc = uflat.shape[1]
    tn = ROW_TILE
    return pl.pallas_call(
        _ssm_y_kernel,
        grid=(N_SG, nc // tn),
        in_specs=[pl.BlockSpec((1, tn, FLAT), lambda s, i: (s, i, 0)),
                  pl.BlockSpec((1, N_TT, MXU_TILE, MXU_TILE), lambda s, i: (s, 0, 0, 0)),
                  pl.BlockSpec((1, tn, 2 * SG_STATE), lambda s, i: (s, i, 0)),
                  pl.BlockSpec((1, 2 * SG_STATE, FLAT), lambda s, i: (s, 0, 0))],
        out_specs=pl.BlockSpec((1, tn, FLAT), lambda s, i: (s, i, 0)),
        out_shape=jax.ShapeDtypeStruct((N_SG, nc, FLAT), BF16),
        compiler_params=_cparams(2),
        name="ssm_y",
    )(uflat, tt, hprev, wc)


def _ssm_matrices(log_dt, a_re, a_im, b_re, b_im, c_re, c_im, d):
    g, n, c = SSM_GROUPS, SSM_STATE, SSM_GROUP
    dt = jnp.exp(log_dt.astype(F32))[:, None]
    a_re = a_re.astype(F32)
    a_im = a_im.astype(F32)
    ks = jnp.arange(CHUNK + 1, dtype=F32)
    mag = jnp.exp((a_re * dt)[..., None] * ks)
    ang = (a_im * dt)[..., None] * ks
    pw_re = mag * jnp.cos(ang)
    pw_im = mag * jnp.sin(ang)
    ab_re, ab_im = pw_re[..., 1], pw_im[..., 1]
    den = jnp.square(a_re) + jnp.square(a_im)
    nr = ab_re - 1.0
    f_re = (nr * a_re + ab_im * a_im) / den
    f_im = (ab_im * a_re - nr * a_im) / den
    b_re = b_re.astype(F32)
    b_im = b_im.astype(F32)
    bb_re = f_re[..., None] * b_re - f_im[..., None] * b_im
    bb_im = f_re[..., None] * b_im + f_im[..., None] * b_re
    c_re = c_re.astype(F32)
    c_im = c_im.astype(F32)
    ca_re = c_re[..., None] * pw_re[:, None] - c_im[..., None] * pw_im[:, None]
    ca_im = c_re[..., None] * pw_im[:, None] + c_im[..., None] * pw_re[:, None]
    eye = jnp.eye(SG_GROUPS, dtype=F32)

    kk = (jnp.einsum('gonk,gni->gkio', ca_re[..., :CHUNK], bb_re)
          - jnp.einsum('gonk,gni->gkio', ca_im[..., :CHUNK], bb_im))
    kk = kk.at[:, 0].add(jnp.eye(c, dtype=F32)[None] * d.astype(F32).reshape(g, c)[:, :, None])
    ktab = jnp.concatenate([jnp.zeros((g, 1, c, c), F32), kk], axis=1)
    dd = np.arange(N_TT)[:, None, None]
    s2 = np.arange(2)[None, :, None]
    t2 = np.arange(2)[None, None, :]
    lag = 2 * dd + t2 - s2 + 1
    x = ktab[:, lag].reshape(N_SG, SG_GROUPS, N_TT, 2, 2, c, c)
    tt = jnp.einsum('aldstio,lm->adslitmo', x, eye).reshape(N_SG, N_TT, MXU_TILE, MXU_TILE)

    rp_re = pw_re[..., CHUNK - 1::-1][..., :CHUNK]
    rp_im = pw_im[..., CHUNK - 1::-1][..., :CHUNK]
    v_re = jnp.einsum('gns,gni->gsin', rp_re, bb_re) - jnp.einsum('gns,gni->gsin', rp_im, bb_im)
    v_im = jnp.einsum('gns,gni->gsin', rp_re, bb_im) + jnp.einsum('gns,gni->gsin', rp_im, bb_re)

    def place_p(v):
        v = v.reshape(N_SG, SG_GROUPS, CHUNK, c, n)
        return jnp.einsum('alsin,lm->aslimn', v, eye).reshape(N_SG, FLAT, SG_STATE)

    wp = jnp.concatenate([place_p(v_re), place_p(v_im)], axis=-1)

    def place_c(m):
        m = jnp.transpose(m[..., 1:], (0, 2, 3, 1)).reshape(N_SG, SG_GROUPS, n, CHUNK, c)
        return jnp.einsum('alnto,lm->alntmo', m, eye).reshape(N_SG, SG_STATE, FLAT)

    wc = jnp.concatenate([place_c(ca_re), place_c(-ca_im)], axis=1)

    ar = pw_re[..., CHUNK].reshape(N_SG, 1, SG_STATE)
    ai = pw_im[..., CHUNK].reshape(N_SG, 1, SG_STATE)
    return tt.astype(BF16), wp.astype(BF16), wc.astype(BF16), ar, ai


def _attn_kernel(q_ref, kp_ref, kc_ref, vp_ref, vc_ref, o_ref, lse_ref):
    n = pl.program_id(1)
    qi = lax.broadcasted_iota(jnp.int32, (ATT_SPAN, ATT_SPAN), 0)
    ki = lax.broadcasted_iota(jnp.int32, (ATT_SPAN, ATT_SPAN), 1)
    mask_prev = jnp.logical_and(ki >= qi, n > 0)
    mask_cur = ki <= qi
    dn = (((1,), (1,)), ((), ()))
    outs, lses = [], []
    for h in range(ATT_HEADS_PER_GROUP):
        sl = slice(h * ATT_HEAD_DIM, (h + 1) * ATT_HEAD_DIM)
        q = q_ref[0, :, sl]
        sp = lax.dot_general(q, kp_ref[0, :, sl], dn, preferred_element_type=F32)
        sc = lax.dot_general(q, kc_ref[0, :, sl], dn, preferred_element_type=F32)
        sp = jnp.where(mask_prev, sp, NEG_INF)
        sc = jnp.where(mask_cur, sc, NEG_INF)
        m = jnp.maximum(jnp.max(sp, axis=-1, keepdims=True), jnp.max(sc, axis=-1, keepdims=True))
        pp = jnp.exp(sp - m)
        pc = jnp.exp(sc - m)
        den = jnp.sum(pp, axis=-1, keepdims=True) + jnp.sum(pc, axis=-1, keepdims=True)
        o = (jnp.dot(pp.astype(BF16), vp_ref[0, :, sl], preferred_element_type=F32)
             + jnp.dot(pc.astype(BF16), vc_ref[0, :, sl], preferred_element_type=F32)) / den
        outs.append(o)
        lses.append(jnp.broadcast_to(m + jnp.log(den), (ATT_SPAN, ATT_HEAD_DIM)))
    o_ref[0] = jnp.concatenate(outs, axis=-1).astype(BF16)
    lse_ref[0] = jnp.concatenate(lses, axis=-1)


def _banded_attention(q, k, v):
    nseq, length, w = q.shape
    blk = ATT_SPAN
    cur = pl.BlockSpec((1, blk, w), lambda s, n: (s, n, 0))
    prev = pl.BlockSpec((1, blk, w), lambda s, n: (s, jnp.maximum(n - 1, 0), 0))
    return pl.pallas_call(
        _attn_kernel,
        grid=(nseq, length // blk),
        in_specs=[cur, prev, cur, prev, cur],
        out_specs=[cur, cur],
        out_shape=[jax.ShapeDtypeStruct(q.shape, BF16), jax.ShapeDtypeStruct(q.shape, F32)],
        compiler_params=_cparams(2),
        name="attn",
    )(q, k, k, v, v)


def _mix_kernel(y_ref, o0_ref, o1_ref, o2_ref, l0_ref, l1_ref, l2_ref, gs_ref, ga_ref, h_ref,
                wglu_ref, bglu_ref, wup_ref, wmix_ref, bmix_ref, g_ref, b_ref, out_ref):
    z = jnp.dot(y_ref[...], wglu_ref[...], preferred_element_type=F32) + bglu_ref[...]
    b_ssm = z[:, :D_MODEL] * jax.nn.sigmoid(z[:, D_MODEL:])
    l0, l1, l2 = l0_ref[...], l1_ref[...], l2_ref[...]
    m = jnp.maximum(jnp.maximum(l0, l1), l2)
    e0, e1, e2 = jnp.exp(l0 - m), jnp.exp(l1 - m), jnp.exp(l2 - m)
    att = (e0 * o0_ref[...].astype(F32) + e1 * o1_ref[...].astype(F32)
           + e2 * o2_ref[...].astype(F32)) / (e0 + e1 + e2)
    b_att = jnp.dot(att.astype(BF16), wup_ref[...], preferred_element_type=F32)
    mixed = gs_ref[...].astype(F32) * b_ssm + ga_ref[...].astype(F32) * b_att
    r = jnp.dot(mixed.astype(BF16), wmix_ref[...], preferred_element_type=F32) + bmix_ref[...]
    out_ref[...] = _layer_norm(DEEPNORM_ALPHA * h_ref[...] + r, g_ref[...], b_ref[...])


def _mix(yg, outs, lses, gs, ga, h, wglu, bglu, wup, wmix, bmix, g, b):
    t = h.shape[0]
    tm = ROW_TILE
    row = lambda n: pl.BlockSpec((tm, n), lambda i: (i, 0))
    full = lambda a: pl.BlockSpec(a.shape, lambda i: (0,) * a.ndim)
    consts = (wglu, bglu, wup, wmix, bmix, g, b)
    return pl.pallas_call(
        _mix_kernel,
        grid=(t // tm,),
        in_specs=[row(SSM_WIDTH)] + [row(ATT_MERGED)] * 6 + [row(D_MODEL)] * 3 + [full(a) for a in consts],
        out_specs=row(D_MODEL),
        out_shape=jax.ShapeDtypeStruct((t, D_MODEL), F32),
        compiler_params=_cparams(1),
        name="mix",
    )(yg, *outs, *lses, gs, ga, h, *consts)


def _mem_kv_kernel(m_ref, w_ref, kv_ref):
    kv_ref[0] = jnp.dot(m_ref[0].astype(BF16), w_ref[...], preferred_element_type=F32).astype(BF16)


def _mem_kv(mem, w):
    bsz, ml, _ = mem.shape
    return pl.pallas_call(
        _mem_kv_kernel,
        grid=(bsz,),
        in_specs=[pl.BlockSpec((1, ml, D_MODEL), lambda b: (b, 0, 0)),
                  pl.BlockSpec(w.shape, lambda b: (0, 0))],
        out_specs=pl.BlockSpec((1, ml, 2 * D_MODEL), lambda b: (b, 0, 0)),
        out_shape=jax.ShapeDtypeStruct((bsz, ml, 2 * D_MODEL), BF16),
        compiler_params=_cparams(1),
        name="mem_kv",
    )(mem, w)


def _xattn_kernel(h_ref, kv_ref, wq_ref, wo_ref, g_ref, b_ref, out_ref):
    h = h_ref[...]
    q = jnp.dot(h.astype(BF16), wq_ref[...], preferred_element_type=F32) * XATT_SCALE
    dn = (((1,), (1,)), ((), ()))
    outs = []
    for hh in range(XATT_HEADS):
        sl = slice(hh * XATT_HEAD_DIM, (hh + 1) * XATT_HEAD_DIM)
        sv = slice(D_MODEL + hh * XATT_HEAD_DIM, D_MODEL + (hh + 1) * XATT_HEAD_DIM)
        s = lax.dot_general(q[:, sl].astype(BF16), kv_ref[0, :, sl], dn, preferred_element_type=F32)
        e = jnp.exp(s - jnp.max(s, axis=-1, keepdims=True))
        p = e / jnp.sum(e, axis=-1, keepdims=True)
        outs.append(jnp.dot(p.astype(BF16), kv_ref[0, :, sv], preferred_element_type=F32))
    o = jnp.concatenate(outs, axis=-1)
    xo = jnp.dot(o.astype(BF16), wo_ref[...], preferred_element_type=F32)
    out_ref[...] = _layer_norm(DEEPNORM_ALPHA * h + xo, g_ref[...], b_ref[...])


def _xattn(h, kv, wq, wo, g, b, seq):
    t = h.shape[0]
    tm = ROW_TILE
    per_b = seq // tm
    row = pl.BlockSpec((tm, D_MODEL), lambda i: (i, 0))
    full = lambda a: pl.BlockSpec(a.shape, lambda i: (0,) * a.ndim)
    return pl.pallas_call(
        _xattn_kernel,
        grid=(t // tm,),
        in_specs=[row, pl.BlockSpec((1,) + kv.shape[1:], lambda i: (i // per_b, 0, 0)),
                  full(wq), full(wo), full(g), full(b)],
        out_specs=row,
        out_shape=jax.ShapeDtypeStruct((t, D_MODEL), F32),
        compiler_params=_cparams(1),
        name="xattn",
    )(h, kv, wq, wo, g, b)


def _ffn_kernel(h_ref, w1_ref, b1_ref, w2_ref, b2_ref, g_ref, b_ref, out_ref):
    h = h_ref[...]
    a = jnp.dot(h.astype(BF16), w1_ref[...], preferred_element_type=F32) + b1_ref[...]
    a = jnp.square(jnp.maximum(a, 0.0))
    ff = jnp.dot(a.astype(BF16), w2_ref[...], preferred_element_type=F32) + b2_ref[...]
    out_ref[...] = _layer_norm(DEEPNORM_ALPHA * h + ff, g_ref[...], b_ref[...])


def _ffn(h, w1, b1, w2, b2, g, b):
    t = h.shape[0]
    tm = ROW_TILE
    row = pl.BlockSpec((tm, D_MODEL), lambda i: (i, 0))
    full = lambda a: pl.BlockSpec(a.shape, lambda i: (0,) * a.ndim)
    consts = (w1, b1, w2, b2, g, b)
    return pl.pallas_call(
        _ffn_kernel,
        grid=(t // tm,),
        in_specs=[row] + [full(a) for a in consts],
        out_specs=row,
        out_shape=jax.ShapeDtypeStruct((t, D_MODEL), F32),
        compiler_params=_cparams(1),
        name="ffn",
    )(h, *consts)


def _rope_tables(positions):
    inv_freq = ROPE_THETA ** (-jnp.arange(0, ROT_DIM, 2, dtype=F32) / ROT_DIM)
    ang = positions.astype(F32).reshape(-1, 1) * inv_freq
    cos, sin = jnp.cos(ang), jnp.sin(ang)
    t = ang.shape[0]
    half = ROT_DIM // 2
    pad = jnp.zeros((t, ATT_HEAD_DIM - ROT_DIM), F32)
    zh = jnp.zeros((t, half), F32)
    cc = jnp.concatenate([cos, cos, pad + 1.0], axis=-1)
    s1 = jnp.concatenate([-sin, zh, pad], axis=-1)
    s2 = jnp.concatenate([zh, sin, pad], axis=-1)
    return tuple(jnp.tile(a, (1, 2)) for a in (cc, s1, s2))


def kernel(x, mem, positions, ln_in_g, ln_in_b, w_in, b_in, ssm_log_dt, ssm_a_re, ssm_a_im, ssm_b_re, ssm_b_im, ssm_c_re, ssm_c_im, ssm_d, w_glu, b_glu, w_att_up, w_mix_out, b_mix_out, ln1_g, ln1_b, w_xq, w_xkv, w_xo, ln2_g, ln2_b, w_ff1, b_ff1, w_ff2, b_ff2, ln3_g, ln3_b):
    bsz, seq, _ = x.shape
    t = bsz * seq
    nc = t // CHUNK
    row2 = lambda a: a.reshape(1, -1).astype(F32)
    cc, s1, s2 = _rope_tables(positions)
    h = x.reshape(t, D_MODEL)
    for l in range(DEPTH):
        h, u, q, k, v, gs, ga = _ln_proj(
            h, row2(ln_in_g), row2(ln_in_b), w_in[l].astype(BF16), row2(b_in[l]), cc, s1, s2)

        tt, wp, wc, ar, ai = _ssm_matrices(ssm_log_dt[l], ssm_a_re[l], ssm_a_im[l], ssm_b_re[l],
                                           ssm_b_im[l], ssm_c_re[l], ssm_c_im[l], ssm_d[l])
        uflat = u.reshape(nc, CHUNK, N_SG, SG_LANES).transpose(2, 0, 1, 3).reshape(N_SG, nc, FLAT)
        p = _ssm_p(uflat, wp)
        hprev = _ssm_scan(p, ar, ai, bsz)
        yflat = _ssm_y(uflat, tt, hprev, wc)
        yg = yflat.reshape(N_SG, nc, CHUNK, SG_LANES).transpose(1, 2, 0, 3).reshape(t, SSM_WIDTH)

        outs, lses = [], []
        for gi, dil in enumerate(DILATIONS):
            def arrange(a):
                a = a[:, gi * ATT_MERGED:(gi + 1) * ATT_MERGED].reshape(bsz, seq // dil, dil, ATT_MERGED)
                return a.transpose(0, 2, 1, 3).reshape(bsz * dil, seq // dil, ATT_MERGED)

            def restore(a):
                a = a.reshape(bsz, dil, seq // dil, ATT_MERGED).transpose(0, 2, 1, 3)
                return a.reshape(t, ATT_MERGED)

            o_g, lse_g = _banded_attention(arrange(q), arrange(k), arrange(v))
            outs.append(restore(o_g))
            lses.append(restore(lse_g))

        h = _mix(yg, outs, lses, gs, ga, h, w_glu[l].astype(BF16), row2(b_glu[l]),
                 w_att_up[l].astype(BF16), w_mix_out[l].astype(BF16), row2(b_mix_out[l]),
                 row2(ln1_g[l]), row2(ln1_b[l]))

        kv = _mem_kv(mem, w_xkv[l].astype(BF16))
        h = _xattn(h, kv, w_xq[l].astype(BF16), w_xo[l].astype(BF16), row2(ln2_g[l]), row2(ln2_b[l]), seq)
        h = _ffn(h, w_ff1[l].astype(BF16), row2(b_ff1[l]), w_ff2[l].astype(BF16), row2(b_ff2[l]),
                 row2(ln3_g[l]), row2(ln3_b[l]))
    return h.reshape(bsz, seq, D_MODEL)
```

```python
import functools
import math

import jax
import jax.numpy as jnp
import numpy as np
from jax import lax
from jax.experimental import pallas as pl
from jax.experimental.pallas import tpu as pltpu

F32 = jnp.float32
BF16 = jnp.bfloat16

D_MODEL = 1024
SSM_GROUP = 16
SSM_WIDTH = 768
SSM_GROUPS = SSM_WIDTH // SSM_GROUP
SSM_STATE = 64
ATT_HEAD_DIM = 64
ATT_HEADS_PER_GROUP = 4
DILATIONS = (1, 4, 16)
ATT_SPAN = 128
ATT_WIDTH = 768
ATT_MERGED = ATT_HEADS_PER_GROUP * ATT_HEAD_DIM
ATT_SCALE = ATT_HEAD_DIM ** -0.5
ROT_DIM = ATT_HEAD_DIM // 4
ROPE_THETA = 500000.0
XATT_HEADS = 4
XATT_HEAD_DIM = D_MODEL // XATT_HEADS
XATT_SCALE = XATT_HEAD_DIM ** -0.5
D_FF = 4 * D_MODEL
DEPTH = 1
DEEPNORM_ALPHA = (2 * DEPTH) ** 0.25
LN_EPS = 1e-5
NEG_INF = -1e30
LOG2E = math.log2(math.e)
LN2 = math.log(2.0)

LANES = 128
CHUNK = 16
SG_GROUPS = LANES // SSM_GROUP
N_SG = SSM_GROUPS // SG_GROUPS
FLAT = CHUNK * LANES
SG_STATE = SG_GROUPS * SSM_STATE
MXU_TILE = 256
N_TT = FLAT // MXU_TILE

ROW_TILE = 256
QKV_OFF = SSM_WIDTH
GATE_OFF = SSM_WIDTH + 3 * ATT_WIDTH
VMEM_LIMIT = 56 * 1024 * 1024


def _cparams(n_axes):
    return pltpu.CompilerParams(dimension_semantics=("parallel",) * n_axes,
                                vmem_limit_bytes=VMEM_LIMIT)


def _layer_norm(x, g, b):
    mu = jnp.mean(x, axis=-1, keepdims=True)
    xc = x - mu
    var = jnp.mean(xc * xc, axis=-1, keepdims=True)
    return xc * lax.rsqrt(var + LN_EPS) * g + b


def _ln_proj_kernel(x_ref, g_ref, b_ref, w_ref, bi_ref, c_ref, s1_ref, s2_ref,
                    h_ref, u_ref, q0_ref, k0_ref, v0_ref, q1_ref, k1_ref, v1_ref, q2_ref, k2_ref, v2_ref,
                    gs_ref, ga_ref, scr_ref):
    tm = x_ref.shape[0]
    h = _layer_norm(x_ref[...], g_ref[...], b_ref[...])
    h_ref[...] = h
    proj = jnp.dot(h.astype(BF16), w_ref[...], preferred_element_type=F32) + bi_ref[...]
    gs_ref[...] = jax.nn.sigmoid(proj[:, GATE_OFF:GATE_OFF + D_MODEL]).astype(BF16)
    ga_ref[...] = jax.nn.sigmoid(proj[:, GATE_OFF + D_MODEL:GATE_OFF + 2 * D_MODEL]).astype(BF16)

    for cb in range(SSM_WIDTH // LANES):
        scr_ref[cb] = proj[:, cb * LANES:(cb + 1) * LANES]
    cc, s1, s2 = c_ref[...], s1_ref[...], s2_ref[...]
    half = ROT_DIM // 2
    for off, scale in ((QKV_OFF, ATT_SCALE * LOG2E), (QKV_OFF + ATT_WIDTH, 1.0)):
        for i in range(ATT_WIDTH // LANES):
            t = proj[:, off + i * LANES: off + (i + 1) * LANES]
            r = t * cc + pltpu.roll(t, LANES - half, 1) * s1 + pltpu.roll(t, half, 1) * s2
            scr_ref[off // LANES + i] = r * scale
    off_v = QKV_OFF + 2 * ATT_WIDTH
    for i in range(ATT_WIDTH // LANES):
        scr_ref[off_v // LANES + i] = proj[:, off_v + i * LANES: off_v + (i + 1) * LANES]

    for j in range(CHUNK):
        for sg in range(N_SG):
            rows = scr_ref[sg, pl.ds(j, tm // CHUNK, stride=CHUNK), :]
            u_ref[sg, :, j * LANES:(j + 1) * LANES] = rows.astype(BF16)

    qkv_refs = ((q0_ref, k0_ref, v0_ref), (q1_ref, k1_ref, v1_ref), (q2_ref, k2_ref, v2_ref))
    for gi, dil in enumerate(DILATIONS):
        for a, ref in enumerate(qkv_refs[gi]):
            cb0 = (QKV_OFF + a * ATT_WIDTH + gi * ATT_MERGED) // LANES
            for r in range(dil):
                for e in range(ATT_MERGED // LANES):
                    if dil == 1:
                        rows = scr_ref[cb0 + e]
                    else:
                        rows = scr_ref[cb0 + e, pl.ds(r, tm // dil, stride=dil), :]
                    ref[0, r, :, e * LANES:(e + 1) * LANES] = rows.astype(BF16)


def _ln_proj(x2, g, b, w, bi, cc, s1, s2, bsz, seq):
    t = x2.shape[0]
    tm = ROW_TILE
    per_b = seq // tm
    row = lambda n: pl.BlockSpec((tm, n), lambda i: (i, 0))
    full = lambda a: pl.BlockSpec(a.shape, lambda i: (0,) * a.ndim)
    grp = lambda d: pl.BlockSpec((1, d, tm // d, ATT_MERGED), lambda i: (i // per_b, 0, i % per_b, 0))
    outs = [jax.ShapeDtypeStruct((t, D_MODEL), F32),
            jax.ShapeDtypeStruct((N_SG, t // CHUNK, FLAT), BF16)]
    out_specs = [row(D_MODEL), pl.BlockSpec((N_SG, tm // CHUNK, FLAT), lambda i: (0, i, 0))]
    for d in DILATIONS:
        outs += [jax.ShapeDtypeStruct((bsz, d, seq // d, ATT_MERGED), BF16)] * 3
        out_specs += [grp(d)] * 3
    outs += [jax.ShapeDtypeStruct((t, D_MODEL), BF16)] * 2
    out_specs += [row(D_MODEL)] * 2
    return pl.pallas_call(
        _ln_proj_kernel,
        grid=(t // tm,),
        in_specs=[row(D_MODEL), full(g), full(b), full(w), full(bi), row(LANES), row(LANES), row(LANES)],
        out_specs=out_specs,
        out_shape=outs,
        scratch_shapes=[pltpu.VMEM((GATE_OFF // LANES, tm, LANES), F32)],
        compiler_params=_cparams(1),
        name="ln_proj",
    )(x2, g, b, w, bi, cc, s1, s2)


def _ssm_p_kernel(u_ref, wp_ref, p_ref):
    p_ref[0] = jnp.dot(u_ref[0], wp_ref[0], preferred_element_type=F32)


def _ssm_p(uflat, wp):
    nc = uflat.shape[1]
    tn = ROW_TILE
    return pl.pallas_call(
        _ssm_p_kernel,
        grid=(N_SG, nc // tn),
        in_specs=[pl.BlockSpec((1, tn, FLAT), lambda s, i: (s, i, 0)),
                  pl.BlockSpec((1, FLAT, 2 * SG_STATE), lambda s, i: (s, 0, 0))],
        out_specs=pl.BlockSpec((1, tn, 2 * SG_STATE), lambda s, i: (s, i, 0)),
        out_shape=jax.ShapeDtypeStruct((N_SG, nc, 2 * SG_STATE), F32),
        compiler_params=_cparams(2),
        name="ssm_p",
    )(uflat, wp)


def _ssm_scan_kernel(p_ref, ar_ref, ai_ref, h_ref, *, steps):
    ar = ar_ref[0]
    ai = ai_ref[0]

    def body(c, carry):
        hr, hi = carry
        h_ref[0, pl.ds(c, 1), 0:SG_STATE] = hr
        h_ref[0, pl.ds(c, 1), SG_STATE:2 * SG_STATE] = hi
        pr = p_ref[0, pl.ds(c, 1), 0:SG_STATE]
        pi = p_ref[0, pl.ds(c, 1), SG_STATE:2 * SG_STATE]
        return ar * hr - ai * hi + pr, ar * hi + ai * hr + pi

    z = jnp.zeros((1, SG_STATE), F32)
    lax.fori_loop(0, steps, body, (z, z))


def _ssm_scan(p, ar, ai, bsz):
    nc = p.shape[1]
    steps = nc // bsz
    return pl.pallas_call(
        functools.partial(_ssm_scan_kernel, steps=steps),
        grid=(N_SG, bsz),
        in_specs=[pl.BlockSpec((1, steps, 2 * SG_STATE), lambda s, b: (s, b, 0)),
                  pl.BlockSpec((1, 1, SG_STATE), lambda s, b: (s, 0, 0)),
                  pl.BlockSpec((1, 1, SG_STATE), lambda s, b: (s, 0, 0))],
        out_specs=pl.BlockSpec((1, steps, 2 * SG_STATE), lambda s, b: (s, b, 0)),
        out_shape=jax.ShapeDtypeStruct(p.shape, F32),
        compiler_params=_cparams(2),
        name="ssm_scan",
    )(p, ar, ai)


def _ssm_y_kernel(u_ref, t_ref, h_ref, wc_ref, y_ref, scr_ref):
    tn = u_ref.shape[1]
    hb = h_ref[0].astype(BF16)
    for jo in range(N_TT):
        cols = slice(jo * MXU_TILE, (jo + 1) * MXU_TILE)
        acc = jnp.dot(hb, wc_ref[0, :, cols], preferred_element_type=F32)
        for ji in range(jo + 1):
            acc += jnp.dot(u_ref[0, :, ji * MXU_TILE:(ji + 1) * MXU_TILE], t_ref[0, jo - ji],
                           preferred_element_type=F32)
        yg = jax.nn.gelu(acc, approximate=True)
        for e in range(MXU_TILE // LANES):
            j = jo * (MXU_TILE // LANES) + e
            scr_ref[pl.ds(j, tn, stride=CHUNK), :] = yg[:, e * LANES:(e + 1) * LANES]
    y_ref[...] = scr_ref[...].astype(BF16)


def _ssm_y(uflat, tt, hprev, wc):
    nc = uflat.shape[1]
    tn = ROW_TILE
    return pl.pallas_call(
        _ssm_y_kernel,
        grid=(N_SG, nc // tn),
        in_specs=[pl.BlockSpec((1, tn, FLAT), lambda s, i: (s, i, 0)),
                  pl.BlockSpec((1, N_TT, MXU_TILE, MXU_TILE), lambda s, i: (s, 0, 0, 0)),
                  pl.BlockSpec((1, tn, 2 * SG_STATE), lambda s, i: (s, i, 0)),
                  pl.BlockSpec((1, 2 * SG_STATE, FLAT), lambda s, i: (s, 0, 0))],
        out_specs=pl.BlockSpec((tn * CHUNK, LANES), lambda s, i: (i, s)),
        out_shape=jax.ShapeDtypeStruct((nc * CHUNK, SSM_WIDTH), BF16),
        scratch_shapes=[pltpu.VMEM((tn * CHUNK, LANES), F32)],
        compiler_params=_cparams(2),
        name="ssm_y",
    )(uflat, tt, hprev, wc)


def _ssm_matrices(log_dt, a_re, a_im, b_re, b_im, c_re, c_im, d):
    g, n, c = SSM_GROUPS, SSM_STATE, SSM_GROUP
    dt = jnp.exp(log_dt.astype(F32))[:, None]
    a_re = a_re.astype(F32)
    a_im = a_im.astype(F32)
    ks = jnp.arange(CHUNK + 1, dtype=F32)
    mag = jnp.exp((a_re * dt)[..., None] * ks)
    ang = (a_im * dt)[..., None] * ks
    pw_re = mag * jnp.cos(ang)
    pw_im = mag * jnp.sin(ang)
    ab_re, ab_im = pw_re[..., 1], pw_im[..., 1]
    den = jnp.square(a_re) + jnp.square(a_im)
    nr = ab_re - 1.0
    f_re = (nr * a_re + ab_im * a_im) / den
    f_im = (ab_im * a_re - nr * a_im) / den
    b_re = b_re.astype(F32)
    b_im = b_im.astype(F32)
    bb_re = f_re[..., None] * b_re - f_im[..., None] * b_im
    bb_im = f_re[..., None] * b_im + f_im[..., None] * b_re
    c_re = c_re.astype(F32)
    c_im = c_im.astype(F32)
    ca_re = c_re[..., None] * pw_re[:, None] - c_im[..., None] * pw_im[:, None]
    ca_im = c_re[..., None] * pw_im[:, None] + c_im[..., None] * pw_re[:, None]
    same_group = jnp.asarray(np.eye(SG_GROUPS, dtype=bool))

    kk = (jnp.einsum('gonk,gni->gkio', ca_re[..., :CHUNK], bb_re)
          - jnp.einsum('gonk,gni->gkio', ca_im[..., :CHUNK], bb_im))
    kk = kk.at[:, 0].add(jnp.eye(c, dtype=F32)[None] * d.astype(F32).reshape(g, c)[:, :, None])
    ktab = jnp.concatenate([jnp.zeros((g, 1, c, c), F32), kk], axis=1)
    dd = np.arange(N_TT)[:, None, None]
    s2 = np.arange(2)[None, :, None]
    t2 = np.arange(2)[None, None, :]
    lag = 2 * dd + t2 - s2 + 1
    x = ktab[:, lag].reshape(N_SG, SG_GROUPS, N_TT, 2, 2, c, c)
    x = jnp.transpose(x, (0, 2, 3, 1, 5, 4, 6))
    tt = jnp.where(same_group[None, None, None, :, None, None, :, None], x[..., None, :], 0.0)
    tt = tt.reshape(N_SG, N_TT, MXU_TILE, MXU_TILE)

    rp_re = pw_re[..., CHUNK - 1::-1][..., :CHUNK]
    rp_im = pw_im[..., CHUNK - 1::-1][..., :CHUNK]
    v_re = jnp.einsum('gns,gni->gsin', rp_re, bb_re) - jnp.einsum('gns,gni->gsin', rp_im, bb_im)
    v_im = jnp.einsum('gns,gni->gsin', rp_re, bb_im) + jnp.einsum('gns,gni->gsin', rp_im, bb_re)

    def place_p(v):
        v = jnp.transpose(v.reshape(N_SG, SG_GROUPS, CHUNK, c, n), (0, 2, 1, 3, 4)).astype(BF16)
        v = jnp.where(same_group[None, None, :, None, :, None], v[:, :, :, :, None, :], 0)
        return v.reshape(N_SG, FLAT, SG_STATE)

    wp = jnp.concatenate([place_p(v_re), place_p(v_im)], axis=-1)

    def place_c(m):
        m = jnp.transpose(m[..., 1:], (0, 2, 3, 1)).reshape(N_SG, SG_GROUPS, n, CHUNK, c).astype(BF16)
        m = jnp.where(same_group[None, :, None, None, :, None], m[:, :, :, :, None, :], 0)
        return m.reshape(N_SG, SG_STATE, FLAT)

    wc = jnp.concatenate([place_c(ca_re), place_c(-ca_im)], axis=1)

    ar = pw_re[..., CHUNK].reshape(N_SG, 1, SG_STATE)
    ai = pw_im[..., CHUNK].reshape(N_SG, 1, SG_STATE)
    return tt.astype(BF16), wp, wc, ar, ai


ATT_UNITS_IN_FLIGHT = 4


def _attn_kernel(q_ref, kp_ref, kc_ref, vp_ref, vc_ref, o_ref, lse_ref, ks_ref, vs_ref):
    qb = q_ref.shape[1]
    blk = ATT_SPAN
    first = pl.program_id(1) == 0

    lane = lax.broadcasted_iota(jnp.int32, (1, ATT_MERGED), 1)
    head_sel = (lane % LANES < ATT_HEAD_DIM, lane % LANES >= ATT_HEAD_DIM)
    for src_p, src_c, dst in ((kp_ref, kc_ref, ks_ref), (vp_ref, vc_ref, vs_ref)):
        for e in range(2):
            xp, xc = src_p[0], src_c[0]
            dst[e, 0:blk] = jnp.where(head_sel[e], xp, jnp.zeros_like(xp))
            dst[e, blk:blk + qb] = jnp.where(head_sel[e], xc, jnp.zeros_like(xc))

    qi = lax.broadcasted_iota(jnp.int32, (blk, 2 * blk), 0)
    kk = lax.broadcasted_iota(jnp.int32, (blk, 2 * blk), 1)
    band = jnp.logical_and(kk >= qi, kk <= qi + blk)
    band_first = jnp.logical_and(band, jnp.logical_or(kk >= blk, jnp.logical_not(first)))
    low = lax.broadcasted_iota(jnp.int32, (blk, LANES), 1) < ATT_HEAD_DIM
    dn = (((1,), (1,)), ((), ()))

    units = [(j, pair) for j in range(qb // blk) for pair in range(ATT_MERGED // LANES)]
    for g0 in range(0, len(units), ATT_UNITS_IN_FLIGHT):
        group = units[g0:g0 + ATT_UNITS_IN_FLIGHT]
        scores = []
        for j, pair in group:
            cols = slice(pair * LANES, (pair + 1) * LANES)
            keys = slice(j * blk, (j + 2) * blk)
            kcat = jnp.concatenate([ks_ref[0, keys, cols], ks_ref[1, keys, cols]], axis=0)
            s = lax.dot_general(q_ref[0, j * blk:(j + 1) * blk, cols], kcat, dn,
                                preferred_element_type=F32)
            mask = band_first if j == 0 else band
            scores.append([jnp.where(mask, s[:, e * 2 * blk:(e + 1) * 2 * blk], NEG_INF) for e in range(2)])
        maxes = [[jnp.max(jnp.maximum(se[:, :blk], se[:, blk:]), axis=-1, keepdims=True) for se in su]
                 for su in scores]
        probs = [[jnp.exp2(se - me) for se, me in zip(su, mu)] for su, mu in zip(scores, maxes)]
        dens = [[jnp.sum(pe[:, :blk] + pe[:, blk:], axis=-1, keepdims=True) for pe in pu] for pu in probs]
        for (j, pair), pu, mu, du in zip(group, probs, maxes, dens):
            cols = slice(pair * LANES, (pair + 1) * LANES)
            rows = slice(j * blk, (j + 1) * blk)
            keys = slice(j * blk, (j + 2) * blk)
            vcat = jnp.concatenate([vs_ref[0, keys, cols], vs_ref[1, keys, cols]], axis=0)
            pcat = jnp.concatenate([pu[0].astype(BF16), pu[1].astype(BF16)], axis=1)
            num = jnp.dot(pcat, vcat, preferred_element_type=F32)
            o_ref[0, rows, cols] = (num * jnp.where(low, 1.0 / du[0], 1.0 / du[1])).astype(BF16)
            lse_ref[0, rows, cols] = jnp.where(low, mu[0] * LN2 + jnp.log(du[0]), mu[1] * LN2 + jnp.log(du[1]))


def _banded_attention(q, k, v):
    nseq, length, w = q.shape
    blk = ATT_SPAN
    qb = min(length, 4 * blk)
    per = qb // blk
    cur = pl.BlockSpec((1, qb, w), lambda s, n: (s, n, 0))
    prev = pl.BlockSpec((1, blk, w), lambda s, n: (s, jnp.maximum(n * per - 1, 0), 0))
    return pl.pallas_call(
        _attn_kernel,
        grid=(nseq, length // qb),
        in_specs=[cur, prev, cur, prev, cur],
        out_specs=[cur, cur],
        out_shape=[jax.ShapeDtypeStruct(q.shape, BF16), jax.ShapeDtypeStruct(q.shape, F32)],
        scratch_shapes=[pltpu.VMEM((2, blk + qb, w), BF16)] * 2,
        compiler_params=_cparams(2),
        name="attn",
    )(q, k, k, v, v)


def _mix_kernel(y_ref, o0_ref, o1_ref, o2_ref, l0_ref, l1_ref, l2_ref, gs_ref, ga_ref, h_ref,
                wglu_ref, bglu_ref, wup_ref, wmix_ref, bmix_ref, g_ref, b_ref, out_ref, scr_ref):
    tm = h_ref.shape[0]
    z = jnp.dot(y_ref[...], wglu_ref[...], preferred_element_type=F32) + bglu_ref[...]
    b_ssm = z[:, :D_MODEL] * jax.nn.sigmoid(z[:, D_MODEL:])

    def natural(ref, dil):
        if dil == 1:
            return ref[0, 0].astype(F32)
        for r in range(dil):
            for e in range(ATT_MERGED // LANES):
                scr_ref[e, pl.ds(r, tm // dil, stride=dil), :] = ref[0, r, :, e * LANES:(e + 1) * LANES].astype(F32)
        return jnp.concatenate([scr_ref[e] for e in range(ATT_MERGED // LANES)], axis=-1)

    o_refs = (o0_ref, o1_ref, o2_ref)
    l_refs = (l0_ref, l1_ref, l2_ref)
    ls = [natural(l_refs[gi], dil) for gi, dil in enumerate(DILATIONS)]
    m = jnp.maximum(jnp.maximum(ls[0], ls[1]), ls[2])
    es = [jnp.exp(l - m) for l in ls]
    att = es[0] * natural(o_refs[0], DILATIONS[0])
    for gi in (1, 2):
        att += es[gi] * natural(o_refs[gi], DILATIONS[gi])
    att = att / (es[0] + es[1] + es[2])
    b_att = jnp.dot(att.astype(BF16), wup_ref[...], preferred_element_type=F32)
    mixed = gs_ref[...].astype(F32) * b_ssm + ga_ref[...].astype(F32) * b_att
    r = jnp.dot(mixed.astype(BF16), wmix_ref[...], preferred_element_type=F32) + bmix_ref[...]
    out_ref[...] = _layer_norm(DEEPNORM_ALPHA * h_ref[...] + r, g_ref[...], b_ref[...])


def _mix(yg, outs, lses, gs, ga, h, wglu, bglu, wup, wmix, bmix, g, b, seq):
    t = h.shape[0]
    tm = ROW_TILE
    per_b = seq // tm
    row = lambda n: pl.BlockSpec((tm, n), lambda i: (i, 0))
    full = lambda a: pl.BlockSpec(a.shape, lambda i: (0,) * a.ndim)
    grp = lambda d: pl.BlockSpec((1, d, tm // d, ATT_MERGED), lambda i: (i // per_b, 0, i % per_b, 0))
    consts = (wglu, bglu, wup, wmix, bmix, g, b)
    return pl.pallas_call(
        _mix_kernel,
        grid=(t // tm,),
        in_specs=([row(SSM_WIDTH)] + [grp(d) for d in DILATIONS] * 2 + [row(D_MODEL)] * 3
                  + [full(a) for a in consts]),
        out_specs=row(D_MODEL),
        out_shape=jax.ShapeDtypeStruct((t, D_MODEL), F32),
        scratch_shapes=[pltpu.VMEM((ATT_MERGED // LANES, tm, LANES), F32)],
        compiler_params=_cparams(1),
        name="mix",
    )(yg, *outs, *lses, gs, ga, h, *consts)


def _mem_kv_kernel(m_ref, w_ref, kv_ref):
    kv_ref[0] = jnp.dot(m_ref[0].astype(BF16), w_ref[...], preferred_element_type=F32).astype(BF16)


def _mem_kv(mem, w):
    bsz, ml, _ = mem.shape
    return pl.pallas_call(
        _mem_kv_kernel,
        grid=(bsz,),
        in_specs=[pl.BlockSpec((1, ml, D_MODEL), lambda b: (b, 0, 0)),
                  pl.BlockSpec(w.shape, lambda b: (0, 0))],
        out_specs=pl.BlockSpec((1, ml, 2 * D_MODEL), lambda b: (b, 0, 0)),
        out_shape=jax.ShapeDtypeStruct((bsz, ml, 2 * D_MODEL), BF16),
        compiler_params=_cparams(1),
        name="mem_kv",
    )(mem, w)


def _xattn_kernel(h_ref, kv_ref, wq_ref, wo_ref, g_ref, b_ref, out_ref):
    h = h_ref[...]
    q = jnp.dot(h.astype(BF16), wq_ref[...], preferred_element_type=F32) * XATT_SCALE
    dn = (((1,), (1,)), ((), ()))
    outs = []
    for hh in range(XATT_HEADS):
        sl = slice(hh * XATT_HEAD_DIM, (hh + 1) * XATT_HEAD_DIM)
        sv = slice(D_MODEL + hh * XATT_HEAD_DIM, D_MODEL + (hh + 1) * XATT_HEAD_DIM)
        s = lax.dot_general(q[:, sl].astype(BF16), kv_ref[0, :, sl], dn, preferred_element_type=F32)
        e = jnp.exp(s - jnp.max(s, axis=-1, keepdims=True))
        p = e / jnp.sum(e, axis=-1, keepdims=True)
        outs.append(jnp.dot(p.astype(BF16), kv_ref[0, :, sv], preferred_element_type=F32))
    o = jnp.concatenate(outs, axis=-1)
    xo = jnp.dot(o.astype(BF16), wo_ref[...], preferred_element_type=F32)
    out_ref[...] = _layer_norm(DEEPNORM_ALPHA * h + xo, g_ref[...], b_ref[...])


def _xattn(h, kv, wq, wo, g, b, seq):
    t = h.shape[0]
    tm = ROW_TILE
    per_b = seq // tm
    row = pl.BlockSpec((tm, D_MODEL), lambda i: (i, 0))
    full = lambda a: pl.BlockSpec(a.shape, lambda i: (0,) * a.ndim)
    return pl.pallas_call(
        _xattn_kernel,
        grid=(t // tm,),
        in_specs=[row, pl.BlockSpec((1,) + kv.shape[1:], lambda i: (i // per_b, 0, 0)),
                  full(wq), full(wo), full(g), full(b)],
        out_specs=row,
        out_shape=jax.ShapeDtypeStruct((t, D_MODEL), F32),
        compiler_params=_cparams(1),
        name="xattn",
    )(h, kv, wq, wo, g, b)


def _ffn_kernel(h_ref, w1_ref, b1_ref, w2_ref, b2_ref, g_ref, b_ref, out_ref):
    h = h_ref[...]
    a = jnp.dot(h.astype(BF16), w1_ref[...], preferred_element_type=F32) + b1_ref[...]
    a = jnp.square(jnp.maximum(a, 0.0))
    ff = jnp.dot(a.astype(BF16), w2_ref[...], preferred_element_type=F32) + b2_ref[...]
    out_ref[...] = _layer_norm(DEEPNORM_ALPHA * h + ff, g_ref[...], b_ref[...])


def _ffn(h, w1, b1, w2, b2, g, b):
    t = h.shape[0]
    tm = ROW_TILE
    row = pl.BlockSpec((tm, D_MODEL), lambda i: (i, 0))
    full = lambda a: pl.BlockSpec(a.shape, lambda i: (0,) * a.ndim)
    consts = (w1, b1, w2, b2, g, b)
    return pl.pallas_call(
        _ffn_kernel,
        grid=(t // tm,),
        in_specs=[row] + [full(a) for a in consts],
        out_specs=row,
        out_shape=jax.ShapeDtypeStruct((t, D_MODEL), F32),
        compiler_params=_cparams(1),
        name="ffn",
    )(h, *consts)


def _rope_tables(positions):
    inv_freq = ROPE_THETA ** (-jnp.arange(0, ROT_DIM, 2, dtype=F32) / ROT_DIM)
    ang = positions.astype(F32).reshape(-1, 1) * inv_freq
    cos, sin = jnp.cos(ang), jnp.sin(ang)
    t = ang.shape[0]
    half = ROT_DIM // 2
    pad = jnp.zeros((t, ATT_HEAD_DIM - ROT_DIM), F32)
    zh = jnp.zeros((t, half), F32)
    cc = jnp.concatenate([cos, cos, pad + 1.0], axis=-1)
    s1 = jnp.concatenate([-sin, zh, pad], axis=-1)
    s2 = jnp.concatenate([zh, sin, pad], axis=-1)
    return tuple(jnp.tile(a, (1, LANES // ATT_HEAD_DIM)) for a in (cc, s1, s2))


def kernel(x, mem, positions, ln_in_g, ln_in_b, w_in, b_in, ssm_log_dt, ssm_a_re, ssm_a_im, ssm_b_re, ssm_b_im, ssm_c_re, ssm_c_im, ssm_d, w_glu, b_glu, w_att_up, w_mix_out, b_mix_out, ln1_g, ln1_b, w_xq, w_xkv, w_xo, ln2_g, ln2_b, w_ff1, b_ff1, w_ff2, b_ff2, ln3_g, ln3_b):
    bsz, seq, _ = x.shape
    t = bsz * seq
    row2 = lambda a: a.reshape(1, -1).astype(F32)
    cc, s1, s2 = _rope_tables(positions)
    h = x.reshape(t, D_MODEL)
    for l in range(DEPTH):
        h, uflat, *qkv, gs, ga = _ln_proj(
            h, row2(ln_in_g), row2(ln_in_b), w_in[l].astype(BF16), row2(b_in[l]), cc, s1, s2, bsz, seq)

        tt, wp, wc, ar, ai = _ssm_matrices(ssm_log_dt[l], ssm_a_re[l], ssm_a_im[l], ssm_b_re[l],
                                           ssm_b_im[l], ssm_c_re[l], ssm_c_im[l], ssm_d[l])
        p = _ssm_p(uflat, wp)
        hprev = _ssm_scan(p, ar, ai, bsz)
        yg = _ssm_y(uflat, tt, hprev, wc)

        outs, lses = [], []
        for gi, dil in enumerate(DILATIONS):
            qg, kg, vg = (a.reshape(bsz * dil, seq // dil, ATT_MERGED) for a in qkv[3 * gi:3 * gi + 3])
            o_g, lse_g = _banded_attention(qg, kg, vg)
            outs.append(o_g.reshape(bsz, dil, seq // dil, ATT_MERGED))
            lses.append(lse_g.reshape(bsz, dil, seq // dil, ATT_MERGED))

        h = _mix(yg, outs, lses, gs, ga, h, w_glu[l].astype(BF16), row2(b_glu[l]),
                 w_att_up[l].astype(BF16), w_mix_out[l].astype(BF16), row2(b_mix_out[l]),
                 row2(ln1_g[l]), row2(ln1_b[l]), seq)

        kv = _mem_kv(mem, w_xkv[l].astype(BF16))
        h = _xattn(h, kv, w_xq[l].astype(BF16), w_xo[l].astype(BF16), row2(ln2_g[l]), row2(ln2_b[l]), seq)
        h = _ffn(h, w_ff1[l].astype(BF16), row2(b_ff1[l]), w_ff2[l].astype(BF16), row2(b_ff2[l]),
                 row2(ln3_g[l]), row2(ln3_b[l]))
    return h.reshape(bsz, seq, D_MODEL)
```

```python
import functools
import math

import jax
import jax.numpy as jnp
import numpy as np
from jax import lax
from jax.experimental import pallas as pl
from jax.experimental.pallas import tpu as pltpu

F32 = jnp.float32
BF16 = jnp.bfloat16

D_MODEL = 1024
SSM_GROUP = 16
SSM_WIDTH = 768
SSM_GROUPS = SSM_WIDTH // SSM_GROUP
SSM_STATE = 64
ATT_HEAD_DIM = 64
ATT_HEADS_PER_GROUP = 4
DILATIONS = (1, 4, 16)
ATT_SPAN = 128
ATT_WIDTH = 768
ATT_MERGED = ATT_HEADS_PER_GROUP * ATT_HEAD_DIM
ATT_SCALE = ATT_HEAD_DIM ** -0.5
ROT_DIM = ATT_HEAD_DIM // 4
ROPE_THETA = 500000.0
XATT_HEADS = 4
XATT_HEAD_DIM = D_MODEL // XATT_HEADS
XATT_SCALE = XATT_HEAD_DIM ** -0.5
D_FF = 4 * D_MODEL
DEPTH = 1
DEEPNORM_ALPHA = (2 * DEPTH) ** 0.25
LN_EPS = 1e-5
NEG_INF = -1e30
LOG2E = math.log2(math.e)
LN2 = math.log(2.0)

LANES = 128
CHUNK = 16
SG_GROUPS = LANES // SSM_GROUP
N_SG = SSM_GROUPS // SG_GROUPS
FLAT = CHUNK * LANES
SG_STATE = SG_GROUPS * SSM_STATE
MXU_TILE = 256
N_TT = FLAT // MXU_TILE

ROW_TILE = 256
QKV_OFF = SSM_WIDTH
GATE_OFF = SSM_WIDTH + 3 * ATT_WIDTH
VMEM_LIMIT = 56 * 1024 * 1024


def _cparams(n_axes):
    return pltpu.CompilerParams(dimension_semantics=("parallel",) * n_axes,
                                vmem_limit_bytes=VMEM_LIMIT)


def _layer_norm(x, g, b):
    mu = jnp.mean(x, axis=-1, keepdims=True)
    xc = x - mu
    var = jnp.mean(xc * xc, axis=-1, keepdims=True)
    return xc * lax.rsqrt(var + LN_EPS) * g + b


def _ln_proj_kernel(x_ref, g_ref, b_ref, w_ref, bi_ref, c_ref, s1_ref, s2_ref,
                    h_ref, u_ref, q0_ref, k0_ref, v0_ref, q1_ref, k1_ref, v1_ref, q2_ref, k2_ref, v2_ref,
                    gs_ref, ga_ref, scr_ref):
    tm = x_ref.shape[0]
    h = _layer_norm(x_ref[...], g_ref[...], b_ref[...])
    h_ref[...] = h
    proj = jnp.dot(h.astype(BF16), w_ref[...], preferred_element_type=F32) + bi_ref[...]
    gs_ref[...] = jax.nn.sigmoid(proj[:, GATE_OFF:GATE_OFF + D_MODEL]).astype(BF16)
    ga_ref[...] = jax.nn.sigmoid(proj[:, GATE_OFF + D_MODEL:GATE_OFF + 2 * D_MODEL]).astype(BF16)

    for cb in range(SSM_WIDTH // LANES):
        scr_ref[cb] = proj[:, cb * LANES:(cb + 1) * LANES]
    cc, s1, s2 = c_ref[...], s1_ref[...], s2_ref[...]
    half = ROT_DIM // 2
    for off, scale in ((QKV_OFF, ATT_SCALE * LOG2E), (QKV_OFF + ATT_WIDTH, 1.0)):
        for i in range(ATT_WIDTH // LANES):
            t = proj[:, off + i * LANES: off + (i + 1) * LANES]
            r = t * cc + pltpu.roll(t, LANES - half, 1) * s1 + pltpu.roll(t, half, 1) * s2
            scr_ref[off // LANES + i] = r * scale
    off_v = QKV_OFF + 2 * ATT_WIDTH
    for i in range(ATT_WIDTH // LANES):
        scr_ref[off_v // LANES + i] = proj[:, off_v + i * LANES: off_v + (i + 1) * LANES]

    for j in range(CHUNK):
        for sg in range(N_SG):
            rows = scr_ref[sg, pl.ds(j, tm // CHUNK, stride=CHUNK), :]
            u_ref[sg, :, j * LANES:(j + 1) * LANES] = rows.astype(BF16)

    qkv_refs = ((q0_ref, k0_ref, v0_ref), (q1_ref, k1_ref, v1_ref), (q2_ref, k2_ref, v2_ref))
    for gi, dil in enumerate(DILATIONS):
        for a, ref in enumerate(qkv_refs[gi]):
            cb0 = (QKV_OFF + a * ATT_WIDTH + gi * ATT_MERGED) // LANES
            for r in range(dil):
                for e in range(ATT_MERGED // LANES):
                    if dil == 1:
                        rows = scr_ref[cb0 + e]
                    else:
                        rows = scr_ref[cb0 + e, pl.ds(r, tm // dil, stride=dil), :]
                    ref[0, r, :, e * LANES:(e + 1) * LANES] = rows.astype(BF16)


def _ln_proj(x2, g, b, w, bi, cc, s1, s2, bsz, seq):
    t = x2.shape[0]
    tm = ROW_TILE
    per_b = seq // tm
    row = lambda n: pl.BlockSpec((tm, n), lambda i: (i, 0))
    full = lambda a: pl.BlockSpec(a.shape, lambda i: (0,) * a.ndim)
    grp = lambda d: pl.BlockSpec((1, d, tm // d, ATT_MERGED), lambda i: (i // per_b, 0, i % per_b, 0))
    outs = [jax.ShapeDtypeStruct((t, D_MODEL), F32),
            jax.ShapeDtypeStruct((N_SG, t // CHUNK, FLAT), BF16)]
    out_specs = [row(D_MODEL), pl.BlockSpec((N_SG, tm // CHUNK, FLAT), lambda i: (0, i, 0))]
    for d in DILATIONS:
        outs += [jax.ShapeDtypeStruct((bsz, d, seq // d, ATT_MERGED), BF16)] * 3
        out_specs += [grp(d)] * 3
    outs += [jax.ShapeDtypeStruct((t, D_MODEL), BF16)] * 2
    out_specs += [row(D_MODEL)] * 2
    return pl.pallas_call(
        _ln_proj_kernel,
        grid=(t // tm,),
        in_specs=[row(D_MODEL), full(g), full(b), full(w), full(bi), row(LANES), row(LANES), row(LANES)],
        out_specs=out_specs,
        out_shape=outs,
        scratch_shapes=[pltpu.VMEM((GATE_OFF // LANES, tm, LANES), F32)],
        compiler_params=_cparams(1),
        name="ln_proj",
    )(x2, g, b, w, bi, cc, s1, s2)


def _ssm_p_kernel(u_ref, wp_ref, p_ref):
    p_ref[0] = jnp.dot(u_ref[0], wp_ref[0], preferred_element_type=F32)


def _ssm_p(uflat, wp):
    nc = uflat.shape[1]
    tn = ROW_TILE
    return pl.pallas_call(
        _ssm_p_kernel,
        grid=(N_SG, nc // tn),
        in_specs=[pl.BlockSpec((1, tn, FLAT), lambda s, i: (s, i, 0)),
                  pl.BlockSpec((1, FLAT, 2 * SG_STATE), lambda s, i: (s, 0, 0))],
        out_specs=pl.BlockSpec((1, tn, 2 * SG_STATE), lambda s, i: (s, i, 0)),
        out_shape=jax.ShapeDtypeStruct((N_SG, nc, 2 * SG_STATE), F32),
        compiler_params=_cparams(2),
        name="ssm_p",
    )(uflat, wp)


def _ssm_scan_kernel(p_ref, ar_ref, ai_ref, h_ref, *, steps):
    ar = ar_ref[0]
    ai = ai_ref[0]

    def body(c, carry):
        hr, hi = carry
        h_ref[0, pl.ds(c, 1), 0:SG_STATE] = hr
        h_ref[0, pl.ds(c, 1), SG_STATE:2 * SG_STATE] = hi
        pr = p_ref[0, pl.ds(c, 1), 0:SG_STATE]
        pi = p_ref[0, pl.ds(c, 1), SG_STATE:2 * SG_STATE]
        return ar * hr - ai * hi + pr, ar * hi + ai * hr + pi

    z = jnp.zeros((1, SG_STATE), F32)
    lax.fori_loop(0, steps, body, (z, z))


def _ssm_scan(p, ar, ai, bsz):
    nc = p.shape[1]
    steps = nc // bsz
    return pl.pallas_call(
        functools.partial(_ssm_scan_kernel, steps=steps),
        grid=(N_SG, bsz),
        in_specs=[pl.BlockSpec((1, steps, 2 * SG_STATE), lambda s, b: (s, b, 0)),
                  pl.BlockSpec((1, 1, SG_STATE), lambda s, b: (s, 0, 0)),
                  pl.BlockSpec((1, 1, SG_STATE), lambda s, b: (s, 0, 0))],
        out_specs=pl.BlockSpec((1, steps, 2 * SG_STATE), lambda s, b: (s, b, 0)),
        out_shape=jax.ShapeDtypeStruct(p.shape, F32),
        compiler_params=_cparams(2),
        name="ssm_scan",
    )(p, ar, ai)


def _ssm_y_kernel(u_ref, t_ref, h_ref, wc_ref, y_ref, scr_ref):
    tn = u_ref.shape[1]
    hb = h_ref[0].astype(BF16)
    for jo in range(N_TT):
        cols = slice(jo * MXU_TILE, (jo + 1) * MXU_TILE)
        acc = jnp.dot(hb, wc_ref[0, :, cols], preferred_element_type=F32)
        for ji in range(jo + 1):
            acc += jnp.dot(u_ref[0, :, ji * MXU_TILE:(ji + 1) * MXU_TILE], t_ref[0, jo - ji],
                           preferred_element_type=F32)
        yg = jax.nn.gelu(acc, approximate=True)
        for e in range(MXU_TILE // LANES):
            j = jo * (MXU_TILE // LANES) + e
            scr_ref[pl.ds(j, tn, stride=CHUNK), :] = yg[:, e * LANES:(e + 1) * LANES]
    y_ref[...] = scr_ref[...].astype(BF16)


def _ssm_y(uflat, tt, hprev, wc):
    nc = uflat.shape[1]
    tn = ROW_TILE
    return pl.pallas_call(
        _ssm_y_kernel,
        grid=(N_SG, nc // tn),
        in_specs=[pl.BlockSpec((1, tn, FLAT), lambda s, i: (s, i, 0)),
                  pl.BlockSpec((1, N_TT, MXU_TILE, MXU_TILE), lambda s, i: (s, 0, 0, 0)),
                  pl.BlockSpec((1, tn, 2 * SG_STATE), lambda s, i: (s, i, 0)),
                  pl.BlockSpec((1, 2 * SG_STATE, FLAT), lambda s, i: (s, 0, 0))],
        out_specs=pl.BlockSpec((tn * CHUNK, LANES), lambda s, i: (i, s)),
        out_shape=jax.ShapeDtypeStruct((nc * CHUNK, SSM_WIDTH), BF16),
        scratch_shapes=[pltpu.VMEM((tn * CHUNK, LANES), F32)],
        compiler_params=_cparams(2),
        name="ssm_y",
    )(uflat, tt, hprev, wc)


def _ssm_matrices(log_dt, a_re, a_im, b_re, b_im, c_re, c_im, d):
    g, n, c = SSM_GROUPS, SSM_STATE, SSM_GROUP
    dt = jnp.exp(log_dt.astype(F32))[:, None]
    a_re = a_re.astype(F32)
    a_im = a_im.astype(F32)
    ks = jnp.arange(CHUNK + 1, dtype=F32)
    mag = jnp.exp((a_re * dt)[..., None] * ks)
    ang = (a_im * dt)[..., None] * ks
    pw_re = mag * jnp.cos(ang)
    pw_im = mag * jnp.sin(ang)
    ab_re, ab_im = pw_re[..., 1], pw_im[..., 1]
    den = jnp.square(a_re) + jnp.square(a_im)
    nr = ab_re - 1.0
    f_re = (nr * a_re + ab_im * a_im) / den
    f_im = (ab_im * a_re - nr * a_im) / den
    b_re = b_re.astype(F32)
    b_im = b_im.astype(F32)
    bb_re = f_re[..., None] * b_re - f_im[..., None] * b_im
    bb_im = f_re[..., None] * b_im + f_im[..., None] * b_re
    c_re = c_re.astype(F32)
    c_im = c_im.astype(F32)
    ca_re = c_re[..., None] * pw_re[:, None] - c_im[..., None] * pw_im[:, None]
    ca_im = c_re[..., None] * pw_im[:, None] + c_im[..., None] * pw_re[:, None]

    kk = (jnp.einsum('gonk,gni->gkio', ca_re[..., :CHUNK], bb_re)
          - jnp.einsum('gonk,gni->gkio', ca_im[..., :CHUNK], bb_im))
    kk = kk.at[:, 0].add(jnp.eye(c, dtype=F32)[None] * d.astype(F32).reshape(g, c)[:, :, None])
    ktab = jnp.concatenate([jnp.zeros((g, 1, c, c), F32), kk], axis=1)
    dd = np.arange(N_TT)[:, None, None]
    s2 = np.arange(2)[None, :, None]
    t2 = np.arange(2)[None, None, :]
    lag = 2 * dd + t2 - s2 + 1
    x = ktab[:, lag].reshape(N_SG, SG_GROUPS, N_TT, 2, 2, c, c)
    kc = jnp.transpose(x, (0, 2, 3, 1, 5, 4, 6)).reshape(N_SG, N_TT, MXU_TILE, 2 * c)

    rp_re = pw_re[..., CHUNK - 1::-1][..., :CHUNK]
    rp_im = pw_im[..., CHUNK - 1::-1][..., :CHUNK]
    v_re = jnp.einsum('gns,gni->gsin', rp_re, bb_re) - jnp.einsum('gns,gni->gsin', rp_im, bb_im)
    v_im = jnp.einsum('gns,gni->gsin', rp_re, bb_im) + jnp.einsum('gns,gni->gsin', rp_im, bb_re)

    def rows_p(v):
        return jnp.transpose(v.reshape(N_SG, SG_GROUPS, CHUNK, c, n), (0, 2, 1, 3, 4)).reshape(N_SG, FLAT, n)

    vc = jnp.stack([rows_p(v_re), rows_p(v_im)], axis=1)

    def rows_c(m):
        return jnp.transpose(m[..., 1:], (0, 2, 3, 1)).reshape(N_SG, SG_STATE, CHUNK * c)

    cm = jnp.stack([rows_c(ca_re), rows_c(-ca_im)], axis=1)

    ar = pw_re[..., CHUNK].reshape(N_SG, 1, SG_STATE)
    ai = pw_im[..., CHUNK].reshape(N_SG, 1, SG_STATE)
    return kc.astype(BF16), vc.astype(BF16), cm.astype(BF16), ar, ai


def _ssm_build_kernel(kc_ref, vc_ref, cm_ref, tt_ref, wp_ref, wc_ref):
    def iota(shape, dim):
        return lax.broadcasted_iota(jnp.int32, shape, dim)

    def spread(nrows):
        r, cidx = iota((nrows, nrows * SG_GROUPS), 0), iota((nrows, nrows * SG_GROUPS), 1)
        return jnp.where(((r >> 4) == (cidx >> 7)) & ((r & 15) == (cidx & 15)), 1.0, 0.0).astype(BF16)

    grp16 = lambda idx: (idx >> 4) & (SG_GROUPS - 1)
    grp64 = lambda idx: (idx >> 6) & (SG_GROUPS - 1)

    e_t = spread(2 * SSM_GROUP)
    keep = grp16(iota((MXU_TILE, MXU_TILE), 0)) == grp16(iota((MXU_TILE, MXU_TILE), 1))
    for dlt in range(N_TT):
        full = jnp.dot(kc_ref[0, dlt], e_t, preferred_element_type=F32)
        tt_ref[0, dlt] = jnp.where(keep, full, 0.0).astype(BF16)

    r, cidx = iota((SSM_STATE, SG_STATE), 0), iota((SSM_STATE, SG_STATE), 1)
    e_p = jnp.where(r == (cidx & (SSM_STATE - 1)), 1.0, 0.0).astype(BF16)
    keep = grp16(iota((FLAT, SG_STATE), 0)) == grp64(iota((FLAT, SG_STATE), 1))
    for part in range(2):
        full = jnp.dot(vc_ref[0, part], e_p, preferred_element_type=F32)
        wp_ref[0, :, part * SG_STATE:(part + 1) * SG_STATE] = jnp.where(keep, full, 0.0).astype(BF16)

    e_c = spread(CHUNK * SSM_GROUP)
    keep = grp64(iota((SG_STATE, FLAT), 0)) == grp16(iota((SG_STATE, FLAT), 1))
    for part in range(2):
        full = jnp.dot(cm_ref[0, part], e_c, preferred_element_type=F32)
        wc_ref[0, part * SG_STATE:(part + 1) * SG_STATE, :] = jnp.where(keep, full, 0.0).astype(BF16)


def _ssm_build(kc, vc, cm):
    blk = lambda a: pl.BlockSpec((1,) + a.shape[1:], lambda s: (s,) + (0,) * (a.ndim - 1))
    outs = [jax.ShapeDtypeStruct((N_SG, N_TT, MXU_TILE, MXU_TILE), BF16),
            jax.ShapeDtypeStruct((N_SG, FLAT, 2 * SG_STATE), BF16),
            jax.ShapeDtypeStruct((N_SG, 2 * SG_STATE, FLAT), BF16)]
    return pl.pallas_call(
        _ssm_build_kernel,
        grid=(N_SG,),
        in_specs=[blk(kc), blk(vc), blk(cm)],
        out_specs=[blk(o) for o in outs],
        out_shape=outs,
        compiler_params=_cparams(1),
        name="ssm_build",
    )(kc, vc, cm)


ATT_UNITS_IN_FLIGHT = 4


def _attn_kernel(q_ref, kp_ref, kc_ref, vp_ref, vc_ref, o_ref, lse_ref, ks_ref, vs_ref):
    qb = q_ref.shape[1]
    blk = ATT_SPAN
    first = pl.program_id(1) == 0

    lane = lax.broadcasted_iota(jnp.int32, (1, ATT_MERGED), 1)
    head_sel = (lane % LANES < ATT_HEAD_DIM, lane % LANES >= ATT_HEAD_DIM)
    for src_p, src_c, dst in ((kp_ref, kc_ref, ks_ref), (vp_ref, vc_ref, vs_ref)):
        for e in range(2):
            xp, xc = src_p[0], src_c[0]
            dst[e, 0:blk] = jnp.where(head_sel[e], xp, jnp.zeros_like(xp))
            dst[e, blk:blk + qb] = jnp.where(head_sel[e], xc, jnp.zeros_like(xc))

    qi = lax.broadcasted_iota(jnp.int32, (blk, 2 * blk), 0)
    kk = lax.broadcasted_iota(jnp.int32, (blk, 2 * blk), 1)
    band = jnp.logical_and(kk >= qi, kk <= qi + blk)
    band_first = jnp.logical_and(band, jnp.logical_or(kk >= blk, jnp.logical_not(first)))
    low = lax.broadcasted_iota(jnp.int32, (blk, LANES), 1) < ATT_HEAD_DIM
    dn = (((1,), (1,)), ((), ()))

    units = [(j, pair) for j in range(qb // blk) for pair in range(ATT_MERGED // LANES)]
    for g0 in range(0, len(units), ATT_UNITS_IN_FLIGHT):
        group = units[g0:g0 + ATT_UNITS_IN_FLIGHT]
        scores = []
        for j, pair in group:
            cols = slice(pair * LANES, (pair + 1) * LANES)
            keys = slice(j * blk, (j + 2) * blk)
            kcat = jnp.concatenate([ks_ref[0, keys, cols], ks_ref[1, keys, cols]], axis=0)
            s = lax.dot_general(q_ref[0, j * blk:(j + 1) * blk, cols], kcat, dn,
                                preferred_element_type=F32)
            mask = band_first if j == 0 else band
            scores.append([jnp.where(mask, s[:, e * 2 * blk:(e + 1) * 2 * blk], NEG_INF) for e in range(2)])
        maxes = [[jnp.max(jnp.maximum(se[:, :blk], se[:, blk:]), axis=-1, keepdims=True) for se in su]
                 for su in scores]
        probs = [[jnp.exp2(se - me) for se, me in zip(su, mu)] for su, mu in zip(scores, maxes)]
        dens = [[jnp.sum(pe[:, :blk] + pe[:, blk:], axis=-1, keepdims=True) for pe in pu] for pu in probs]
        for (j, pair), pu, mu, du in zip(group, probs, maxes, dens):
            cols = slice(pair * LANES, (pair + 1) * LANES)
            rows = slice(j * blk, (j + 1) * blk)
            keys = slice(j * blk, (j + 2) * blk)
            vcat = jnp.concatenate([vs_ref[0, keys, cols], vs_ref[1, keys, cols]], axis=0)
            pcat = jnp.concatenate([pu[0].astype(BF16), pu[1].astype(BF16)], axis=1)
            num = jnp.dot(pcat, vcat, preferred_element_type=F32)
            o_ref[0, rows, cols] = (num * jnp.where(low, 1.0 / du[0], 1.0 / du[1])).astype(BF16)
            lse_ref[0, rows, cols] = jnp.where(low, mu[0] * LN2 + jnp.log(du[0]), mu[1] * LN2 + jnp.log(du[1]))


def _banded_attention(q, k, v):
    nseq, length, w = q.shape
    blk = ATT_SPAN
    qb = min(length, 4 * blk)
    per = qb // blk
    cur = pl.BlockSpec((1, qb, w), lambda s, n: (s, n, 0))
    prev = pl.BlockSpec((1, blk, w), lambda s, n: (s, jnp.maximum(n * per - 1, 0), 0))
    return pl.pallas_call(
        _attn_kernel,
        grid=(nseq, length // qb),
        in_specs=[cur, prev, cur, prev, cur],
        out_specs=[cur, cur],
        out_shape=[jax.ShapeDtypeStruct(q.shape, BF16), jax.ShapeDtypeStruct(q.shape, F32)],
        scratch_shapes=[pltpu.VMEM((2, blk + qb, w), BF16)] * 2,
        compiler_params=_cparams(2),
        name="attn",
    )(q, k, k, v, v)


def _mix_kernel(y_ref, o0_ref, o1_ref, o2_ref, l0_ref, l1_ref, l2_ref, gs_ref, ga_ref, h_ref,
                wglu_ref, bglu_ref, wup_ref, wmix_ref, bmix_ref, g_ref, b_ref, out_ref, scr_ref):
    tm = h_ref.shape[0]
    z = jnp.dot(y_ref[...], wglu_ref[...], preferred_element_type=F32) + bglu_ref[...]
    b_ssm = z[:, :D_MODEL] * jax.nn.sigmoid(z[:, D_MODEL:])

    def natural(ref, dil):
        if dil == 1:
            return ref[0, 0].astype(F32)
        for r in range(dil):
            for e in range(ATT_MERGED // LANES):
                scr_ref[e, pl.ds(r, tm // dil, stride=dil), :] = ref[0, r, :, e * LANES:(e + 1) * LANES].astype(F32)
        return jnp.concatenate([scr_ref[e] for e in range(ATT_MERGED // LANES)], axis=-1)

    o_refs = (o0_ref, o1_ref, o2_ref)
    l_refs = (l0_ref, l1_ref, l2_ref)
    ls = [natural(l_refs[gi], dil) for gi, dil in enumerate(DILATIONS)]
    m = jnp.maximum(jnp.maximum(ls[0], ls[1]), ls[2])
    es = [jnp.exp(l - m) for l in ls]
    att = es[0] * natural(o_refs[0], DILATIONS[0])
    for gi in (1, 2):
        att += es[gi] * natural(o_refs[gi], DILATIONS[gi])
    att = att / (es[0] + es[1] + es[2])
    b_att = jnp.dot(att.astype(BF16), wup_ref[...], preferred_element_type=F32)
    mixed = gs_ref[...].astype(F32) * b_ssm + ga_ref[...].astype(F32) * b_att
    r = jnp.dot(mixed.astype(BF16), wmix_ref[...], preferred_element_type=F32) + bmix_ref[...]
    out_ref[...] = _layer_norm(DEEPNORM_ALPHA * h_ref[...] + r, g_ref[...], b_ref[...])


def _mix(yg, outs, lses, gs, ga, h, wglu, bglu, wup, wmix, bmix, g, b, seq):
    t = h.shape[0]
    tm = ROW_TILE
    per_b = seq // tm
    row = lambda n: pl.BlockSpec((tm, n), lambda i: (i, 0))
    full = lambda a: pl.BlockSpec(a.shape, lambda i: (0,) * a.ndim)
    grp = lambda d: pl.BlockSpec((1, d, tm // d, ATT_MERGED), lambda i: (i // per_b, 0, i % per_b, 0))
    consts = (wglu, bglu, wup, wmix, bmix, g, b)
    return pl.pallas_call(
        _mix_kernel,
        grid=(t // tm,),
        in_specs=([row(SSM_WIDTH)] + [grp(d) for d in DILATIONS] * 2 + [row(D_MODEL)] * 3
                  + [full(a) for a in consts]),
        out_specs=row(D_MODEL),
        out_shape=jax.ShapeDtypeStruct((t, D_MODEL), F32),
        scratch_shapes=[pltpu.VMEM((ATT_MERGED // LANES, tm, LANES), F32)],
        compiler_params=_cparams(1),
        name="mix",
    )(yg, *outs, *lses, gs, ga, h, *consts)


def _mem_kv_kernel(m_ref, w_ref, kv_ref):
    kv_ref[0] = jnp.dot(m_ref[0].astype(BF16), w_ref[...], preferred_element_type=F32).astype(BF16)


def _mem_kv(mem, w):
    bsz, ml, _ = mem.shape
    return pl.pallas_call(
        _mem_kv_kernel,
        grid=(bsz,),
        in_specs=[pl.BlockSpec((1, ml, D_MODEL), lambda b: (b, 0, 0)),
                  pl.BlockSpec(w.shape, lambda b: (0, 0))],
        out_specs=pl.BlockSpec((1, ml, 2 * D_MODEL), lambda b: (b, 0, 0)),
        out_shape=jax.ShapeDtypeStruct((bsz, ml, 2 * D_MODEL), BF16),
        compiler_params=_cparams(1),
        name="mem_kv",
    )(mem, w)


def _xattn_kernel(h_ref, kv_ref, wq_ref, wo_ref, g_ref, b_ref, out_ref):
    h = h_ref[...]
    q = jnp.dot(h.astype(BF16), wq_ref[...], preferred_element_type=F32) * XATT_SCALE
    dn = (((1,), (1,)), ((), ()))
    outs = []
    for hh in range(XATT_HEADS):
        sl = slice(hh * XATT_HEAD_DIM, (hh + 1) * XATT_HEAD_DIM)
        sv = slice(D_MODEL + hh * XATT_HEAD_DIM, D_MODEL + (hh + 1) * XATT_HEAD_DIM)
        s = lax.dot_general(q[:, sl].astype(BF16), kv_ref[0, :, sl], dn, preferred_element_type=F32)
        e = jnp.exp(s - jnp.max(s, axis=-1, keepdims=True))
        p = e / jnp.sum(e, axis=-1, keepdims=True)
        outs.append(jnp.dot(p.astype(BF16), kv_ref[0, :, sv], preferred_element_type=F32))
    o = jnp.concatenate(outs, axis=-1)
    xo = jnp.dot(o.astype(BF16), wo_ref[...], preferred_element_type=F32)
    out_ref[...] = _layer_norm(DEEPNORM_ALPHA * h + xo, g_ref[...], b_ref[...])


def _xattn(h, kv, wq, wo, g, b, seq):
    t = h.shape[0]
    tm = ROW_TILE
    per_b = seq // tm
    row = pl.BlockSpec((tm, D_MODEL), lambda i: (i, 0))
    full = lambda a: pl.BlockSpec(a.shape, lambda i: (0,) * a.ndim)
    return pl.pallas_call(
        _xattn_kernel,
        grid=(t // tm,),
        in_specs=[row, pl.BlockSpec((1,) + kv.shape[1:], lambda i: (i // per_b, 0, 0)),
                  full(wq), full(wo), full(g), full(b)],
        out_specs=row,
        out_shape=jax.ShapeDtypeStruct((t, D_MODEL), F32),
        compiler_params=_cparams(1),
        name="xattn",
    )(h, kv, wq, wo, g, b)


def _ffn_kernel(h_ref, w1_ref, b1_ref, w2_ref, b2_ref, g_ref, b_ref, out_ref):
    h = h_ref[...]
    a = jnp.dot(h.astype(BF16), w1_ref[...], preferred_element_type=F32) + b1_ref[...]
    a = jnp.square(jnp.maximum(a, 0.0))
    ff = jnp.dot(a.astype(BF16), w2_ref[...], preferred_element_type=F32) + b2_ref[...]
    out_ref[...] = _layer_norm(DEEPNORM_ALPHA * h + ff, g_ref[...], b_ref[...])


def _ffn(h, w1, b1, w2, b2, g, b):
    t = h.shape[0]
    tm = ROW_TILE
    row = pl.BlockSpec((tm, D_MODEL), lambda i: (i, 0))
    full = lambda a: pl.BlockSpec(a.shape, lambda i: (0,) * a.ndim)
    consts = (w1, b1, w2, b2, g, b)
    return pl.pallas_call(
        _ffn_kernel,
        grid=(t // tm,),
        in_specs=[row] + [full(a) for a in consts],
        out_specs=row,
        out_shape=jax.ShapeDtypeStruct((t, D_MODEL), F32),
        compiler_params=_cparams(1),
        name="ffn",
    )(h, *consts)


def _rope_tables(positions):
    inv_freq = ROPE_THETA ** (-jnp.arange(0, ROT_DIM, 2, dtype=F32) / ROT_DIM)
    ang = positions.astype(F32).reshape(-1, 1) * inv_freq
    cos, sin = jnp.cos(ang), jnp.sin(ang)
    t = ang.shape[0]
    half = ROT_DIM // 2
    pad = jnp.zeros((t, ATT_HEAD_DIM - ROT_DIM), F32)
    zh = jnp.zeros((t, half), F32)
    cc = jnp.concatenate([cos, cos, pad + 1.0], axis=-1)
    s1 = jnp.concatenate([-sin, zh, pad], axis=-1)
    s2 = jnp.concatenate([zh, sin, pad], axis=-1)
    return tuple(jnp.tile(a, (1, LANES // ATT_HEAD_DIM)) for a in (cc, s1, s2))


def kernel(x, mem, positions, ln_in_g, ln_in_b, w_in, b_in, ssm_log_dt, ssm_a_re, ssm_a_im, ssm_b_re, ssm_b_im, ssm_c_re, ssm_c_im, ssm_d, w_glu, b_glu, w_att_up, w_mix_out, b_mix_out, ln1_g, ln1_b, w_xq, w_xkv, w_xo, ln2_g, ln2_b, w_ff1, b_ff1, w_ff2, b_ff2, ln3_g, ln3_b):
    bsz, seq, _ = x.shape
    t = bsz * seq
    row2 = lambda a: a.reshape(1, -1).astype(F32)
    cc, s1, s2 = _rope_tables(positions)
    h = x.reshape(t, D_MODEL)
    for l in range(DEPTH):
        h, uflat, *qkv, gs, ga = _ln_proj(
            h, row2(ln_in_g), row2(ln_in_b), w_in[l].astype(BF16), row2(b_in[l]), cc, s1, s2, bsz, seq)

        kc, vc, cm, ar, ai = _ssm_matrices(ssm_log_dt[l], ssm_a_re[l], ssm_a_im[l], ssm_b_re[l],
                                           ssm_b_im[l], ssm_c_re[l], ssm_c_im[l], ssm_d[l])
        tt, wp, wc = _ssm_build(kc, vc, cm)
        p = _ssm_p(uflat, wp)
        hprev = _ssm_scan(p, ar, ai, bsz)
        yg = _ssm_y(uflat, tt, hprev, wc)

        outs, lses = [], []
        for gi, dil in enumerate(DILATIONS):
            qg, kg, vg = (a.reshape(bsz * dil, seq // dil, ATT_MERGED) for a in qkv[3 * gi:3 * gi + 3])
            o_g, lse_g = _banded_attention(qg, kg, vg)
            outs.append(o_g.reshape(bsz, dil, seq // dil, ATT_MERGED))
            lses.append(lse_g.reshape(bsz, dil, seq // dil, ATT_MERGED))

        h = _mix(yg, outs, lses, gs, ga, h, w_glu[l].astype(BF16), row2(b_glu[l]),
                 w_att_up[l].astype(BF16), w_mix_out[l].astype(BF16), row2(b_mix_out[l]),
                 row2(ln1_g[l]), row2(ln1_b[l]), seq)

        kv = _mem_kv(mem, w_xkv[l].astype(BF16))
        h = _xattn(h, kv, w_xq[l].astype(BF16), w_xo[l].astype(BF16), row2(ln2_g[l]), row2(ln2_b[l]), seq)
        h = _ffn(h, w_ff1[l].astype(BF16), row2(b_ff1[l]), w_ff2[l].astype(BF16), row2(b_ff2[l]),
                 row2(ln3_g[l]), row2(ln3_b[l]))
    return h.reshape(bsz, seq, D_MODEL)
```

```python
import functools
import math

import jax
import jax.numpy as jnp
import numpy as np
from jax import lax
from jax.experimental import pallas as pl
from jax.experimental.pallas import tpu as pltpu

F32 = jnp.float32
BF16 = jnp.bfloat16

D_MODEL = 1024
SSM_GROUP = 16
SSM_WIDTH = 768
SSM_GROUPS = SSM_WIDTH // SSM_GROUP
SSM_STATE = 64
ATT_HEAD_DIM = 64
ATT_HEADS_PER_GROUP = 4
DILATIONS = (1, 4, 16)
ATT_SPAN = 128
ATT_WIDTH = 768
ATT_MERGED = ATT_HEADS_PER_GROUP * ATT_HEAD_DIM
ATT_SCALE = ATT_HEAD_DIM ** -0.5
ROT_DIM = ATT_HEAD_DIM // 4
ROPE_THETA = 500000.0
XATT_HEADS = 4
XATT_HEAD_DIM = D_MODEL // XATT_HEADS
XATT_SCALE = XATT_HEAD_DIM ** -0.5
D_FF = 4 * D_MODEL
DEPTH = 1
DEEPNORM_ALPHA = (2 * DEPTH) ** 0.25
LN_EPS = 1e-5
NEG_INF = -1e30
LOG2E = math.log2(math.e)
LN2 = math.log(2.0)

LANES = 128
CHUNK = 16
SG_GROUPS = LANES // SSM_GROUP
N_SG = SSM_GROUPS // SG_GROUPS
FLAT = CHUNK * LANES
SG_STATE = SG_GROUPS * SSM_STATE
MXU_TILE = 256
N_TT = FLAT // MXU_TILE

ROW_TILE = 256
SUB_ROWS = 256
TOKEN_TILE = 512
MIX_CHUNK = 256
FFN_CHUNK = 1024
QKV_OFF = SSM_WIDTH
GATE_OFF = SSM_WIDTH + 3 * ATT_WIDTH
VMEM_LIMIT = 56 * 1024 * 1024


def _cparams(n_axes):
    return pltpu.CompilerParams(dimension_semantics=("parallel",) * n_axes,
                                vmem_limit_bytes=VMEM_LIMIT)


def _layer_norm(x, g, b):
    mu = jnp.mean(x, axis=-1, keepdims=True)
    xc = x - mu
    var = jnp.mean(xc * xc, axis=-1, keepdims=True)
    return xc * lax.rsqrt(var + LN_EPS) * g + b


def _stagger(tiles, lag):
    tiles = list(tiles)
    live = [True] * len(tiles)
    rnd = 0
    while any(live):
        for i in reversed(range(len(tiles))):
            if live[i] and rnd >= i * lag:
                try:
                    next(tiles[i])
                except StopIteration:
                    live[i] = False
        rnd += 1


def _sub_tiles(tm):
    return [(i, slice(i * SUB_ROWS, (i + 1) * SUB_ROWS)) for i in range(tm // SUB_ROWS)]


def _ln_proj_kernel(x_ref, g_ref, b_ref, w_ref, bi_ref, c_ref, s1_ref, s2_ref,
                    h_ref, u_ref, q0_ref, k0_ref, v0_ref, q1_ref, k1_ref, v1_ref, q2_ref, k2_ref, v2_ref,
                    gs_ref, ga_ref, scr_ref):
    half = ROT_DIM // 2
    qkv_refs = ((q0_ref, k0_ref, v0_ref), (q1_ref, k1_ref, v1_ref), (q2_ref, k2_ref, v2_ref))
    sections = [(0, SSM_WIDTH), (QKV_OFF, ATT_WIDTH), (QKV_OFF + ATT_WIDTH, ATT_WIDTH),
                (QKV_OFF + 2 * ATT_WIDTH, ATT_WIDTH), (GATE_OFF, D_MODEL), (GATE_OFF + D_MODEL, D_MODEL)]

    def tile(sub, rows):
        h = _layer_norm(x_ref[rows, :], g_ref[...], b_ref[...])
        h_ref[rows, :] = h
        hb = h.astype(BF16)
        yield

        def stage(off, val):
            for i in range(val.shape[1] // LANES):
                scr_ref[sub, off // LANES + i] = val[:, i * LANES:(i + 1) * LANES]

        def rope(val, scale):
            cc, s1, s2 = c_ref[rows, :], s1_ref[rows, :], s2_ref[rows, :]
            out = []
            for i in range(ATT_WIDTH // LANES):
                t = val[:, i * LANES:(i + 1) * LANES]
                out.append((t * cc + pltpu.roll(t, LANES - half, 1) * s1 + pltpu.roll(t, half, 1) * s2) * scale)
            return jnp.concatenate(out, axis=-1)

        def emit_groups(a):
            for gi, dil in enumerate(DILATIONS):
                ref = qkv_refs[gi][a]
                cb0 = (QKV_OFF + a * ATT_WIDTH + gi * ATT_MERGED) // LANES
                n = SUB_ROWS // dil
                for r in range(dil):
                    for e in range(ATT_MERGED // LANES):
                        if dil == 1:
                            blk = scr_ref[sub, cb0 + e]
                        else:
                            blk = scr_ref[sub, cb0 + e, pl.ds(r, n, stride=dil), :]
                        ref[0, r, sub * n:(sub + 1) * n, e * LANES:(e + 1) * LANES] = blk.astype(BF16)

        for c, (off, width) in enumerate(sections):
            val = (jnp.dot(hb, w_ref[:, off:off + width], preferred_element_type=F32)
                   + bi_ref[:, off:off + width])
            if c == 0:
                stage(off, val)
                n = SUB_ROWS // CHUNK
                for j in range(CHUNK):
                    for sg in range(N_SG):
                        blk = scr_ref[sub, sg, pl.ds(j, n, stride=CHUNK), :]
                        u_ref[sg, sub * n:(sub + 1) * n, j * LANES:(j + 1) * LANES] = blk.astype(BF16)
            elif c in (1, 2):
                stage(off, rope(val, ATT_SCALE * LOG2E if c == 1 else 1.0))
                emit_groups(c - 1)
            elif c == 3:
                stage(off, val)
                emit_groups(2)
            else:
                (gs_ref if c == 4 else ga_ref)[rows, :] = jax.nn.sigmoid(val).astype(BF16)
            yield

    _stagger([tile(i, rows) for i, rows in _sub_tiles(x_ref.shape[0])], lag=len(sections) - 1)


def _ln_proj(x2, g, b, w, bi, cc, s1, s2, bsz, seq):
    t = x2.shape[0]
    tm = TOKEN_TILE
    per_b = seq // tm
    row = lambda n: pl.BlockSpec((tm, n), lambda i: (i, 0))
    full = lambda a: pl.BlockSpec(a.shape, lambda i: (0,) * a.ndim)
    grp = lambda d: pl.BlockSpec((1, d, tm // d, ATT_MERGED), lambda i: (i // per_b, 0, i % per_b, 0))
    outs = [jax.ShapeDtypeStruct((t, D_MODEL), F32),
            jax.ShapeDtypeStruct((N_SG, t // CHUNK, FLAT), BF16)]
    out_specs = [row(D_MODEL), pl.BlockSpec((N_SG, tm // CHUNK, FLAT), lambda i: (0, i, 0))]
    for d in DILATIONS:
        outs += [jax.ShapeDtypeStruct((bsz, d, seq // d, ATT_MERGED), BF16)] * 3
        out_specs += [grp(d)] * 3
    outs += [jax.ShapeDtypeStruct((t, D_MODEL), BF16)] * 2
    out_specs += [row(D_MODEL)] * 2
    return pl.pallas_call(
        _ln_proj_kernel,
        grid=(t // tm,),
        in_specs=[row(D_MODEL), full(g), full(b), full(w), full(bi), row(LANES), row(LANES), row(LANES)],
        out_specs=out_specs,
        out_shape=outs,
        scratch_shapes=[pltpu.VMEM((tm // SUB_ROWS, GATE_OFF // LANES, SUB_ROWS, LANES), F32)],
        compiler_params=_cparams(1),
        name="ln_proj",
    )(x2, g, b, w, bi, cc, s1, s2)


def _ssm_p_kernel(u_ref, wp_ref, p_ref):
    p_ref[0] = jnp.dot(u_ref[0], wp_ref[0], preferred_element_type=F32)


def _ssm_p(uflat, wp):
    nc = uflat.shape[1]
    tn = ROW_TILE
    return pl.pallas_call(
        _ssm_p_kernel,
        grid=(N_SG, nc // tn),
        in_specs=[pl.BlockSpec((1, tn, FLAT), lambda s, i: (s, i, 0)),
                  pl.BlockSpec((1, FLAT, 2 * SG_STATE), lambda s, i: (s, 0, 0))],
        out_specs=pl.BlockSpec((1, tn, 2 * SG_STATE), lambda s, i: (s, i, 0)),
        out_shape=jax.ShapeDtypeStruct((N_SG, nc, 2 * SG_STATE), F32),
        compiler_params=_cparams(2),
        name="ssm_p",
    )(uflat, wp)


def _ssm_scan_kernel(p_ref, ar_ref, ai_ref, h_ref, *, steps):
    ar = ar_ref[0]
    ai = ai_ref[0]

    def body(c, carry):
        hr, hi = carry
        h_ref[0, pl.ds(c, 1), 0:SG_STATE] = hr
        h_ref[0, pl.ds(c, 1), SG_STATE:2 * SG_STATE] = hi
        pr = p_ref[0, pl.ds(c, 1), 0:SG_STATE]
        pi = p_ref[0, pl.ds(c, 1), SG_STATE:2 * SG_STATE]
        return ar * hr - ai * hi + pr, ar * hi + ai * hr + pi

    z = jnp.zeros((1, SG_STATE), F32)
    lax.fori_loop(0, steps, body, (z, z))


def _ssm_scan(p, ar, ai, bsz):
    nc = p.shape[1]
    steps = nc // bsz
    return pl.pallas_call(
        functools.partial(_ssm_scan_kernel, steps=steps),
        grid=(N_SG, bsz),
        in_specs=[pl.BlockSpec((1, steps, 2 * SG_STATE), lambda s, b: (s, b, 0)),
                  pl.BlockSpec((1, 1, SG_STATE), lambda s, b: (s, 0, 0)),
                  pl.BlockSpec((1, 1, SG_STATE), lambda s, b: (s, 0, 0))],
        out_specs=pl.BlockSpec((1, steps, 2 * SG_STATE), lambda s, b: (s, b, 0)),
        out_shape=jax.ShapeDtypeStruct(p.shape, F32),
        compiler_params=_cparams(2),
        name="ssm_scan",
    )(p, ar, ai)


def _ssm_y_kernel(u_ref, t_ref, h_ref, wc_ref, y_ref, scr_ref):
    tn = u_ref.shape[1]
    hb = h_ref[0].astype(BF16)
    for jo in range(N_TT):
        cols = slice(jo * MXU_TILE, (jo + 1) * MXU_TILE)
        acc = jnp.dot(hb, wc_ref[0, :, cols], preferred_element_type=F32)
        for ji in range(jo + 1):
            acc += jnp.dot(u_ref[0, :, ji * MXU_TILE:(ji + 1) * MXU_TILE], t_ref[0, jo - ji],
                           preferred_element_type=F32)
        yg = jax.nn.gelu(acc, approximate=True)
        for e in range(MXU_TILE // LANES):
            j = jo * (MXU_TILE // LANES) + e
            scr_ref[pl.ds(j, tn, stride=CHUNK), :] = yg[:, e * LANES:(e + 1) * LANES]
    y_ref[...] = scr_ref[...].astype(BF16)


def _ssm_y(uflat, tt, hprev, wc):
    nc = uflat.shape[1]
    tn = ROW_TILE
    return pl.pallas_call(
        _ssm_y_kernel,
        grid=(N_SG, nc // tn),
        in_specs=[pl.BlockSpec((1, tn, FLAT), lambda s, i: (s, i, 0)),
                  pl.BlockSpec((1, N_TT, MXU_TILE, MXU_TILE), lambda s, i: (s, 0, 0, 0)),
                  pl.BlockSpec((1, tn, 2 * SG_STATE), lambda s, i: (s, i, 0)),
                  pl.BlockSpec((1, 2 * SG_STATE, FLAT), lambda s, i: (s, 0, 0))],
        out_specs=pl.BlockSpec((tn * CHUNK, LANES), lambda s, i: (i, s)),
        out_shape=jax.ShapeDtypeStruct((nc * CHUNK, SSM_WIDTH), BF16),
        scratch_shapes=[pltpu.VMEM((tn * CHUNK, LANES), F32)],
        compiler_params=_cparams(2),
        name="ssm_y",
    )(uflat, tt, hprev, wc)


def _ssm_matrices(log_dt, a_re, a_im, b_re, b_im, c_re, c_im, d):
    g, n, c = SSM_GROUPS, SSM_STATE, SSM_GROUP
    dt = jnp.exp(log_dt.astype(F32))[:, None]
    a_re = a_re.astype(F32)
    a_im = a_im.astype(F32)
    ks = jnp.arange(CHUNK + 1, dtype=F32)
    mag = jnp.exp((a_re * dt)[..., None] * ks)
    ang = (a_im * dt)[..., None] * ks
    pw_re = mag * jnp.cos(ang)
    pw_im = mag * jnp.sin(ang)
    ab_re, ab_im = pw_re[..., 1], pw_im[..., 1]
    den = jnp.square(a_re) + jnp.square(a_im)
    nr = ab_re - 1.0
    f_re = (nr * a_re + ab_im * a_im) / den
    f_im = (ab_im * a_re - nr * a_im) / den
    b_re = b_re.astype(F32)
    b_im = b_im.astype(F32)
    bb_re = f_re[..., None] * b_re - f_im[..., None] * b_im
    bb_im = f_re[..., None] * b_im + f_im[..., None] * b_re
    c_re = c_re.astype(F32)
    c_im = c_im.astype(F32)
    ca_re = c_re[..., None] * pw_re[:, None] - c_im[..., None] * pw_im[:, None]
    ca_im = c_re[..., None] * pw_im[:, None] + c_im[..., None] * pw_re[:, None]

    kk = (jnp.einsum('gonk,gni->gkio', ca_re[..., :CHUNK], bb_re)
          - jnp.einsum('gonk,gni->gkio', ca_im[..., :CHUNK], bb_im))
    kk = kk.at[:, 0].add(jnp.eye(c, dtype=F32)[None] * d.astype(F32).reshape(g, c)[:, :, None])
    ktab = jnp.concatenate([jnp.zeros((g, 1, c, c), F32), kk], axis=1)
    dd = np.arange(N_TT)[:, None, None]
    s2 = np.arange(2)[None, :, None]
    t2 = np.arange(2)[None, None, :]
    lag = 2 * dd + t2 - s2 + 1
    x = ktab[:, lag].reshape(N_SG, SG_GROUPS, N_TT, 2, 2, c, c)
    kc = jnp.transpose(x, (0, 2, 3, 1, 5, 4, 6)).reshape(N_SG, N_TT, MXU_TILE, 2 * c)

    rp_re = pw_re[..., CHUNK - 1::-1][..., :CHUNK]
    rp_im = pw_im[..., CHUNK - 1::-1][..., :CHUNK]
    v_re = jnp.einsum('gns,gni->gsin', rp_re, bb_re) - jnp.einsum('gns,gni->gsin', rp_im, bb_im)
    v_im = jnp.einsum('gns,gni->gsin', rp_re, bb_im) + jnp.einsum('gns,gni->gsin', rp_im, bb_re)

    def rows_p(v):
        return jnp.transpose(v.reshape(N_SG, SG_GROUPS, CHUNK, c, n), (0, 2, 1, 3, 4)).reshape(N_SG, FLAT, n)

    vc = jnp.stack([rows_p(v_re), rows_p(v_im)], axis=1)

    def rows_c(m):
        return jnp.transpose(m[..., 1:], (0, 2, 3, 1)).reshape(N_SG, SG_STATE, CHUNK * c)

    cm = jnp.stack([rows_c(ca_re), rows_c(-ca_im)], axis=1)

    ar = pw_re[..., CHUNK].reshape(N_SG, 1, SG_STATE)
    ai = pw_im[..., CHUNK].reshape(N_SG, 1, SG_STATE)
    return kc.astype(BF16), vc.astype(BF16), cm.astype(BF16), ar, ai


def _ssm_build_kernel(kc_ref, vc_ref, cm_ref, tt_ref, wp_ref, wc_ref):
    def iota(shape, dim):
        return lax.broadcasted_iota(jnp.int32, shape, dim)

    def spread(nrows):
        r, cidx = iota((nrows, nrows * SG_GROUPS), 0), iota((nrows, nrows * SG_GROUPS), 1)
        return jnp.where(((r >> 4) == (cidx >> 7)) & ((r & 15) == (cidx & 15)), 1.0, 0.0).astype(BF16)

    grp16 = lambda idx: (idx >> 4) & (SG_GROUPS - 1)
    grp64 = lambda idx: (idx >> 6) & (SG_GROUPS - 1)

    e_t = spread(2 * SSM_GROUP)
    keep = grp16(iota((MXU_TILE, MXU_TILE), 0)) == grp16(iota((MXU_TILE, MXU_TILE), 1))
    for dlt in range(N_TT):
        full = jnp.dot(kc_ref[0, dlt], e_t, preferred_element_type=F32)
        tt_ref[0, dlt] = jnp.where(keep, full, 0.0).astype(BF16)

    r, cidx = iota((SSM_STATE, SG_STATE), 0), iota((SSM_STATE, SG_STATE), 1)
    e_p = jnp.where(r == (cidx & (SSM_STATE - 1)), 1.0, 0.0).astype(BF16)
    keep = grp16(iota((FLAT, SG_STATE), 0)) == grp64(iota((FLAT, SG_STATE), 1))
    for part in range(2):
        full = jnp.dot(vc_ref[0, part], e_p, preferred_element_type=F32)
        wp_ref[0, :, part * SG_STATE:(part + 1) * SG_STATE] = jnp.where(keep, full, 0.0).astype(BF16)

    e_c = spread(CHUNK * SSM_GROUP)
    keep = grp64(iota((SG_STATE, FLAT), 0)) == grp16(iota((SG_STATE, FLAT), 1))
    for part in range(2):
        full = jnp.dot(cm_ref[0, part], e_c, preferred_element_type=F32)
        wc_ref[0, part * SG_STATE:(part + 1) * SG_STATE, :] = jnp.where(keep, full, 0.0).astype(BF16)


def _ssm_build(kc, vc, cm):
    blk = lambda a: pl.BlockSpec((1,) + a.shape[1:], lambda s: (s,) + (0,) * (a.ndim - 1))
    outs = [jax.ShapeDtypeStruct((N_SG, N_TT, MXU_TILE, MXU_TILE), BF16),
            jax.ShapeDtypeStruct((N_SG, FLAT, 2 * SG_STATE), BF16),
            jax.ShapeDtypeStruct((N_SG, 2 * SG_STATE, FLAT), BF16)]
    return pl.pallas_call(
        _ssm_build_kernel,
        grid=(N_SG,),
        in_specs=[blk(kc), blk(vc), blk(cm)],
        out_specs=[blk(o) for o in outs],
        out_shape=outs,
        compiler_params=_cparams(1),
        name="ssm_build",
    )(kc, vc, cm)


ATT_UNITS_IN_FLIGHT = 4


def _attn_kernel(q_ref, kp_ref, kc_ref, vp_ref, vc_ref, o_ref, lse_ref, ks_ref, vs_ref):
    qb = q_ref.shape[1]
    blk = ATT_SPAN
    first = pl.program_id(1) == 0

    lane = lax.broadcasted_iota(jnp.int32, (1, ATT_MERGED), 1)
    head_sel = (lane % LANES < ATT_HEAD_DIM, lane % LANES >= ATT_HEAD_DIM)
    for src_p, src_c, dst in ((kp_ref, kc_ref, ks_ref), (vp_ref, vc_ref, vs_ref)):
        for e in range(2):
            xp, xc = src_p[0], src_c[0]
            dst[e, 0:blk] = jnp.where(head_sel[e], xp, jnp.zeros_like(xp))
            dst[e, blk:blk + qb] = jnp.where(head_sel[e], xc, jnp.zeros_like(xc))

    qi = lax.broadcasted_iota(jnp.int32, (blk, 2 * blk), 0)
    kk = lax.broadcasted_iota(jnp.int32, (blk, 2 * blk), 1)
    band = jnp.logical_and(kk >= qi, kk <= qi + blk)
    band_first = jnp.logical_and(band, jnp.logical_or(kk >= blk, jnp.logical_not(first)))
    low = lax.broadcasted_iota(jnp.int32, (blk, LANES), 1) < ATT_HEAD_DIM
    dn = (((1,), (1,)), ((), ()))

    units = [(j, pair) for j in range(qb // blk) for pair in range(ATT_MERGED // LANES)]
    for g0 in range(0, len(units), ATT_UNITS_IN_FLIGHT):
        group = units[g0:g0 + ATT_UNITS_IN_FLIGHT]
        scores = []
        for j, pair in group:
            cols = slice(pair * LANES, (pair + 1) * LANES)
            keys = slice(j * blk, (j + 2) * blk)
            kcat = jnp.concatenate([ks_ref[0, keys, cols], ks_ref[1, keys, cols]], axis=0)
            s = lax.dot_general(q_ref[0, j * blk:(j + 1) * blk, cols], kcat, dn,
                                preferred_element_type=F32)
            mask = band_first if j == 0 else band
            scores.append([jnp.where(mask, s[:, e * 2 * blk:(e + 1) * 2 * blk], NEG_INF) for e in range(2)])
        maxes = [[jnp.max(jnp.maximum(se[:, :blk], se[:, blk:]), axis=-1, keepdims=True) for se in su]
                 for su in scores]
        probs = [[jnp.exp2(se - me) for se, me in zip(su, mu)] for su, mu in zip(scores, maxes)]
        dens = [[jnp.sum(pe[:, :blk] + pe[:, blk:], axis=-1, keepdims=True) for pe in pu] for pu in probs]
        for (j, pair), pu, mu, du in zip(group, probs, maxes, dens):
            cols = slice(pair * LANES, (pair + 1) * LANES)
            rows = slice(j * blk, (j + 1) * blk)
            keys = slice(j * blk, (j + 2) * blk)
            vcat = jnp.concatenate([vs_ref[0, keys, cols], vs_ref[1, keys, cols]], axis=0)
            pcat = jnp.concatenate([pu[0].astype(BF16), pu[1].astype(BF16)], axis=1)
            num = jnp.dot(pcat, vcat, preferred_element_type=F32)
            o_ref[0, rows, cols] = (num * jnp.where(low, 1.0 / du[0], 1.0 / du[1])).astype(BF16)
            lse_ref[0, rows, cols] = jnp.where(low, mu[0] * LN2 + jnp.log(du[0]), mu[1] * LN2 + jnp.log(du[1]))


def _banded_attention(q, k, v):
    nseq, length, w = q.shape
    blk = ATT_SPAN
    qb = min(length, 4 * blk)
    per = qb // blk
    cur = pl.BlockSpec((1, qb, w), lambda s, n: (s, n, 0))
    prev = pl.BlockSpec((1, blk, w), lambda s, n: (s, jnp.maximum(n * per - 1, 0), 0))
    return pl.pallas_call(
        _attn_kernel,
        grid=(nseq, length // qb),
        in_specs=[cur, prev, cur, prev, cur],
        out_specs=[cur, cur],
        out_shape=[jax.ShapeDtypeStruct(q.shape, BF16), jax.ShapeDtypeStruct(q.shape, F32)],
        scratch_shapes=[pltpu.VMEM((2, blk + qb, w), BF16)] * 2,
        compiler_params=_cparams(2),
        name="attn",
    )(q, k, k, v, v)


def _mix_kernel(y_ref, o0_ref, o1_ref, o2_ref, l0_ref, l1_ref, l2_ref, gs_ref, ga_ref, h_ref,
                wglu_ref, bglu_ref, wup_ref, wmix_ref, bmix_ref, g_ref, b_ref, out_ref, scr_ref, acc_ref):
    o_refs = (o0_ref, o1_ref, o2_ref)
    l_refs = (l0_ref, l1_ref, l2_ref)

    def tile(sub, rows):
        def natural(ref, dil):
            n = SUB_ROWS // dil
            if dil == 1:
                return ref[0, 0, rows, :].astype(F32)
            for r in range(dil):
                for e in range(ATT_MERGED // LANES):
                    scr_ref[sub, e, pl.ds(r, n, stride=dil), :] = (
                        ref[0, r, sub * n:(sub + 1) * n, e * LANES:(e + 1) * LANES].astype(F32))
            return jnp.concatenate([scr_ref[sub, e] for e in range(ATT_MERGED // LANES)], axis=-1)

        ls = [natural(l_refs[gi], dil) for gi, dil in enumerate(DILATIONS)]
        m = jnp.maximum(jnp.maximum(ls[0], ls[1]), ls[2])
        es = [jnp.exp(l - m) for l in ls]
        att = es[0] * natural(o_refs[0], DILATIONS[0])
        for gi in (1, 2):
            att += es[gi] * natural(o_refs[gi], DILATIONS[gi])
        att = (att / (es[0] + es[1] + es[2])).astype(BF16)
        y = y_ref[rows, :]
        yield

        for c in range(D_MODEL // MIX_CHUNK):
            cols = slice(c * MIX_CHUNK, (c + 1) * MIX_CHUNK)
            gate_cols = slice(D_MODEL + c * MIX_CHUNK, D_MODEL + (c + 1) * MIX_CHUNK)
            val = jnp.dot(y, wglu_ref[:, cols], preferred_element_type=F32) + bglu_ref[:, cols]
            gate = jnp.dot(y, wglu_ref[:, gate_cols], preferred_element_type=F32) + bglu_ref[:, gate_cols]
            b_att = jnp.dot(att, wup_ref[:, cols], preferred_element_type=F32)
            mixed = (gs_ref[rows, cols].astype(F32) * (val * jax.nn.sigmoid(gate))
                     + ga_ref[rows, cols].astype(F32) * b_att)
            part = jnp.dot(mixed.astype(BF16), wmix_ref[cols, :], preferred_element_type=F32)
            if c == 0:
                acc_ref[rows, :] = part
            else:
                acc_ref[rows, :] += part
            yield

        r = acc_ref[rows, :] + bmix_ref[...]
        out_ref[rows, :] = _layer_norm(DEEPNORM_ALPHA * h_ref[rows, :] + r, g_ref[...], b_ref[...])

    _stagger([tile(i, rows) for i, rows in _sub_tiles(h_ref.shape[0])], lag=D_MODEL // MIX_CHUNK)


def _mix(yg, outs, lses, gs, ga, h, wglu, bglu, wup, wmix, bmix, g, b, seq):
    t = h.shape[0]
    tm = TOKEN_TILE
    per_b = seq // tm
    row = lambda n: pl.BlockSpec((tm, n), lambda i: (i, 0))
    full = lambda a: pl.BlockSpec(a.shape, lambda i: (0,) * a.ndim)
    grp = lambda d: pl.BlockSpec((1, d, tm // d, ATT_MERGED), lambda i: (i // per_b, 0, i % per_b, 0))
    consts = (wglu, bglu, wup, wmix, bmix, g, b)
    return pl.pallas_call(
        _mix_kernel,
        grid=(t // tm,),
        in_specs=([row(SSM_WIDTH)] + [grp(d) for d in DILATIONS] * 2 + [row(D_MODEL)] * 3
                  + [full(a) for a in consts]),
        out_specs=row(D_MODEL),
        out_shape=jax.ShapeDtypeStruct((t, D_MODEL), F32),
        scratch_shapes=[pltpu.VMEM((tm // SUB_ROWS, ATT_MERGED // LANES, SUB_ROWS, LANES), F32),
                        pltpu.VMEM((tm, D_MODEL), F32)],
        compiler_params=_cparams(1),
        name="mix",
    )(yg, *outs, *lses, gs, ga, h, *consts)


def _mem_kv_kernel(m_ref, w_ref, kv_ref):
    kv_ref[0] = jnp.dot(m_ref[0].astype(BF16), w_ref[...], preferred_element_type=F32).astype(BF16)


def _mem_kv(mem, w):
    bsz, ml, _ = mem.shape
    return pl.pallas_call(
        _mem_kv_kernel,
        grid=(bsz,),
        in_specs=[pl.BlockSpec((1, ml, D_MODEL), lambda b: (b, 0, 0)),
                  pl.BlockSpec(w.shape, lambda b: (0, 0))],
        out_specs=pl.BlockSpec((1, ml, 2 * D_MODEL), lambda b: (b, 0, 0)),
        out_shape=jax.ShapeDtypeStruct((bsz, ml, 2 * D_MODEL), BF16),
        compiler_params=_cparams(1),
        name="mem_kv",
    )(mem, w)


def _xattn_kernel(h_ref, kv_ref, wq_ref, wo_ref, g_ref, b_ref, out_ref):
    dn = (((1,), (1,)), ((), ()))

    def tile(sub, rows):
        h = h_ref[rows, :]
        hb = h.astype(BF16)
        yield
        q = (jnp.dot(hb, wq_ref[...], preferred_element_type=F32) * XATT_SCALE).astype(BF16)
        yield
        outs = []
        for hh in range(XATT_HEADS):
            sl = slice(hh * XATT_HEAD_DIM, (hh + 1) * XATT_HEAD_DIM)
            sv = slice(D_MODEL + hh * XATT_HEAD_DIM, D_MODEL + (hh + 1) * XATT_HEAD_DIM)
            s = lax.dot_general(q[:, sl], kv_ref[0, :, sl], dn, preferred_element_type=F32)
            e = jnp.exp(s - jnp.max(s, axis=-1, keepdims=True))
            p = e / jnp.sum(e, axis=-1, keepdims=True)
            outs.append(jnp.dot(p.astype(BF16), kv_ref[0, :, sv], preferred_element_type=F32).astype(BF16))
            yield
        xo = jnp.dot(jnp.concatenate(outs, axis=-1), wo_ref[...], preferred_element_type=F32)
        yield
        out_ref[rows, :] = _layer_norm(DEEPNORM_ALPHA * h + xo, g_ref[...], b_ref[...])

    _stagger([tile(i, rows) for i, rows in _sub_tiles(h_ref.shape[0])], lag=XATT_HEADS + 2)


def _xattn(h, kv, wq, wo, g, b, seq):
    t = h.shape[0]
    tm = TOKEN_TILE
    per_b = seq // tm
    row = pl.BlockSpec((tm, D_MODEL), lambda i: (i, 0))
    full = lambda a: pl.BlockSpec(a.shape, lambda i: (0,) * a.ndim)
    return pl.pallas_call(
        _xattn_kernel,
        grid=(t // tm,),
        in_specs=[row, pl.BlockSpec((1,) + kv.shape[1:], lambda i: (i // per_b, 0, 0)),
                  full(wq), full(wo), full(g), full(b)],
        out_specs=row,
        out_shape=jax.ShapeDtypeStruct((t, D_MODEL), F32),
        compiler_params=_cparams(1),
        name="xattn",
    )(h, kv, wq, wo, g, b)


def _ffn_kernel(h_ref, w1_ref, b1_ref, w2_ref, b2_ref, g_ref, b_ref, out_ref, acc_ref):
    def tile(sub, rows):
        h = h_ref[rows, :]
        hb = h.astype(BF16)
        yield
        for c in range(D_FF // FFN_CHUNK):
            cols = slice(c * FFN_CHUNK, (c + 1) * FFN_CHUNK)
            a = jnp.dot(hb, w1_ref[:, cols], preferred_element_type=F32) + b1_ref[:, cols]
            a = jnp.square(jnp.maximum(a, 0.0)).astype(BF16)
            part = jnp.dot(a, w2_ref[cols, :], preferred_element_type=F32)
            if c == 0:
                acc_ref[rows, :] = part
            else:
                acc_ref[rows, :] += part
            yield
        ff = acc_ref[rows, :] + b2_ref[...]
        out_ref[rows, :] = _layer_norm(DEEPNORM_ALPHA * h + ff, g_ref[...], b_ref[...])

    _stagger([tile(i, rows) for i, rows in _sub_tiles(h_ref.shape[0])], lag=D_FF // FFN_CHUNK)


def _ffn(h, w1, b1, w2, b2, g, b):
    t = h.shape[0]
    tm = TOKEN_TILE
    row = pl.BlockSpec((tm, D_MODEL), lambda i: (i, 0))
    full = lambda a: pl.BlockSpec(a.shape, lambda i: (0,) * a.ndim)
    consts = (w1, b1, w2, b2, g, b)
    return pl.pallas_call(
        _ffn_kernel,
        grid=(t // tm,),
        in_specs=[row] + [full(a) for a in consts],
        out_specs=row,
        out_shape=jax.ShapeDtypeStruct((t, D_MODEL), F32),
        scratch_shapes=[pltpu.VMEM((tm, D_MODEL), F32)],
        compiler_params=_cparams(1),
        name="ffn",
    )(h, *consts)


def _rope_tables(positions):
    inv_freq = ROPE_THETA ** (-jnp.arange(0, ROT_DIM, 2, dtype=F32) / ROT_DIM)
    ang = positions.astype(F32).reshape(-1, 1) * inv_freq
    cos, sin = jnp.cos(ang), jnp.sin(ang)
    t = ang.shape[0]
    half = ROT_DIM // 2
    pad = jnp.zeros((t, ATT_HEAD_DIM - ROT_DIM), F32)
    zh = jnp.zeros((t, half), F32)
    cc = jnp.concatenate([cos, cos, pad + 1.0], axis=-1)
    s1 = jnp.concatenate([-sin, zh, pad], axis=-1)
    s2 = jnp.concatenate([zh, sin, pad], axis=-1)
    return tuple(jnp.tile(a, (1, LANES // ATT_HEAD_DIM)) for a in (cc, s1, s2))


def kernel(x, mem, positions, ln_in_g, ln_in_b, w_in, b_in, ssm_log_dt, ssm_a_re, ssm_a_im, ssm_b_re, ssm_b_im, ssm_c_re, ssm_c_im, ssm_d, w_glu, b_glu, w_att_up, w_mix_out, b_mix_out, ln1_g, ln1_b, w_xq, w_xkv, w_xo, ln2_g, ln2_b, w_ff1, b_ff1, w_ff2, b_ff2, ln3_g, ln3_b):
    bsz, seq, _ = x.shape
    t = bsz * seq
    row2 = lambda a: a.reshape(1, -1).astype(F32)
    cc, s1, s2 = _rope_tables(positions)
    h = x.reshape(t, D_MODEL)
    for l in range(DEPTH):
        h, uflat, *qkv, gs, ga = _ln_proj(
            h, row2(ln_in_g), row2(ln_in_b), w_in[l].astype(BF16), row2(b_in[l]), cc, s1, s2, bsz, seq)

        kc, vc, cm, ar, ai = _ssm_matrices(ssm_log_dt[l], ssm_a_re[l], ssm_a_im[l], ssm_b_re[l],
                                           ssm_b_im[l], ssm_c_re[l], ssm_c_im[l], ssm_d[l])
        tt, wp, wc = _ssm_build(kc, vc, cm)
        p = _ssm_p(uflat, wp)
        hprev = _ssm_scan(p, ar, ai, bsz)
        yg = _ssm_y(uflat, tt, hprev, wc)

        outs, lses = [], []
        for gi, dil in enumerate(DILATIONS):
            qg, kg, vg = (a.reshape(bsz * dil, seq // dil, ATT_MERGED) for a in qkv[3 * gi:3 * gi + 3])
            o_g, lse_g = _banded_attention(qg, kg, vg)
            outs.append(o_g.reshape(bsz, dil, seq // dil, ATT_MERGED))
            lses.append(lse_g.reshape(bsz, dil, seq // dil, ATT_MERGED))

        h = _mix(yg, outs, lses, gs, ga, h, w_glu[l].astype(BF16), row2(b_glu[l]),
                 w_att_up[l].astype(BF16), w_mix_out[l].astype(BF16), row2(b_mix_out[l]),
                 row2(ln1_g[l]), row2(ln1_b[l]), seq)

        kv = _mem_kv(mem, w_xkv[l].astype(BF16))
        h = _xattn(h, kv, w_xq[l].astype(BF16), w_xo[l].astype(BF16), row2(ln2_g[l]), row2(ln2_b[l]), seq)
        h = _ffn(h, w_ff1[l].astype(BF16), row2(b_ff1[l]), w_ff2[l].astype(BF16), row2(b_ff2[l]),
                 row2(ln3_g[l]), row2(ln3_b[l]))
    return h.reshape(bsz, seq, D_MODEL)
```

```python
import functools
import math

import jax
import jax.numpy as jnp
import numpy as np
from jax import lax
from jax.experimental import pallas as pl
from jax.experimental.pallas import tpu as pltpu

F32 = jnp.float32
BF16 = jnp.bfloat16

D_MODEL = 1024
SSM_GROUP = 16
SSM_WIDTH = 768
SSM_GROUPS = SSM_WIDTH // SSM_GROUP
SSM_STATE = 64
ATT_HEAD_DIM = 64
ATT_HEADS_PER_GROUP = 4
DILATIONS = (1, 4, 16)
ATT_SPAN = 128
ATT_WIDTH = 768
ATT_MERGED = ATT_HEADS_PER_GROUP * ATT_HEAD_DIM
ATT_SCALE = ATT_HEAD_DIM ** -0.5
ROT_DIM = ATT_HEAD_DIM // 4
ROPE_THETA = 500000.0
XATT_HEADS = 4
XATT_HEAD_DIM = D_MODEL // XATT_HEADS
XATT_SCALE = XATT_HEAD_DIM ** -0.5
D_FF = 4 * D_MODEL
DEPTH = 1
DEEPNORM_ALPHA = (2 * DEPTH) ** 0.25
LN_EPS = 1e-5
NEG_INF = -1e30
LOG2E = math.log2(math.e)
LN2 = math.log(2.0)

LANES = 128
CHUNK = 16
SG_GROUPS = LANES // SSM_GROUP
N_SG = SSM_GROUPS // SG_GROUPS
FLAT = CHUNK * LANES
SG_STATE = SG_GROUPS * SSM_STATE
MXU_TILE = 256
N_TT = FLAT // MXU_TILE

SSM_P_ROWS = 1024
SSM_Y_ROWS = 512
SUB_ROWS = 256
TOKEN_TILE = 512
MIX_CHUNK = 256
FFN_CHUNK = 1024
QKV_OFF = SSM_WIDTH
GATE_OFF = SSM_WIDTH + 3 * ATT_WIDTH
VMEM_LIMIT = 56 * 1024 * 1024


def _cparams(n_axes):
    return pltpu.CompilerParams(dimension_semantics=("parallel",) * n_axes,
                                vmem_limit_bytes=VMEM_LIMIT)


def _layer_norm(x, g, b):
    mu = jnp.mean(x, axis=-1, keepdims=True)
    xc = x - mu
    var = jnp.mean(xc * xc, axis=-1, keepdims=True)
    return xc * lax.rsqrt(var + LN_EPS) * g + b


def _stagger(tiles, lag):
    tiles = list(tiles)
    live = [True] * len(tiles)
    rnd = 0
    while any(live):
        for i in reversed(range(len(tiles))):
            if live[i] and rnd >= i * lag:
                try:
                    next(tiles[i])
                except StopIteration:
                    live[i] = False
        rnd += 1


def _sub_tiles(tm):
    return [(i, slice(i * SUB_ROWS, (i + 1) * SUB_ROWS)) for i in range(tm // SUB_ROWS)]


def _ln_proj_kernel(x_ref, g_ref, b_ref, w_ref, bi_ref, cs_ref, sel_ref, one_ref,
                    h_ref, u_ref, q0_ref, k0_ref, v0_ref, q1_ref, k1_ref, v1_ref, q2_ref, k2_ref, v2_ref,
                    gs_ref, ga_ref, scr_ref):
    half = ROT_DIM // 2
    qkv_refs = ((q0_ref, k0_ref, v0_ref), (q1_ref, k1_ref, v1_ref), (q2_ref, k2_ref, v2_ref))
    sections = [(0, SSM_WIDTH), (QKV_OFF, ATT_WIDTH), (QKV_OFF + ATT_WIDTH, ATT_WIDTH),
                (QKV_OFF + 2 * ATT_WIDTH, ATT_WIDTH), (GATE_OFF, D_MODEL), (GATE_OFF + D_MODEL, D_MODEL)]

    def tile(sub, rows):
        h = _layer_norm(x_ref[rows, :], g_ref[...], b_ref[...])
        h_ref[rows, :] = h
        hb = h.astype(BF16)
        cs = cs_ref[rows, :]
        cs_hi = cs.astype(BF16)
        cs_lo = (cs - cs_hi.astype(F32)).astype(BF16)
        tab = (jnp.dot(cs_hi, sel_ref[...], preferred_element_type=F32)
               + jnp.dot(cs_lo, sel_ref[...], preferred_element_type=F32))
        cc = tab[:, 0:LANES] + one_ref[...]
        s1 = tab[:, LANES:2 * LANES]
        s2 = tab[:, 2 * LANES:3 * LANES]
        yield

        def stage(off, val):
            for i in range(val.shape[1] // LANES):
                scr_ref[sub, off // LANES + i] = val[:, i * LANES:(i + 1) * LANES]

        def rope(val, scale):
            out = []
            for i in range(ATT_WIDTH // LANES):
                t = val[:, i * LANES:(i + 1) * LANES]
                out.append((t * cc + pltpu.roll(t, LANES - half, 1) * s1 + pltpu.roll(t, half, 1) * s2) * scale)
            return jnp.concatenate(out, axis=-1)

        def emit_groups(a):
            for gi, dil in enumerate(DILATIONS):
                ref = qkv_refs[gi][a]
                cb0 = (QKV_OFF + a * ATT_WIDTH + gi * ATT_MERGED) // LANES
                n = SUB_ROWS // dil
                for r in range(dil):
                    for e in range(ATT_MERGED // LANES):
                        if dil == 1:
                            blk = scr_ref[sub, cb0 + e]
                        else:
                            blk = scr_ref[sub, cb0 + e, pl.ds(r, n, stride=dil), :]
                        ref[0, r, sub * n:(sub + 1) * n, e * LANES:(e + 1) * LANES] = blk.astype(BF16)

        for c, (off, width) in enumerate(sections):
            val = (jnp.dot(hb, w_ref[:, off:off + width], preferred_element_type=F32)
                   + bi_ref[:, off:off + width])
            if c == 0:
                stage(off, val)
                n = SUB_ROWS // CHUNK
                for j in range(CHUNK):
                    for sg in range(N_SG):
                        blk = scr_ref[sub, sg, pl.ds(j, n, stride=CHUNK), :]
                        u_ref[sg, sub * n:(sub + 1) * n, j * LANES:(j + 1) * LANES] = blk.astype(BF16)
            elif c in (1, 2):
                stage(off, rope(val, ATT_SCALE * LOG2E if c == 1 else 1.0))
                emit_groups(c - 1)
            elif c == 3:
                stage(off, val)
                emit_groups(2)
            else:
                (gs_ref if c == 4 else ga_ref)[rows, :] = jax.nn.sigmoid(val).astype(BF16)
            yield

    _stagger([tile(i, rows) for i, rows in _sub_tiles(x_ref.shape[0])], lag=len(sections) - 1)


def _ln_proj(x2, g, b, w, bi, cs, bsz, seq):
    t = x2.shape[0]
    tm = TOKEN_TILE
    per_b = seq // tm
    sel, one = _rope_selectors()
    row = lambda n: pl.BlockSpec((tm, n), lambda i: (i, 0))
    full = lambda a: pl.BlockSpec(a.shape, lambda i: (0,) * a.ndim)
    grp = lambda d: pl.BlockSpec((1, d, tm // d, ATT_MERGED), lambda i: (i // per_b, 0, i % per_b, 0))
    outs = [jax.ShapeDtypeStruct((t, D_MODEL), F32),
            jax.ShapeDtypeStruct((N_SG, t // CHUNK, FLAT), BF16)]
    out_specs = [row(D_MODEL), pl.BlockSpec((N_SG, tm // CHUNK, FLAT), lambda i: (0, i, 0))]
    for d in DILATIONS:
        outs += [jax.ShapeDtypeStruct((bsz, d, seq // d, ATT_MERGED), BF16)] * 3
        out_specs += [grp(d)] * 3
    outs += [jax.ShapeDtypeStruct((t, D_MODEL), BF16)] * 2
    out_specs += [row(D_MODEL)] * 2
    return pl.pallas_call(
        _ln_proj_kernel,
        grid=(t // tm,),
        in_specs=[row(D_MODEL), full(g), full(b), full(w), full(bi), row(ROT_DIM), full(sel), full(one)],
        out_specs=out_specs,
        out_shape=outs,
        scratch_shapes=[pltpu.VMEM((tm // SUB_ROWS, GATE_OFF // LANES, SUB_ROWS, LANES), F32)],
        compiler_params=_cparams(1),
        name="ln_proj",
    )(x2, g, b, w, bi, cs, sel, one)


def _ssm_p_kernel(u_ref, wp_ref, p_ref):
    p_ref[0] = jnp.dot(u_ref[0], wp_ref[0], preferred_element_type=F32)


def _ssm_p(uflat, wp):
    nc = uflat.shape[1]
    tn = SSM_P_ROWS
    return pl.pallas_call(
        _ssm_p_kernel,
        grid=(N_SG, nc // tn),
        in_specs=[pl.BlockSpec((1, tn, FLAT), lambda s, i: (s, i, 0)),
                  pl.BlockSpec((1, FLAT, 2 * SG_STATE), lambda s, i: (s, 0, 0))],
        out_specs=pl.BlockSpec((1, tn, 2 * SG_STATE), lambda s, i: (s, i, 0)),
        out_shape=jax.ShapeDtypeStruct((N_SG, nc, 2 * SG_STATE), F32),
        compiler_params=_cparams(2),
        name="ssm_p",
    )(uflat, wp)


def _ssm_scan_kernel(p_ref, ar_ref, ai_ref, h_ref, *, steps, nseq):
    ar = ar_ref[0]
    ai = ai_ref[0]

    def body(c, carry):
        new = []
        for b in range(nseq):
            hr, hi = carry[b]
            row = b * steps + c
            h_ref[0, pl.ds(row, 1), 0:SG_STATE] = hr
            h_ref[0, pl.ds(row, 1), SG_STATE:2 * SG_STATE] = hi
            pr = p_ref[0, pl.ds(row, 1), 0:SG_STATE]
            pi = p_ref[0, pl.ds(row, 1), SG_STATE:2 * SG_STATE]
            new.append((ar * hr - ai * hi + pr, ar * hi + ai * hr + pi))
        return tuple(new)

    z = jnp.zeros((1, SG_STATE), F32)
    lax.fori_loop(0, steps, body, ((z, z),) * nseq, unroll=2)


def _ssm_scan(p, ar, ai, bsz):
    nc = p.shape[1]
    return pl.pallas_call(
        functools.partial(_ssm_scan_kernel, steps=nc // bsz, nseq=bsz),
        grid=(N_SG,),
        in_specs=[pl.BlockSpec((1, nc, 2 * SG_STATE), lambda s: (s, 0, 0)),
                  pl.BlockSpec((1, 1, SG_STATE), lambda s: (s, 0, 0)),
                  pl.BlockSpec((1, 1, SG_STATE), lambda s: (s, 0, 0))],
        out_specs=pl.BlockSpec((1, nc, 2 * SG_STATE), lambda s: (s, 0, 0)),
        out_shape=jax.ShapeDtypeStruct(p.shape, F32),
        compiler_params=_cparams(1),
        name="ssm_scan",
    )(p, ar, ai)


def _ssm_y_kernel(u_ref, t_ref, h_ref, wc_ref, y_ref, scr_ref):
    tn = u_ref.shape[1]
    hb = h_ref[0].astype(BF16)
    for jo in range(N_TT):
        cols = slice(jo * MXU_TILE, (jo + 1) * MXU_TILE)
        acc = jnp.dot(hb, wc_ref[0, :, cols], preferred_element_type=F32)
        for ji in range(jo + 1):
            acc += jnp.dot(u_ref[0, :, ji * MXU_TILE:(ji + 1) * MXU_TILE], t_ref[0, jo - ji],
                           preferred_element_type=F32)
        yg = jax.nn.gelu(acc, approximate=True)
        for e in range(MXU_TILE // LANES):
            j = jo * (MXU_TILE // LANES) + e
            scr_ref[pl.ds(j, tn, stride=CHUNK), :] = yg[:, e * LANES:(e + 1) * LANES]
    y_ref[...] = scr_ref[...].astype(BF16)


def _ssm_y(uflat, tt, hprev, wc):
    nc = uflat.shape[1]
    tn = SSM_Y_ROWS
    return pl.pallas_call(
        _ssm_y_kernel,
        grid=(N_SG, nc // tn),
        in_specs=[pl.BlockSpec((1, tn, FLAT), lambda s, i: (s, i, 0)),
                  pl.BlockSpec((1, N_TT, MXU_TILE, MXU_TILE), lambda s, i: (s, 0, 0, 0)),
                  pl.BlockSpec((1, tn, 2 * SG_STATE), lambda s, i: (s, i, 0)),
                  pl.BlockSpec((1, 2 * SG_STATE, FLAT), lambda s, i: (s, 0, 0))],
        out_specs=pl.BlockSpec((tn * CHUNK, LANES), lambda s, i: (i, s)),
        out_shape=jax.ShapeDtypeStruct((nc * CHUNK, SSM_WIDTH), BF16),
        scratch_shapes=[pltpu.VMEM((tn * CHUNK, LANES), F32)],
        compiler_params=_cparams(2),
        name="ssm_y",
    )(uflat, tt, hprev, wc)


def _ssm_matrices(log_dt, a_re, a_im, b_re, b_im, c_re, c_im, d):
    g, n, c = SSM_GROUPS, SSM_STATE, SSM_GROUP
    dt = jnp.exp(log_dt.astype(F32))[:, None]
    a_re = a_re.astype(F32)
    a_im = a_im.astype(F32)
    ks = jnp.arange(CHUNK + 1, dtype=F32)
    mag = jnp.exp((a_re * dt)[..., None] * ks)
    ang = (a_im * dt)[..., None] * ks
    pw_re = mag * jnp.cos(ang)
    pw_im = mag * jnp.sin(ang)
    ab_re, ab_im = pw_re[..., 1], pw_im[..., 1]
    den = jnp.square(a_re) + jnp.square(a_im)
    nr = ab_re - 1.0
    f_re = (nr * a_re + ab_im * a_im) / den
    f_im = (ab_im * a_re - nr * a_im) / den
    b_re = b_re.astype(F32)
    b_im = b_im.astype(F32)
    bb_re = f_re[..., None] * b_re - f_im[..., None] * b_im
    bb_im = f_re[..., None] * b_im + f_im[..., None] * b_re
    c_re = c_re.astype(F32)
    c_im = c_im.astype(F32)
    ca_re = c_re[..., None] * pw_re[:, None] - c_im[..., None] * pw_im[:, None]
    ca_im = c_re[..., None] * pw_im[:, None] + c_im[..., None] * pw_re[:, None]

    kk = (jnp.einsum('gonk,gni->gkio', ca_re[..., :CHUNK], bb_re)
          - jnp.einsum('gonk,gni->gkio', ca_im[..., :CHUNK], bb_im))
    lag0_diag = (np.arange(CHUNK)[:, None, None] == 0) & np.eye(c, dtype=bool)[None]
    kk = kk + jnp.where(lag0_diag[None], d.astype(F32).reshape(g, 1, c, 1), 0.0)
    ktab = jnp.concatenate([jnp.zeros((g, 1, c, c), F32), kk], axis=1)
    dd = np.arange(N_TT)[:, None, None]
    s2 = np.arange(2)[None, :, None]
    t2 = np.arange(2)[None, None, :]
    lag = 2 * dd + t2 - s2 + 1
    x = ktab[:, lag].reshape(N_SG, SG_GROUPS, N_TT, 2, 2, c, c)
    kc = jnp.transpose(x, (0, 2, 3, 1, 5, 4, 6)).reshape(N_SG, N_TT, MXU_TILE, 2 * c)

    rp_re = pw_re[..., CHUNK - 1::-1][..., :CHUNK]
    rp_im = pw_im[..., CHUNK - 1::-1][..., :CHUNK]
    v_re = jnp.einsum('gns,gni->gsin', rp_re, bb_re) - jnp.einsum('gns,gni->gsin', rp_im, bb_im)
    v_im = jnp.einsum('gns,gni->gsin', rp_re, bb_im) + jnp.einsum('gns,gni->gsin', rp_im, bb_re)

    def rows_p(v):
        return jnp.transpose(v.reshape(N_SG, SG_GROUPS, CHUNK, c, n), (0, 2, 1, 3, 4)).reshape(N_SG, FLAT, n)

    vc = jnp.stack([rows_p(v_re), rows_p(v_im)], axis=1)

    def rows_c(m):
        return jnp.transpose(m[..., 1:], (0, 2, 3, 1)).reshape(N_SG, SG_STATE, CHUNK * c)

    cm = jnp.stack([rows_c(ca_re), rows_c(-ca_im)], axis=1)

    ar = pw_re[..., CHUNK].reshape(N_SG, 1, SG_STATE)
    ai = pw_im[..., CHUNK].reshape(N_SG, 1, SG_STATE)
    return kc.astype(BF16), vc.astype(BF16), cm.astype(BF16), ar, ai


def _ssm_build_kernel(kc_ref, vc_ref, cm_ref, tt_ref, wp_ref, wc_ref):
    def iota(shape, dim):
        return lax.broadcasted_iota(jnp.int32, shape, dim)

    def spread(nrows):
        r, cidx = iota((nrows, nrows * SG_GROUPS), 0), iota((nrows, nrows * SG_GROUPS), 1)
        return jnp.where(((r >> 4) == (cidx >> 7)) & ((r & 15) == (cidx & 15)), 1.0, 0.0).astype(BF16)

    grp16 = lambda idx: (idx >> 4) & (SG_GROUPS - 1)
    grp64 = lambda idx: (idx >> 6) & (SG_GROUPS - 1)

    e_t = spread(2 * SSM_GROUP)
    keep = grp16(iota((MXU_TILE, MXU_TILE), 0)) == grp16(iota((MXU_TILE, MXU_TILE), 1))
    for dlt in range(N_TT):
        full = jnp.dot(kc_ref[0, dlt], e_t, preferred_element_type=F32)
        tt_ref[0, dlt] = jnp.where(keep, full, 0.0).astype(BF16)

    r, cidx = iota((SSM_STATE, SG_STATE), 0), iota((SSM_STATE, SG_STATE), 1)
    e_p = jnp.where(r == (cidx & (SSM_STATE - 1)), 1.0, 0.0).astype(BF16)
    keep = grp16(iota((FLAT, SG_STATE), 0)) == grp64(iota((FLAT, SG_STATE), 1))
    for part in range(2):
        full = jnp.dot(vc_ref[0, part], e_p, preferred_element_type=F32)
        wp_ref[0, :, part * SG_STATE:(part + 1) * SG_STATE] = jnp.where(keep, full, 0.0).astype(BF16)

    e_c = spread(CHUNK * SSM_GROUP)
    keep = grp64(iota((SG_STATE, FLAT), 0)) == grp16(iota((SG_STATE, FLAT), 1))
    for part in range(2):
        full = jnp.dot(cm_ref[0, part], e_c, preferred_element_type=F32)
        wc_ref[0, part * SG_STATE:(part + 1) * SG_STATE, :] = jnp.where(keep, full, 0.0).astype(BF16)


def _ssm_build(kc, vc, cm):
    blk = lambda a: pl.BlockSpec((1,) + a.shape[1:], lambda s: (s,) + (0,) * (a.ndim - 1))
    outs = [jax.ShapeDtypeStruct((N_SG, N_TT, MXU_TILE, MXU_TILE), BF16),
            jax.ShapeDtypeStruct((N_SG, FLAT, 2 * SG_STATE), BF16),
            jax.ShapeDtypeStruct((N_SG, 2 * SG_STATE, FLAT), BF16)]
    return pl.pallas_call(
        _ssm_build_kernel,
        grid=(N_SG,),
        in_specs=[blk(kc), blk(vc), blk(cm)],
        out_specs=[blk(o) for o in outs],
        out_shape=outs,
        compiler_params=_cparams(1),
        name="ssm_build",
    )(kc, vc, cm)


ATT_UNITS_IN_FLIGHT = 4


def _attn_kernel(q_ref, kp_ref, kc_ref, vp_ref, vc_ref, o_ref, lse_ref, ks_ref, vs_ref):
    qb = q_ref.shape[1]
    blk = ATT_SPAN
    first = pl.program_id(1) == 0

    lane = lax.broadcasted_iota(jnp.int32, (1, ATT_MERGED), 1)
    head_sel = (lane % LANES < ATT_HEAD_DIM, lane % LANES >= ATT_HEAD_DIM)
    for src_p, src_c, dst in ((kp_ref, kc_ref, ks_ref), (vp_ref, vc_ref, vs_ref)):
        for e in range(2):
            xp, xc = src_p[0], src_c[0]
            dst[e, 0:blk] = jnp.where(head_sel[e], xp, jnp.zeros_like(xp))
            dst[e, blk:blk + qb] = jnp.where(head_sel[e], xc, jnp.zeros_like(xc))

    qi = lax.broadcasted_iota(jnp.int32, (blk, 2 * blk), 0)
    kk = lax.broadcasted_iota(jnp.int32, (blk, 2 * blk), 1)
    band = jnp.logical_and(kk >= qi, kk <= qi + blk)
    band_first = jnp.logical_and(band, jnp.logical_or(kk >= blk, jnp.logical_not(first)))
    low = lax.broadcasted_iota(jnp.int32, (blk, LANES), 1) < ATT_HEAD_DIM
    dn = (((1,), (1,)), ((), ()))

    units = [(j, pair) for j in range(qb // blk) for pair in range(ATT_MERGED // LANES)]
    for g0 in range(0, len(units), ATT_UNITS_IN_FLIGHT):
        group = units[g0:g0 + ATT_UNITS_IN_FLIGHT]
        scores = []
        for j, pair in group:
            cols = slice(pair * LANES, (pair + 1) * LANES)
            keys = slice(j * blk, (j + 2) * blk)
            kcat = jnp.concatenate([ks_ref[0, keys, cols], ks_ref[1, keys, cols]], axis=0)
            s = lax.dot_general(q_ref[0, j * blk:(j + 1) * blk, cols], kcat, dn,
                                preferred_element_type=F32)
            mask = band_first if j == 0 else band
            scores.append([jnp.where(mask, s[:, e * 2 * blk:(e + 1) * 2 * blk], NEG_INF) for e in range(2)])
        maxes = [[jnp.max(jnp.maximum(se[:, :blk], se[:, blk:]), axis=-1, keepdims=True) for se in su]
                 for su in scores]
        probs = [[jnp.exp2(se - me) for se, me in zip(su, mu)] for su, mu in zip(scores, maxes)]
        dens = [[jnp.sum(pe[:, :blk] + pe[:, blk:], axis=-1, keepdims=True) for pe in pu] for pu in probs]
        for (j, pair), pu, mu, du in zip(group, probs, maxes, dens):
            cols = slice(pair * LANES, (pair + 1) * LANES)
            rows = slice(j * blk, (j + 1) * blk)
            keys = slice(j * blk, (j + 2) * blk)
            vcat = jnp.concatenate([vs_ref[0, keys, cols], vs_ref[1, keys, cols]], axis=0)
            pcat = jnp.concatenate([pu[0].astype(BF16), pu[1].astype(BF16)], axis=1)
            num = jnp.dot(pcat, vcat, preferred_element_type=F32)
            o_ref[0, rows, cols] = (num * jnp.where(low, 1.0 / du[0], 1.0 / du[1])).astype(BF16)
            lse_ref[0, rows, cols] = jnp.where(low, mu[0] * LN2 + jnp.log(du[0]), mu[1] * LN2 + jnp.log(du[1]))


def _banded_attention(q, k, v):
    nseq, length, w = q.shape
    blk = ATT_SPAN
    qb = min(length, 4 * blk)
    per = qb // blk
    cur = pl.BlockSpec((1, qb, w), lambda s, n: (s, n, 0))
    prev = pl.BlockSpec((1, blk, w), lambda s, n: (s, jnp.maximum(n * per - 1, 0), 0))
    return pl.pallas_call(
        _attn_kernel,
        grid=(nseq, length // qb),
        in_specs=[cur, prev, cur, prev, cur],
        out_specs=[cur, cur],
        out_shape=[jax.ShapeDtypeStruct(q.shape, BF16), jax.ShapeDtypeStruct(q.shape, F32)],
        scratch_shapes=[pltpu.VMEM((2, blk + qb, w), BF16)] * 2,
        compiler_params=_cparams(2),
        name="attn",
    )(q, k, k, v, v)


def _mix_kernel(y_ref, o0_ref, o1_ref, o2_ref, l0_ref, l1_ref, l2_ref, gs_ref, ga_ref, h_ref,
                wglu_ref, bglu_ref, wup_ref, wmix_ref, bmix_ref, g_ref, b_ref, out_ref, scr_ref, acc_ref):
    o_refs = (o0_ref, o1_ref, o2_ref)
    l_refs = (l0_ref, l1_ref, l2_ref)

    def tile(sub, rows):
        def natural(ref, dil):
            n = SUB_ROWS // dil
            if dil == 1:
                return ref[0, 0, rows, :].astype(F32)
            for r in range(dil):
                for e in range(ATT_MERGED // LANES):
                    scr_ref[sub, e, pl.ds(r, n, stride=dil), :] = (
                        ref[0, r, sub * n:(sub + 1) * n, e * LANES:(e + 1) * LANES].astype(F32))
            return jnp.concatenate([scr_ref[sub, e] for e in range(ATT_MERGED // LANES)], axis=-1)

        ls = [natural(l_refs[gi], dil) for gi, dil in enumerate(DILATIONS)]
        m = jnp.maximum(jnp.maximum(ls[0], ls[1]), ls[2])
        es = [jnp.exp(l - m) for l in ls]
        att = es[0] * natural(o_refs[0], DILATIONS[0])
        for gi in (1, 2):
            att += es[gi] * natural(o_refs[gi], DILATIONS[gi])
        att = (att / (es[0] + es[1] + es[2])).astype(BF16)
        y = y_ref[rows, :]
        yield

        for c in range(D_MODEL // MIX_CHUNK):
            cols = slice(c * MIX_CHUNK, (c + 1) * MIX_CHUNK)
            gate_cols = slice(D_MODEL + c * MIX_CHUNK, D_MODEL + (c + 1) * MIX_CHUNK)
            val = jnp.dot(y, wglu_ref[:, cols], preferred_element_type=F32) + bglu_ref[:, cols]
            gate = jnp.dot(y, wglu_ref[:, gate_cols], preferred_element_type=F32) + bglu_ref[:, gate_cols]
            b_att = jnp.dot(att, wup_ref[:, cols], preferred_element_type=F32)
            mixed = (gs_ref[rows, cols].astype(F32) * (val * jax.nn.sigmoid(gate))
                     + ga_ref[rows, cols].astype(F32) * b_att)
            part = jnp.dot(mixed.astype(BF16), wmix_ref[cols, :], preferred_element_type=F32)
            if c == 0:
                acc_ref[rows, :] = part
            else:
                acc_ref[rows, :] += part
            yield

        r = acc_ref[rows, :] + bmix_ref[...]
        out_ref[rows, :] = _layer_norm(DEEPNORM_ALPHA * h_ref[rows, :] + r, g_ref[...], b_ref[...])

    _stagger([tile(i, rows) for i, rows in _sub_tiles(h_ref.shape[0])], lag=D_MODEL // MIX_CHUNK)


def _mix(yg, outs, lses, gs, ga, h, wglu, bglu, wup, wmix, bmix, g, b, seq):
    t = h.shape[0]
    tm = TOKEN_TILE
    per_b = seq // tm
    row = lambda n: pl.BlockSpec((tm, n), lambda i: (i, 0))
    full = lambda a: pl.BlockSpec(a.shape, lambda i: (0,) * a.ndim)
    grp = lambda d: pl.BlockSpec((1, d, tm // d, ATT_MERGED), lambda i: (i // per_b, 0, i % per_b, 0))
    consts = (wglu, bglu, wup, wmix, bmix, g, b)
    return pl.pallas_call(
        _mix_kernel,
        grid=(t // tm,),
        in_specs=([row(SSM_WIDTH)] + [grp(d) for d in DILATIONS] * 2 + [row(D_MODEL)] * 3
                  + [full(a) for a in consts]),
        out_specs=row(D_MODEL),
        out_shape=jax.ShapeDtypeStruct((t, D_MODEL), F32),
        scratch_shapes=[pltpu.VMEM((tm // SUB_ROWS, ATT_MERGED // LANES, SUB_ROWS, LANES), F32),
                        pltpu.VMEM((tm, D_MODEL), F32)],
        compiler_params=_cparams(1),
        name="mix",
    )(yg, *outs, *lses, gs, ga, h, *consts)


def _mem_kv_kernel(m_ref, w_ref, kv_ref):
    kv_ref[0] = jnp.dot(m_ref[0].astype(BF16), w_ref[...], preferred_element_type=F32).astype(BF16)


def _mem_kv(mem, w):
    bsz, ml, _ = mem.shape
    return pl.pallas_call(
        _mem_kv_kernel,
        grid=(bsz,),
        in_specs=[pl.BlockSpec((1, ml, D_MODEL), lambda b: (b, 0, 0)),
                  pl.BlockSpec(w.shape, lambda b: (0, 0))],
        out_specs=pl.BlockSpec((1, ml, 2 * D_MODEL), lambda b: (b, 0, 0)),
        out_shape=jax.ShapeDtypeStruct((bsz, ml, 2 * D_MODEL), BF16),
        compiler_params=_cparams(1),
        name="mem_kv",
    )(mem, w)


def _xattn_kernel(h_ref, kv_ref, wq_ref, wo_ref, g_ref, b_ref, out_ref):
    dn = (((1,), (1,)), ((), ()))

    def tile(sub, rows):
        h = h_ref[rows, :]
        hb = h.astype(BF16)
        yield
        q = (jnp.dot(hb, wq_ref[...], preferred_element_type=F32) * XATT_SCALE).astype(BF16)
        yield
        outs = []
        for hh in range(XATT_HEADS):
            sl = slice(hh * XATT_HEAD_DIM, (hh + 1) * XATT_HEAD_DIM)
            sv = slice(D_MODEL + hh * XATT_HEAD_DIM, D_MODEL + (hh + 1) * XATT_HEAD_DIM)
            s = lax.dot_general(q[:, sl], kv_ref[0, :, sl], dn, preferred_element_type=F32)
            e = jnp.exp(s - jnp.max(s, axis=-1, keepdims=True))
            p = e / jnp.sum(e, axis=-1, keepdims=True)
            outs.append(jnp.dot(p.astype(BF16), kv_ref[0, :, sv], preferred_element_type=F32).astype(BF16))
            yield
        xo = jnp.dot(jnp.concatenate(outs, axis=-1), wo_ref[...], preferred_element_type=F32)
        yield
        out_ref[rows, :] = _layer_norm(DEEPNORM_ALPHA * h + xo, g_ref[...], b_ref[...])

    _stagger([tile(i, rows) for i, rows in _sub_tiles(h_ref.shape[0])], lag=XATT_HEADS + 2)


def _xattn(h, kv, wq, wo, g, b, seq):
    t = h.shape[0]
    tm = TOKEN_TILE
    per_b = seq // tm
    row = pl.BlockSpec((tm, D_MODEL), lambda i: (i, 0))
    full = lambda a: pl.BlockSpec(a.shape, lambda i: (0,) * a.ndim)
    return pl.pallas_call(
        _xattn_kernel,
        grid=(t // tm,),
        in_specs=[row, pl.BlockSpec((1,) + kv.shape[1:], lambda i: (i // per_b, 0, 0)),
                  full(wq), full(wo), full(g), full(b)],
        out_specs=row,
        out_shape=jax.ShapeDtypeStruct((t, D_MODEL), F32),
        compiler_params=_cparams(1),
        name="xattn",
    )(h, kv, wq, wo, g, b)


def _ffn_kernel(h_ref, w1_ref, b1_ref, w2_ref, b2_ref, g_ref, b_ref, out_ref, acc_ref):
    def tile(sub, rows):
        h = h_ref[rows, :]
        hb = h.astype(BF16)
        yield
        for c in range(D_FF // FFN_CHUNK):
            cols = slice(c * FFN_CHUNK, (c + 1) * FFN_CHUNK)
            a = jnp.dot(hb, w1_ref[:, cols], preferred_element_type=F32) + b1_ref[:, cols]
            a = jnp.square(jnp.maximum(a, 0.0)).astype(BF16)
            part = jnp.dot(a, w2_ref[cols, :], preferred_element_type=F32)
            if c == 0:
                acc_ref[rows, :] = part
            else:
                acc_ref[rows, :] += part
            yield
        ff = acc_ref[rows, :] + b2_ref[...]
        out_ref[rows, :] = _layer_norm(DEEPNORM_ALPHA * h + ff, g_ref[...], b_ref[...])

    _stagger([tile(i, rows) for i, rows in _sub_tiles(h_ref.shape[0])], lag=D_FF // FFN_CHUNK)


def _ffn(h, w1, b1, w2, b2, g, b):
    t = h.shape[0]
    tm = TOKEN_TILE
    row = pl.BlockSpec((tm, D_MODEL), lambda i: (i, 0))
    full = lambda a: pl.BlockSpec(a.shape, lambda i: (0,) * a.ndim)
    consts = (w1, b1, w2, b2, g, b)
    return pl.pallas_call(
        _ffn_kernel,
        grid=(t // tm,),
        in_specs=[row] + [full(a) for a in consts],
        out_specs=row,
        out_shape=jax.ShapeDtypeStruct((t, D_MODEL), F32),
        scratch_shapes=[pltpu.VMEM((tm, D_MODEL), F32)],
        compiler_params=_cparams(1),
        name="ffn",
    )(h, *consts)


def _rope_selectors():
    half = ROT_DIM // 2
    sel = np.zeros((ROT_DIM, 3 * LANES), np.float32)
    one = np.zeros((1, LANES), np.float32)
    for lane in range(LANES):
        d = lane % ATT_HEAD_DIM
        if d >= ROT_DIM:
            one[0, lane] = 1.0
            continue
        sel[d % half, lane] = 1.0
        if d < half:
            sel[half + d, LANES + lane] = -1.0
        else:
            sel[d, 2 * LANES + lane] = 1.0
    return jnp.asarray(sel, BF16), jnp.asarray(one)


def _rope_cos_sin(positions):
    inv_freq = ROPE_THETA ** (-jnp.arange(0, ROT_DIM, 2, dtype=F32) / ROT_DIM)
    ang = positions.astype(F32).reshape(-1, 1) * inv_freq
    return jnp.concatenate([jnp.cos(ang), jnp.sin(ang)], axis=-1)


def kernel(x, mem, positions, ln_in_g, ln_in_b, w_in, b_in, ssm_log_dt, ssm_a_re, ssm_a_im, ssm_b_re, ssm_b_im, ssm_c_re, ssm_c_im, ssm_d, w_glu, b_glu, w_att_up, w_mix_out, b_mix_out, ln1_g, ln1_b, w_xq, w_xkv, w_xo, ln2_g, ln2_b, w_ff1, b_ff1, w_ff2, b_ff2, ln3_g, ln3_b):
    bsz, seq, _ = x.shape
    t = bsz * seq
    row2 = lambda a: a.reshape(1, -1).astype(F32)
    cs = _rope_cos_sin(positions)
    h = x.reshape(t, D_MODEL)
    for l in range(DEPTH):
        h, uflat, *qkv, gs, ga = _ln_proj(
            h, row2(ln_in_g), row2(ln_in_b), w_in[l].astype(BF16), row2(b_in[l]), cs, bsz, seq)

        kc, vc, cm, ar, ai = _ssm_matrices(ssm_log_dt[l], ssm_a_re[l], ssm_a_im[l], ssm_b_re[l],
                                           ssm_b_im[l], ssm_c_re[l], ssm_c_im[l], ssm_d[l])
        tt, wp, wc = _ssm_build(kc, vc, cm)
        p = _ssm_p(uflat, wp)
        hprev = _ssm_scan(p, ar, ai, bsz)
        yg = _ssm_y(uflat, tt, hprev, wc)

        outs, lses = [], []
        for gi, dil in enumerate(DILATIONS):
            qg, kg, vg = (a.reshape(bsz * dil, seq // dil, ATT_MERGED) for a in qkv[3 * gi:3 * gi + 3])
            o_g, lse_g = _banded_attention(qg, kg, vg)
            outs.append(o_g.reshape(bsz, dil, seq // dil, ATT_MERGED))
            lses.append(lse_g.reshape(bsz, dil, seq // dil, ATT_MERGED))

        h = _mix(yg, outs, lses, gs, ga, h, w_glu[l].astype(BF16), row2(b_glu[l]),
                 w_att_up[l].astype(BF16), w_mix_out[l].astype(BF16), row2(b_mix_out[l]),
                 row2(ln1_g[l]), row2(ln1_b[l]), seq)

        kv = _mem_kv(mem, w_xkv[l].astype(BF16))
        h = _xattn(h, kv, w_xq[l].astype(BF16), w_xo[l].astype(BF16), row2(ln2_g[l]), row2(ln2_b[l]), seq)
        h = _ffn(h, w_ff1[l].astype(BF16), row2(b_ff1[l]), w_ff2[l].astype(BF16), row2(b_ff2[l]),
                 row2(ln3_g[l]), row2(ln3_b[l]))
    return h.reshape(bsz, seq, D_MODEL)
```

```python
import functools
import math

import jax
import jax.numpy as jnp
import numpy as np
from jax import lax
from jax.experimental import pallas as pl
from jax.experimental.pallas import tpu as pltpu

F32 = jnp.float32
BF16 = jnp.bfloat16

D_MODEL = 1024
SSM_GROUP = 16
SSM_WIDTH = 768
SSM_GROUPS = SSM_WIDTH // SSM_GROUP
SSM_STATE = 64
ATT_HEAD_DIM = 64
ATT_HEADS_PER_GROUP = 4
DILATIONS = (1, 4, 16)
ATT_SPAN = 128
ATT_WIDTH = 768
ATT_MERGED = ATT_HEADS_PER_GROUP * ATT_HEAD_DIM
ATT_SCALE = ATT_HEAD_DIM ** -0.5
ROT_DIM = ATT_HEAD_DIM // 4
ROPE_THETA = 500000.0
XATT_HEADS = 4
XATT_HEAD_DIM = D_MODEL // XATT_HEADS
XATT_SCALE = XATT_HEAD_DIM ** -0.5
D_FF = 4 * D_MODEL
DEPTH = 1
DEEPNORM_ALPHA = (2 * DEPTH) ** 0.25
LN_EPS = 1e-5
NEG_INF = -1e30
LOG2E = math.log2(math.e)
LN2 = math.log(2.0)

LANES = 128
CHUNK = 16
SG_GROUPS = LANES // SSM_GROUP
N_SG = SSM_GROUPS // SG_GROUPS
FLAT = CHUNK * LANES
SG_STATE = SG_GROUPS * SSM_STATE
MXU_TILE = 256
N_TT = FLAT // MXU_TILE

SSM_P_ROWS = 1024
SSM_Y_ROWS = 512
SUB_ROWS = 256
TOKEN_TILE = 512
MIX_CHUNK = 256
FFN_CHUNK = 1024
QKV_OFF = SSM_WIDTH
GATE_OFF = SSM_WIDTH + 3 * ATT_WIDTH
VMEM_LIMIT = 56 * 1024 * 1024


def _cparams(n_axes):
    return pltpu.CompilerParams(dimension_semantics=("parallel",) * n_axes,
                                vmem_limit_bytes=VMEM_LIMIT)


def _layer_norm(x, g, b):
    mu = jnp.mean(x, axis=-1, keepdims=True)
    xc = x - mu
    var = jnp.mean(xc * xc, axis=-1, keepdims=True)
    return xc * lax.rsqrt(var + LN_EPS) * g + b


def _sigmoid(x):
    return 0.5 * jnp.tanh(0.5 * x) + 0.5


def _stagger(tiles, lag):
    tiles = list(tiles)
    live = [True] * len(tiles)
    rnd = 0
    while any(live):
        for i in reversed(range(len(tiles))):
            if live[i] and rnd >= i * lag:
                try:
                    next(tiles[i])
                except StopIteration:
                    live[i] = False
        rnd += 1


def _sub_tiles(tm):
    return [(i, slice(i * SUB_ROWS, (i + 1) * SUB_ROWS)) for i in range(tm // SUB_ROWS)]


def _ln_proj_kernel(x_ref, g_ref, b_ref, w_ref, bi_ref, cs_ref, sel_ref, one_ref,
                    h_ref, u_ref, q0_ref, k0_ref, v0_ref, q1_ref, k1_ref, v1_ref, q2_ref, k2_ref, v2_ref,
                    gs_ref, ga_ref, scr_ref):
    half = ROT_DIM // 2
    qkv_refs = ((q0_ref, k0_ref, v0_ref), (q1_ref, k1_ref, v1_ref), (q2_ref, k2_ref, v2_ref))
    sections = [(0, SSM_WIDTH), (QKV_OFF, ATT_WIDTH), (QKV_OFF + ATT_WIDTH, ATT_WIDTH),
                (QKV_OFF + 2 * ATT_WIDTH, ATT_WIDTH), (GATE_OFF, D_MODEL), (GATE_OFF + D_MODEL, D_MODEL)]

    def tile(sub, rows):
        h = _layer_norm(x_ref[rows, :], g_ref[...], b_ref[...])
        h_ref[rows, :] = h
        hb = h.astype(BF16)
        cs = cs_ref[rows, :]
        cs_hi = cs.astype(BF16)
        cs_lo = (cs - cs_hi.astype(F32)).astype(BF16)
        tab = (jnp.dot(cs_hi, sel_ref[...], preferred_element_type=F32)
               + jnp.dot(cs_lo, sel_ref[...], preferred_element_type=F32))
        cc = tab[:, 0:LANES] + one_ref[...]
        s1 = tab[:, LANES:2 * LANES]
        s2 = tab[:, 2 * LANES:3 * LANES]
        yield

        def stage(off, val):
            for i in range(val.shape[1] // LANES):
                scr_ref[sub, off // LANES + i] = val[:, i * LANES:(i + 1) * LANES]

        def rope(val, scale):
            out = []
            for i in range(ATT_WIDTH // LANES):
                t = val[:, i * LANES:(i + 1) * LANES]
                out.append((t * cc + pltpu.roll(t, LANES - half, 1) * s1 + pltpu.roll(t, half, 1) * s2) * scale)
            return jnp.concatenate(out, axis=-1)

        def emit_groups(a):
            for gi, dil in enumerate(DILATIONS):
                ref = qkv_refs[gi][a]
                cb0 = (QKV_OFF + a * ATT_WIDTH + gi * ATT_MERGED) // LANES
                n = SUB_ROWS // dil
                for r in range(dil):
                    for e in range(ATT_MERGED // LANES):
                        if dil == 1:
                            blk = scr_ref[sub, cb0 + e]
                        else:
                            blk = scr_ref[sub, cb0 + e, pl.ds(r, n, stride=dil), :]
                        ref[0, r, sub * n:(sub + 1) * n, e * LANES:(e + 1) * LANES] = blk.astype(BF16)

        for c, (off, width) in enumerate(sections):
            val = (jnp.dot(hb, w_ref[:, off:off + width], preferred_element_type=F32)
                   + bi_ref[:, off:off + width])
            if c == 0:
                stage(off, val)
                n = SUB_ROWS // CHUNK
                for j in range(CHUNK):
                    for sg in range(N_SG):
                        blk = scr_ref[sub, sg, pl.ds(j, n, stride=CHUNK), :]
                        u_ref[sg, sub * n:(sub + 1) * n, j * LANES:(j + 1) * LANES] = blk.astype(BF16)
            elif c in (1, 2):
                stage(off, rope(val, ATT_SCALE * LOG2E if c == 1 else 1.0))
                emit_groups(c - 1)
            elif c == 3:
                stage(off, val)
                emit_groups(2)
            else:
                (gs_ref if c == 4 else ga_ref)[rows, :] = _sigmoid(val).astype(BF16)
            yield

    _stagger([tile(i, rows) for i, rows in _sub_tiles(x_ref.shape[0])], lag=len(sections) - 1)


def _ln_proj(x2, g, b, w, bi, cs, bsz, seq):
    t = x2.shape[0]
    tm = TOKEN_TILE
    per_b = seq // tm
    sel, one = _rope_selectors()
    row = lambda n: pl.BlockSpec((tm, n), lambda i: (i, 0))
    full = lambda a: pl.BlockSpec(a.shape, lambda i: (0,) * a.ndim)
    grp = lambda d: pl.BlockSpec((1, d, tm // d, ATT_MERGED), lambda i: (i // per_b, 0, i % per_b, 0))
    outs = [jax.ShapeDtypeStruct((t, D_MODEL), F32),
            jax.ShapeDtypeStruct((N_SG, t // CHUNK, FLAT), BF16)]
    out_specs = [row(D_MODEL), pl.BlockSpec((N_SG, tm // CHUNK, FLAT), lambda i: (0, i, 0))]
    for d in DILATIONS:
        outs += [jax.ShapeDtypeStruct((bsz, d, seq // d, ATT_MERGED), BF16)] * 3
        out_specs += [grp(d)] * 3
    outs += [jax.ShapeDtypeStruct((t, D_MODEL), BF16)] * 2
    out_specs += [row(D_MODEL)] * 2
    return pl.pallas_call(
        _ln_proj_kernel,
        grid=(t // tm,),
        in_specs=[row(D_MODEL), full(g), full(b), full(w), full(bi), row(ROT_DIM), full(sel), full(one)],
        out_specs=out_specs,
        out_shape=outs,
        scratch_shapes=[pltpu.VMEM((tm // SUB_ROWS, GATE_OFF // LANES, SUB_ROWS, LANES), F32)],
        compiler_params=_cparams(1),
        name="ln_proj",
    )(x2, g, b, w, bi, cs, sel, one)


def _ssm_p_kernel(u_ref, wp_ref, p_ref):
    p_ref[0] = jnp.dot(u_ref[0], wp_ref[0], preferred_element_type=F32)


def _ssm_p(uflat, wp):
    nc = uflat.shape[1]
    tn = SSM_P_ROWS
    return pl.pallas_call(
        _ssm_p_kernel,
        grid=(N_SG, nc // tn),
        in_specs=[pl.BlockSpec((1, tn, FLAT), lambda s, i: (s, i, 0)),
                  pl.BlockSpec((1, FLAT, 2 * SG_STATE), lambda s, i: (s, 0, 0))],
        out_specs=pl.BlockSpec((1, tn, 2 * SG_STATE), lambda s, i: (s, i, 0)),
        out_shape=jax.ShapeDtypeStruct((N_SG, nc, 2 * SG_STATE), F32),
        compiler_params=_cparams(2),
        name="ssm_p",
    )(uflat, wp)


def _ssm_scan_kernel(p_ref, ar_ref, ai_ref, h_ref, *, steps, nseq):
    ar = ar_ref[0]
    ai = ai_ref[0]

    def body(c, carry):
        new = []
        for b in range(nseq):
            hr, hi = carry[b]
            row = b * steps + c
            h_ref[0, pl.ds(row, 1), 0:SG_STATE] = hr
            h_ref[0, pl.ds(row, 1), SG_STATE:2 * SG_STATE] = hi
            pr = p_ref[0, pl.ds(row, 1), 0:SG_STATE]
            pi = p_ref[0, pl.ds(row, 1), SG_STATE:2 * SG_STATE]
            new.append((ar * hr - ai * hi + pr, ar * hi + ai * hr + pi))
        return tuple(new)

    z = jnp.zeros((1, SG_STATE), F32)
    lax.fori_loop(0, steps, body, ((z, z),) * nseq, unroll=2)


def _ssm_scan(p, ar, ai, bsz):
    nc = p.shape[1]
    return pl.pallas_call(
        functools.partial(_ssm_scan_kernel, steps=nc // bsz, nseq=bsz),
        grid=(N_SG,),
        in_specs=[pl.BlockSpec((1, nc, 2 * SG_STATE), lambda s: (s, 0, 0)),
                  pl.BlockSpec((1, 1, SG_STATE), lambda s: (s, 0, 0)),
                  pl.BlockSpec((1, 1, SG_STATE), lambda s: (s, 0, 0))],
        out_specs=pl.BlockSpec((1, nc, 2 * SG_STATE), lambda s: (s, 0, 0)),
        out_shape=jax.ShapeDtypeStruct(p.shape, F32),
        compiler_params=_cparams(1),
        name="ssm_scan",
    )(p, ar, ai)


def _ssm_y_kernel(u_ref, t_ref, h_ref, wc_ref, y_ref, scr_ref):
    tn = u_ref.shape[1]
    hb = h_ref[0].astype(BF16)
    for jo in range(N_TT):
        cols = slice(jo * MXU_TILE, (jo + 1) * MXU_TILE)
        acc = jnp.dot(hb, wc_ref[0, :, cols], preferred_element_type=F32)
        for ji in range(jo + 1):
            acc += jnp.dot(u_ref[0, :, ji * MXU_TILE:(ji + 1) * MXU_TILE], t_ref[0, jo - ji],
                           preferred_element_type=F32)
        yg = jax.nn.gelu(acc, approximate=True)
        for e in range(MXU_TILE // LANES):
            j = jo * (MXU_TILE // LANES) + e
            scr_ref[pl.ds(j, tn, stride=CHUNK), :] = yg[:, e * LANES:(e + 1) * LANES]
    y_ref[...] = scr_ref[...].astype(BF16)


def _ssm_y(uflat, tt, hprev, wc):
    nc = uflat.shape[1]
    tn = SSM_Y_ROWS
    return pl.pallas_call(
        _ssm_y_kernel,
        grid=(N_SG, nc // tn),
        in_specs=[pl.BlockSpec((1, tn, FLAT), lambda s, i: (s, i, 0)),
                  pl.BlockSpec((1, N_TT, MXU_TILE, MXU_TILE), lambda s, i: (s, 0, 0, 0)),
                  pl.BlockSpec((1, tn, 2 * SG_STATE), lambda s, i: (s, i, 0)),
                  pl.BlockSpec((1, 2 * SG_STATE, FLAT), lambda s, i: (s, 0, 0))],
        out_specs=pl.BlockSpec((tn * CHUNK, LANES), lambda s, i: (i, s)),
        out_shape=jax.ShapeDtypeStruct((nc * CHUNK, SSM_WIDTH), BF16),
        scratch_shapes=[pltpu.VMEM((tn * CHUNK, LANES), F32)],
        compiler_params=_cparams(2),
        name="ssm_y",
    )(uflat, tt, hprev, wc)


def _ssm_matrices(log_dt, a_re, a_im, b_re, b_im, c_re, c_im, d):
    g, n, c = SSM_GROUPS, SSM_STATE, SSM_GROUP
    dt = jnp.exp(log_dt.astype(F32))[:, None]
    a_re = a_re.astype(F32)
    a_im = a_im.astype(F32)
    ks = jnp.arange(CHUNK + 1, dtype=F32)
    mag = jnp.exp((a_re * dt)[..., None] * ks)
    ang = (a_im * dt)[..., None] * ks
    pw_re = mag * jnp.cos(ang)
    pw_im = mag * jnp.sin(ang)
    ab_re, ab_im = pw_re[..., 1], pw_im[..., 1]
    den = jnp.square(a_re) + jnp.square(a_im)
    nr = ab_re - 1.0
    f_re = (nr * a_re + ab_im * a_im) / den
    f_im = (ab_im * a_re - nr * a_im) / den
    b_re = b_re.astype(F32)
    b_im = b_im.astype(F32)
    bb_re = f_re[..., None] * b_re - f_im[..., None] * b_im
    bb_im = f_re[..., None] * b_im + f_im[..., None] * b_re
    c_re = c_re.astype(F32)
    c_im = c_im.astype(F32)
    ca_re = c_re[..., None] * pw_re[:, None] - c_im[..., None] * pw_im[:, None]
    ca_im = c_re[..., None] * pw_im[:, None] + c_im[..., None] * pw_re[:, None]

    kk = (jnp.einsum('gonk,gni->gkio', ca_re[..., :CHUNK], bb_re)
          - jnp.einsum('gonk,gni->gkio', ca_im[..., :CHUNK], bb_im))
    lag0_diag = (np.arange(CHUNK)[:, None, None] == 0) & np.eye(c, dtype=bool)[None]
    kk = kk + jnp.where(lag0_diag[None], d.astype(F32).reshape(g, 1, c, 1), 0.0)
    kc = jnp.transpose(kk.reshape(N_SG, SG_GROUPS, CHUNK, c, c), (0, 2, 1, 3, 4)).reshape(N_SG, CHUNK, LANES, c)

    rp_re = pw_re[..., CHUNK - 1::-1][..., :CHUNK]
    rp_im = pw_im[..., CHUNK - 1::-1][..., :CHUNK]
    v_re = jnp.einsum('gns,gni->gsin', rp_re, bb_re) - jnp.einsum('gns,gni->gsin', rp_im, bb_im)
    v_im = jnp.einsum('gns,gni->gsin', rp_re, bb_im) + jnp.einsum('gns,gni->gsin', rp_im, bb_re)

    def rows_p(v):
        return jnp.transpose(v.reshape(N_SG, SG_GROUPS, CHUNK, c, n), (0, 2, 1, 3, 4)).reshape(N_SG, FLAT, n)

    vc = jnp.stack([rows_p(v_re), rows_p(v_im)], axis=1)

    def rows_c(m):
        return jnp.transpose(m[..., 1:], (0, 2, 3, 1)).reshape(N_SG, SG_STATE, CHUNK * c)

    cm = jnp.stack([rows_c(ca_re), rows_c(-ca_im)], axis=1)

    ar = pw_re[..., CHUNK].reshape(N_SG, 1, SG_STATE)
    ai = pw_im[..., CHUNK].reshape(N_SG, 1, SG_STATE)
    return kc.astype(BF16), vc.astype(BF16), cm.astype(BF16), ar, ai


def _ssm_build_kernel(kc_ref, vc_ref, cm_ref, tt_ref, wp_ref, wc_ref):
    def iota(shape, dim):
        return lax.broadcasted_iota(jnp.int32, shape, dim)

    def spread(nrows):
        r, cidx = iota((nrows, nrows * SG_GROUPS), 0), iota((nrows, nrows * SG_GROUPS), 1)
        return jnp.where(((r >> 4) == (cidx >> 7)) & ((r & 15) == (cidx & 15)), 1.0, 0.0).astype(BF16)

    grp16 = lambda idx: (idx >> 4) & (SG_GROUPS - 1)
    grp64 = lambda idx: (idx >> 6) & (SG_GROUPS - 1)

    e_t = spread(SSM_GROUP)
    keep = grp16(iota((LANES, LANES), 0)) == grp16(iota((LANES, LANES), 1))
    lag_blocks = [jnp.where(keep, jnp.dot(kc_ref[0, k], e_t, preferred_element_type=F32), 0.0).astype(BF16)
                  for k in range(CHUNK)]
    for dlt in range(N_TT):
        for s in range(2):
            for t in range(2):
                k = 2 * dlt + t - s
                blk = lag_blocks[k] if k >= 0 else jnp.zeros((LANES, LANES), BF16)
                tt_ref[0, dlt, s * LANES:(s + 1) * LANES, t * LANES:(t + 1) * LANES] = blk

    r, cidx = iota((SSM_STATE, SG_STATE), 0), iota((SSM_STATE, SG_STATE), 1)
    e_p = jnp.where(r == (cidx & (SSM_STATE - 1)), 1.0, 0.0).astype(BF16)
    keep = grp16(iota((FLAT, SG_STATE), 0)) == grp64(iota((FLAT, SG_STATE), 1))
    for part in range(2):
        full = jnp.dot(vc_ref[0, part], e_p, preferred_element_type=F32)
        wp_ref[0, :, part * SG_STATE:(part + 1) * SG_STATE] = jnp.where(keep, full, 0.0).astype(BF16)

    e_c = spread(CHUNK * SSM_GROUP)
    keep = grp64(iota((SG_STATE, FLAT), 0)) == grp16(iota((SG_STATE, FLAT), 1))
    for part in range(2):
        full = jnp.dot(cm_ref[0, part], e_c, preferred_element_type=F32)
        wc_ref[0, part * SG_STATE:(part + 1) * SG_STATE, :] = jnp.where(keep, full, 0.0).astype(BF16)


def _ssm_build(kc, vc, cm):
    blk = lambda a: pl.BlockSpec((1,) + a.shape[1:], lambda s: (s,) + (0,) * (a.ndim - 1))
    outs = [jax.ShapeDtypeStruct((N_SG, N_TT, MXU_TILE, MXU_TILE), BF16),
            jax.ShapeDtypeStruct((N_SG, FLAT, 2 * SG_STATE), BF16),
            jax.ShapeDtypeStruct((N_SG, 2 * SG_STATE, FLAT), BF16)]
    return pl.pallas_call(
        _ssm_build_kernel,
        grid=(N_SG,),
        in_specs=[blk(kc), blk(vc), blk(cm)],
        out_specs=[blk(o) for o in outs],
        out_shape=outs,
        compiler_params=_cparams(1),
        name="ssm_build",
    )(kc, vc, cm)


ATT_UNITS_IN_FLIGHT = 4
ATT_STEP_QUERIES = 512


def _attn_kernel(q_ref, kp_ref, kc_ref, vp_ref, vc_ref, o_ref, lse_ref, ks_ref, vs_ref):
    nsb, qb = q_ref.shape[0], q_ref.shape[1]
    blk = ATT_SPAN
    first = pl.program_id(1) == 0

    lane = lax.broadcasted_iota(jnp.int32, (1, ATT_MERGED), 1)
    head_sel = (lane % LANES < ATT_HEAD_DIM, lane % LANES >= ATT_HEAD_DIM)
    for sq in range(nsb):
        for src_p, src_c, dst in ((kp_ref, kc_ref, ks_ref), (vp_ref, vc_ref, vs_ref)):
            for e in range(2):
                xp, xc = src_p[sq], src_c[sq]
                dst[sq, e, 0:blk] = jnp.where(head_sel[e], xp, jnp.zeros_like(xp))
                dst[sq, e, blk:blk + qb] = jnp.where(head_sel[e], xc, jnp.zeros_like(xc))

    qi = lax.broadcasted_iota(jnp.int32, (blk, 2 * blk), 0)
    kk = lax.broadcasted_iota(jnp.int32, (blk, 2 * blk), 1)
    band = jnp.logical_and(kk >= qi, kk <= qi + blk)
    band_first = jnp.logical_and(band, jnp.logical_or(kk >= blk, jnp.logical_not(first)))
    low = lax.broadcasted_iota(jnp.int32, (blk, LANES), 1) < ATT_HEAD_DIM
    dn = (((1,), (1,)), ((), ()))

    units = [(sq, j, pair) for sq in range(nsb) for j in range(qb // blk) for pair in range(ATT_MERGED // LANES)]
    for g0 in range(0, len(units), ATT_UNITS_IN_FLIGHT):
        group = units[g0:g0 + ATT_UNITS_IN_FLIGHT]
        scores = []
        for sq, j, pair in group:
            cols = slice(pair * LANES, (pair + 1) * LANES)
            keys = slice(j * blk, (j + 2) * blk)
            kcat = jnp.concatenate([ks_ref[sq, 0, keys, cols], ks_ref[sq, 1, keys, cols]], axis=0)
            s = lax.dot_general(q_ref[sq, j * blk:(j + 1) * blk, cols], kcat, dn,
                                preferred_element_type=F32)
            mask = band_first if j == 0 else band
            scores.append([jnp.where(mask, s[:, e * 2 * blk:(e + 1) * 2 * blk], NEG_INF) for e in range(2)])
        maxes = [[jnp.max(jnp.maximum(se[:, :blk], se[:, blk:]), axis=-1, keepdims=True) for se in su]
                 for su in scores]
        probs = [[jnp.exp2(se - me) for se, me in zip(su, mu)] for su, mu in zip(scores, maxes)]
        dens = [[jnp.sum(pe[:, :blk] + pe[:, blk:], axis=-1, keepdims=True) for pe in pu] for pu in probs]
        for (sq, j, pair), pu, mu, du in zip(group, probs, maxes, dens):
            cols = slice(pair * LANES, (pair + 1) * LANES)
            rows = slice(j * blk, (j + 1) * blk)
            keys = slice(j * blk, (j + 2) * blk)
            vcat = jnp.concatenate([vs_ref[sq, 0, keys, cols], vs_ref[sq, 1, keys, cols]], axis=0)
            pcat = jnp.concatenate([pu[0].astype(BF16), pu[1].astype(BF16)], axis=1)
            num = jnp.dot(pcat, vcat, preferred_element_type=F32)
            o_ref[sq, rows, cols] = (num * jnp.where(low, 1.0 / du[0], 1.0 / du[1])).astype(BF16)
            lse_ref[sq, rows, cols] = jnp.where(low, mu[0] * LN2 + jnp.log(du[0]), mu[1] * LN2 + jnp.log(du[1]))


def _banded_attention(q, k, v):
    nseq, length, w = q.shape
    blk = ATT_SPAN
    qb = min(length, ATT_STEP_QUERIES)
    nsb = ATT_STEP_QUERIES // qb
    per = qb // blk
    cur = pl.BlockSpec((nsb, qb, w), lambda s, n: (s, n, 0))
    prev = pl.BlockSpec((nsb, blk, w), lambda s, n: (s, jnp.maximum(n * per - 1, 0), 0))
    return pl.pallas_call(
        _attn_kernel,
        grid=(nseq // nsb, length // qb),
        in_specs=[cur, prev, cur, prev, cur],
        out_specs=[cur, cur],
        out_shape=[jax.ShapeDtypeStruct(q.shape, BF16), jax.ShapeDtypeStruct(q.shape, F32)],
        scratch_shapes=[pltpu.VMEM((nsb, 2, blk + qb, w), BF16)] * 2,
        compiler_params=_cparams(2),
        name="attn",
    )(q, k, k, v, v)


def _mix_kernel(y_ref, o0_ref, o1_ref, o2_ref, l0_ref, l1_ref, l2_ref, gs_ref, ga_ref, h_ref,
                wglu_ref, bglu_ref, wup_ref, wmix_ref, bmix_ref, g_ref, b_ref, out_ref, scr_ref, acc_ref):
    o_refs = (o0_ref, o1_ref, o2_ref)
    l_refs = (l0_ref, l1_ref, l2_ref)

    def tile(sub, rows):
        def natural(ref, dil):
            n = SUB_ROWS // dil
            if dil == 1:
                return ref[0, 0, rows, :].astype(F32)
            for r in range(dil):
                for e in range(ATT_MERGED // LANES):
                    scr_ref[sub, e, pl.ds(r, n, stride=dil), :] = (
                        ref[0, r, sub * n:(sub + 1) * n, e * LANES:(e + 1) * LANES].astype(F32))
            return jnp.concatenate([scr_ref[sub, e] for e in range(ATT_MERGED // LANES)], axis=-1)

        ls = [natural(l_refs[gi], dil) for gi, dil in enumerate(DILATIONS)]
        m = jnp.maximum(jnp.maximum(ls[0], ls[1]), ls[2])
        es = [jnp.exp(l - m) for l in ls]
        att = es[0] * natural(o_refs[0], DILATIONS[0])
        for gi in (1, 2):
            att += es[gi] * natural(o_refs[gi], DILATIONS[gi])
        att = (att / (es[0] + es[1] + es[2])).astype(BF16)
        y = y_ref[rows, :]
        yield

        for c in range(D_MODEL // MIX_CHUNK):
            cols = slice(c * MIX_CHUNK, (c + 1) * MIX_CHUNK)
            gate_cols = slice(D_MODEL + c * MIX_CHUNK, D_MODEL + (c + 1) * MIX_CHUNK)
            val = jnp.dot(y, wglu_ref[:, cols], preferred_element_type=F32) + bglu_ref[:, cols]
            gate = jnp.dot(y, wglu_ref[:, gate_cols], preferred_element_type=F32) + bglu_ref[:, gate_cols]
            b_att = jnp.dot(att, wup_ref[:, cols], preferred_element_type=F32)
            mixed = (gs_ref[rows, cols].astype(F32) * (val * _sigmoid(gate))
                     + ga_ref[rows, cols].astype(F32) * b_att)
            part = jnp.dot(mixed.astype(BF16), wmix_ref[cols, :], preferred_element_type=F32)
            if c == 0:
                acc_ref[rows, :] = part
            else:
                acc_ref[rows, :] += part
            yield

        r = acc_ref[rows, :] + bmix_ref[...]
        out_ref[rows, :] = _layer_norm(DEEPNORM_ALPHA * h_ref[rows, :] + r, g_ref[...], b_ref[...])

    _stagger([tile(i, rows) for i, rows in _sub_tiles(h_ref.shape[0])], lag=D_MODEL // MIX_CHUNK)


def _mix(yg, outs, lses, gs, ga, h, wglu, bglu, wup, wmix, bmix, g, b, seq):
    t = h.shape[0]
    tm = TOKEN_TILE
    per_b = seq // tm
    row = lambda n: pl.BlockSpec((tm, n), lambda i: (i, 0))
    full = lambda a: pl.BlockSpec(a.shape, lambda i: (0,) * a.ndim)
    grp = lambda d: pl.BlockSpec((1, d, tm // d, ATT_MERGED), lambda i: (i // per_b, 0, i % per_b, 0))
    consts = (wglu, bglu, wup, wmix, bmix, g, b)
    return pl.pallas_call(
        _mix_kernel,
        grid=(t // tm,),
        in_specs=([row(SSM_WIDTH)] + [grp(d) for d in DILATIONS] * 2 + [row(D_MODEL)] * 3
                  + [full(a) for a in consts]),
        out_specs=row(D_MODEL),
        out_shape=jax.ShapeDtypeStruct((t, D_MODEL), F32),
        scratch_shapes=[pltpu.VMEM((tm // SUB_ROWS, ATT_MERGED // LANES, SUB_ROWS, LANES), F32),
                        pltpu.VMEM((tm, D_MODEL), F32)],
        compiler_params=_cparams(1),
        name="mix",
    )(yg, *outs, *lses, gs, ga, h, *consts)


def _mem_kv_kernel(m_ref, w_ref, kv_ref):
    kv_ref[0] = jnp.dot(m_ref[0].astype(BF16), w_ref[...], preferred_element_type=F32).astype(BF16)


def _mem_kv(mem, w):
    bsz, ml, _ = mem.shape
    return pl.pallas_call(
        _mem_kv_kernel,
        grid=(bsz,),
        in_specs=[pl.BlockSpec((1, ml, D_MODEL), lambda b: (b, 0, 0)),
                  pl.BlockSpec(w.shape, lambda b: (0, 0))],
        out_specs=pl.BlockSpec((1, ml, 2 * D_MODEL), lambda b: (b, 0, 0)),
        out_shape=jax.ShapeDtypeStruct((bsz, ml, 2 * D_MODEL), BF16),
        compiler_params=_cparams(1),
        name="mem_kv",
    )(mem, w)


def _xattn_kernel(h_ref, kv_ref, wq_ref, wo_ref, g_ref, b_ref, out_ref):
    dn = (((1,), (1,)), ((), ()))

    def tile(sub, rows):
        h = h_ref[rows, :]
        hb = h.astype(BF16)
        yield
        q = (jnp.dot(hb, wq_ref[...], preferred_element_type=F32) * XATT_SCALE).astype(BF16)
        yield
        outs = []
        for hh in range(XATT_HEADS):
            sl = slice(hh * XATT_HEAD_DIM, (hh + 1) * XATT_HEAD_DIM)
            sv = slice(D_MODEL + hh * XATT_HEAD_DIM, D_MODEL + (hh + 1) * XATT_HEAD_DIM)
            s = lax.dot_general(q[:, sl], kv_ref[0, :, sl], dn, preferred_element_type=F32)
            e = jnp.exp(s - jnp.max(s, axis=-1, keepdims=True))
            p = e / jnp.sum(e, axis=-1, keepdims=True)
            outs.append(jnp.dot(p.astype(BF16), kv_ref[0, :, sv], preferred_element_type=F32).astype(BF16))
            yield
        xo = jnp.dot(jnp.concatenate(outs, axis=-1), wo_ref[...], preferred_element_type=F32)
        yield
        out_ref[rows, :] = _layer_norm(DEEPNORM_ALPHA * h + xo, g_ref[...], b_ref[...])

    _stagger([tile(i, rows) for i, rows in _sub_tiles(h_ref.shape[0])], lag=XATT_HEADS + 2)


def _xattn(h, kv, wq, wo, g, b, seq):
    t = h.shape[0]
    tm = TOKEN_TILE
    per_b = seq // tm
    row = pl.BlockSpec((tm, D_MODEL), lambda i: (i, 0))
    full = lambda a: pl.BlockSpec(a.shape, lambda i: (0,) * a.ndim)
    return pl.pallas_call(
        _xattn_kernel,
        grid=(t // tm,),
        in_specs=[row, pl.BlockSpec((1,) + kv.shape[1:], lambda i: (i // per_b, 0, 0)),
                  full(wq), full(wo), full(g), full(b)],
        out_specs=row,
        out_shape=jax.ShapeDtypeStruct((t, D_MODEL), F32),
        compiler_params=_cparams(1),
        name="xattn",
    )(h, kv, wq, wo, g, b)


def _ffn_kernel(h_ref, w1_ref, b1_ref, w2_ref, b2_ref, g_ref, b_ref, out_ref, acc_ref):
    def tile(sub, rows):
        h = h_ref[rows, :]
        hb = h.astype(BF16)
        yield
        for c in range(D_FF // FFN_CHUNK):
            cols = slice(c * FFN_CHUNK, (c + 1) * FFN_CHUNK)
            a = jnp.dot(hb, w1_ref[:, cols], preferred_element_type=F32) + b1_ref[:, cols]
            a = jnp.square(jnp.maximum(a, 0.0)).astype(BF16)
            part = jnp.dot(a, w2_ref[cols, :], preferred_element_type=F32)
            if c == 0:
                acc_ref[rows, :] = part
            else:
                acc_ref[rows, :] += part
            yield
        ff = acc_ref[rows, :] + b2_ref[...]
        out_ref[rows, :] = _layer_norm(DEEPNORM_ALPHA * h + ff, g_ref[...], b_ref[...])

    _stagger([tile(i, rows) for i, rows in _sub_tiles(h_ref.shape[0])], lag=D_FF // FFN_CHUNK)


def _ffn(h, w1, b1, w2, b2, g, b):
    t = h.shape[0]
    tm = TOKEN_TILE
    row = pl.BlockSpec((tm, D_MODEL), lambda i: (i, 0))
    full = lambda a: pl.BlockSpec(a.shape, lambda i: (0,) * a.ndim)
    consts = (w1, b1, w2, b2, g, b)
    return pl.pallas_call(
        _ffn_kernel,
        grid=(t // tm,),
        in_specs=[row] + [full(a) for a in consts],
        out_specs=row,
        out_shape=jax.ShapeDtypeStruct((t, D_MODEL), F32),
        scratch_shapes=[pltpu.VMEM((tm, D_MODEL), F32)],
        compiler_params=_cparams(1),
        name="ffn",
    )(h, *consts)


def _rope_selectors():
    half = ROT_DIM // 2
    sel = np.zeros((ROT_DIM, 3 * LANES), np.float32)
    one = np.zeros((1, LANES), np.float32)
    for lane in range(LANES):
        d = lane % ATT_HEAD_DIM
        if d >= ROT_DIM:
            one[0, lane] = 1.0
            continue
        sel[d % half, lane] = 1.0
        if d < half:
            sel[half + d, LANES + lane] = -1.0
        else:
            sel[d, 2 * LANES + lane] = 1.0
    return jnp.asarray(sel, BF16), jnp.asarray(one)


def _rope_cos_sin(positions):
    inv_freq = ROPE_THETA ** (-jnp.arange(0, ROT_DIM, 2, dtype=F32) / ROT_DIM)
    ang = positions.astype(F32).reshape(-1, 1) * inv_freq
    return jnp.concatenate([jnp.cos(ang), jnp.sin(ang)], axis=-1)


def kernel(x, mem, positions, ln_in_g, ln_in_b, w_in, b_in, ssm_log_dt, ssm_a_re, ssm_a_im, ssm_b_re, ssm_b_im, ssm_c_re, ssm_c_im, ssm_d, w_glu, b_glu, w_att_up, w_mix_out, b_mix_out, ln1_g, ln1_b, w_xq, w_xkv, w_xo, ln2_g, ln2_b, w_ff1, b_ff1, w_ff2, b_ff2, ln3_g, ln3_b):
    bsz, seq, _ = x.shape
    t = bsz * seq
    row2 = lambda a: a.reshape(1, -1).astype(F32)
    cs = _rope_cos_sin(positions)
    h = x.reshape(t, D_MODEL)
    for l in range(DEPTH):
        h, uflat, *qkv, gs, ga = _ln_proj(
            h, row2(ln_in_g), row2(ln_in_b), w_in[l].astype(BF16), row2(b_in[l]), cs, bsz, seq)

        kc, vc, cm, ar, ai = _ssm_matrices(ssm_log_dt[l], ssm_a_re[l], ssm_a_im[l], ssm_b_re[l],
                                           ssm_b_im[l], ssm_c_re[l], ssm_c_im[l], ssm_d[l])
        tt, wp, wc = _ssm_build(kc, vc, cm)
        p = _ssm_p(uflat, wp)
        hprev = _ssm_scan(p, ar, ai, bsz)
        yg = _ssm_y(uflat, tt, hprev, wc)

        outs, lses = [], []
        for gi, dil in enumerate(DILATIONS):
            qg, kg, vg = (a.reshape(bsz * dil, seq // dil, ATT_MERGED) for a in qkv[3 * gi:3 * gi + 3])
            o_g, lse_g = _banded_attention(qg, kg, vg)
            outs.append(o_g.reshape(bsz, dil, seq // dil, ATT_MERGED))
            lses.append(lse_g.reshape(bsz, dil, seq // dil, ATT_MERGED))

        h = _mix(yg, outs, lses, gs, ga, h, w_glu[l].astype(BF16), row2(b_glu[l]),
                 w_att_up[l].astype(BF16), w_mix_out[l].astype(BF16), row2(b_mix_out[l]),
                 row2(ln1_g[l]), row2(ln1_b[l]), seq)

        kv = _mem_kv(mem, w_xkv[l].astype(BF16))
        h = _xattn(h, kv, w_xq[l].astype(BF16), w_xo[l].astype(BF16), row2(ln2_g[l]), row2(ln2_b[l]), seq)
        h = _ffn(h, w_ff1[l].astype(BF16), row2(b_ff1[l]), w_ff2[l].astype(BF16), row2(b_ff2[l]),
                 row2(ln3_g[l]), row2(ln3_b[l]))
    return h.reshape(bsz, seq, D_MODEL)
```

```python
import functools
import math

import jax
import jax.numpy as jnp
import numpy as np
from jax import lax
from jax.experimental import pallas as pl
from jax.experimental.pallas import tpu as pltpu

F32 = jnp.float32
BF16 = jnp.bfloat16

D_MODEL = 1024
SSM_GROUP = 16
SSM_WIDTH = 768
SSM_GROUPS = SSM_WIDTH // SSM_GROUP
SSM_STATE = 64
ATT_HEAD_DIM = 64
ATT_HEADS_PER_GROUP = 4
DILATIONS = (1, 4, 16)
ATT_SPAN = 128
ATT_WIDTH = 768
ATT_MERGED = ATT_HEADS_PER_GROUP * ATT_HEAD_DIM
ATT_SCALE = ATT_HEAD_DIM ** -0.5
ROT_DIM = ATT_HEAD_DIM // 4
ROPE_THETA = 500000.0
XATT_HEADS = 4
XATT_HEAD_DIM = D_MODEL // XATT_HEADS
XATT_SCALE = XATT_HEAD_DIM ** -0.5
D_FF = 4 * D_MODEL
DEPTH = 1
DEEPNORM_ALPHA = (2 * DEPTH) ** 0.25
LN_EPS = 1e-5
NEG_INF = -1e30
LOG2E = math.log2(math.e)
LN2 = math.log(2.0)

LANES = 128
CHUNK = 16
SG_GROUPS = LANES // SSM_GROUP
N_SG = SSM_GROUPS // SG_GROUPS
FLAT = CHUNK * LANES
SG_STATE = SG_GROUPS * SSM_STATE
MXU_TILE = 256
N_TT = FLAT // MXU_TILE

SSM_P_ROWS = 1024
SSM_Y_ROWS = 512
SUB_ROWS = 256
TOKEN_TILE = 512
STAGE_LAG = 1
MIX_CHUNK = 256
FFN_CHUNK = 1024
QKV_OFF = SSM_WIDTH
GATE_OFF = SSM_WIDTH + 3 * ATT_WIDTH
VMEM_LIMIT = 56 * 1024 * 1024


def _cparams(n_axes):
    return pltpu.CompilerParams(dimension_semantics=("parallel",) * n_axes,
                                vmem_limit_bytes=VMEM_LIMIT)


def _layer_norm(x, g, b):
    mu = jnp.mean(x, axis=-1, keepdims=True)
    xc = x - mu
    var = jnp.mean(xc * xc, axis=-1, keepdims=True)
    return xc * lax.rsqrt(var + LN_EPS) * g + b


def _sigmoid(x):
    return 0.5 * jnp.tanh(0.5 * x) + 0.5


def _stagger(tiles, lag):
    tiles = list(tiles)
    live = [True] * len(tiles)
    rnd = 0
    while any(live):
        for i in reversed(range(len(tiles))):
            if live[i] and rnd >= i * lag:
                try:
                    next(tiles[i])
                except StopIteration:
                    live[i] = False
        rnd += 1


def _sub_tiles(tm):
    return [(i, slice(i * SUB_ROWS, (i + 1) * SUB_ROWS)) for i in range(tm // SUB_ROWS)]


def _ln_proj_kernel(x_ref, g_ref, b_ref, w_ref, bi_ref, cs_ref, sel_ref, one_ref,
                    h_ref, u_ref, q0_ref, k0_ref, v0_ref, q1_ref, k1_ref, v1_ref, q2_ref, k2_ref, v2_ref,
                    gs_ref, ga_ref, scr_ref):
    half = ROT_DIM // 2
    qkv_refs = ((q0_ref, k0_ref, v0_ref), (q1_ref, k1_ref, v1_ref), (q2_ref, k2_ref, v2_ref))
    sections = [(0, SSM_WIDTH), (QKV_OFF, ATT_WIDTH), (QKV_OFF + ATT_WIDTH, ATT_WIDTH),
                (QKV_OFF + 2 * ATT_WIDTH, ATT_WIDTH), (GATE_OFF, D_MODEL), (GATE_OFF + D_MODEL, D_MODEL)]

    def tile(sub, rows):
        h = _layer_norm(x_ref[rows, :], g_ref[...], b_ref[...])
        h_ref[rows, :] = h
        hb = h.astype(BF16)
        cs = cs_ref[rows, :]
        cs_hi = cs.astype(BF16)
        cs_lo = (cs - cs_hi.astype(F32)).astype(BF16)
        tab = (jnp.dot(cs_hi, sel_ref[...], preferred_element_type=F32)
               + jnp.dot(cs_lo, sel_ref[...], preferred_element_type=F32))
        cc = tab[:, 0:LANES] + one_ref[...]
        s1 = tab[:, LANES:2 * LANES]
        s2 = tab[:, 2 * LANES:3 * LANES]
        yield

        def stage(off, val):
            for i in range(val.shape[1] // LANES):
                scr_ref[sub, off // LANES + i] = val[:, i * LANES:(i + 1) * LANES]

        def rope(val, scale):
            out = []
            for i in range(ATT_WIDTH // LANES):
                t = val[:, i * LANES:(i + 1) * LANES]
                out.append((t * cc + pltpu.roll(t, LANES - half, 1) * s1 + pltpu.roll(t, half, 1) * s2) * scale)
            return jnp.concatenate(out, axis=-1)

        def emit_groups(a):
            for gi, dil in enumerate(DILATIONS):
                ref = qkv_refs[gi][a]
                cb0 = (QKV_OFF + a * ATT_WIDTH + gi * ATT_MERGED) // LANES
                n = SUB_ROWS // dil
                for r in range(dil):
                    for e in range(ATT_MERGED // LANES):
                        if dil == 1:
                            blk = scr_ref[sub, cb0 + e]
                        else:
                            blk = scr_ref[sub, cb0 + e, pl.ds(r, n, stride=dil), :]
                        ref[0, r, sub * n:(sub + 1) * n, e * LANES:(e + 1) * LANES] = blk.astype(BF16)

        for c, (off, width) in enumerate(sections):
            val = (jnp.dot(hb, w_ref[:, off:off + width], preferred_element_type=F32)
                   + bi_ref[:, off:off + width])
            if c == 0:
                stage(off, val)
                n = SUB_ROWS // CHUNK
                for j in range(CHUNK):
                    for sg in range(N_SG):
                        blk = scr_ref[sub, sg, pl.ds(j, n, stride=CHUNK), :]
                        u_ref[sg, sub * n:(sub + 1) * n, j * LANES:(j + 1) * LANES] = blk.astype(BF16)
            elif c in (1, 2):
                stage(off, rope(val, ATT_SCALE * LOG2E if c == 1 else 1.0))
                emit_groups(c - 1)
            elif c == 3:
                stage(off, val)
                emit_groups(2)
            else:
                (gs_ref if c == 4 else ga_ref)[rows, :] = _sigmoid(val).astype(BF16)
            yield

    _stagger([tile(i, rows) for i, rows in _sub_tiles(x_ref.shape[0])], lag=len(sections) - 1)


def _ln_proj(x2, g, b, w, bi, cs, bsz, seq):
    t = x2.shape[0]
    tm = TOKEN_TILE
    per_b = seq // tm
    sel, one = _rope_selectors()
    row = lambda n: pl.BlockSpec((tm, n), lambda i: (i, 0))
    full = lambda a: pl.BlockSpec(a.shape, lambda i: (0,) * a.ndim)
    grp = lambda d: pl.BlockSpec((1, d, tm // d, ATT_MERGED), lambda i: (i // per_b, 0, i % per_b, 0))
    outs = [jax.ShapeDtypeStruct((t, D_MODEL), F32),
            jax.ShapeDtypeStruct((N_SG, t // CHUNK, FLAT), BF16)]
    out_specs = [row(D_MODEL), pl.BlockSpec((N_SG, tm // CHUNK, FLAT), lambda i: (0, i, 0))]
    for d in DILATIONS:
        outs += [jax.ShapeDtypeStruct((bsz, d, seq // d, ATT_MERGED), BF16)] * 3
        out_specs += [grp(d)] * 3
    outs += [jax.ShapeDtypeStruct((t, D_MODEL), BF16)] * 2
    out_specs += [row(D_MODEL)] * 2
    return pl.pallas_call(
        _ln_proj_kernel,
        grid=(t // tm,),
        in_specs=[row(D_MODEL), full(g), full(b), full(w), full(bi), row(ROT_DIM), full(sel), full(one)],
        out_specs=out_specs,
        out_shape=outs,
        scratch_shapes=[pltpu.VMEM((tm // SUB_ROWS, GATE_OFF // LANES, SUB_ROWS, LANES), F32)],
        compiler_params=_cparams(1),
        name="ln_proj",
    )(x2, g, b, w, bi, cs, sel, one)


def _ssm_p_kernel(u_ref, wp_ref, p_ref):
    p_ref[0] = jnp.dot(u_ref[0], wp_ref[0], preferred_element_type=F32)


def _ssm_p(uflat, wp):
    nc = uflat.shape[1]
    tn = SSM_P_ROWS
    return pl.pallas_call(
        _ssm_p_kernel,
        grid=(N_SG, nc // tn),
        in_specs=[pl.BlockSpec((1, tn, FLAT), lambda s, i: (s, i, 0)),
                  pl.BlockSpec((1, FLAT, 2 * SG_STATE), lambda s, i: (s, 0, 0))],
        out_specs=pl.BlockSpec((1, tn, 2 * SG_STATE), lambda s, i: (s, i, 0)),
        out_shape=jax.ShapeDtypeStruct((N_SG, nc, 2 * SG_STATE), F32),
        compiler_params=_cparams(2),
        name="ssm_p",
    )(uflat, wp)


def _ssm_scan_kernel(p_ref, ar_ref, ai_ref, h_ref, *, steps, nseq):
    ar = ar_ref[0]
    ai = ai_ref[0]

    def body(c, carry):
        new = []
        for b in range(nseq):
            hr, hi = carry[b]
            row = b * steps + c
            h_ref[0, pl.ds(row, 1), 0:SG_STATE] = hr
            h_ref[0, pl.ds(row, 1), SG_STATE:2 * SG_STATE] = hi
            pr = p_ref[0, pl.ds(row, 1), 0:SG_STATE]
            pi = p_ref[0, pl.ds(row, 1), SG_STATE:2 * SG_STATE]
            new.append((ar * hr - ai * hi + pr, ar * hi + ai * hr + pi))
        return tuple(new)

    z = jnp.zeros((1, SG_STATE), F32)
    lax.fori_loop(0, steps, body, ((z, z),) * nseq, unroll=2)


def _ssm_scan(p, ar, ai, bsz):
    nc = p.shape[1]
    return pl.pallas_call(
        functools.partial(_ssm_scan_kernel, steps=nc // bsz, nseq=bsz),
        grid=(N_SG,),
        in_specs=[pl.BlockSpec((1, nc, 2 * SG_STATE), lambda s: (s, 0, 0)),
                  pl.BlockSpec((1, 1, SG_STATE), lambda s: (s, 0, 0)),
                  pl.BlockSpec((1, 1, SG_STATE), lambda s: (s, 0, 0))],
        out_specs=pl.BlockSpec((1, nc, 2 * SG_STATE), lambda s: (s, 0, 0)),
        out_shape=jax.ShapeDtypeStruct(p.shape, F32),
        compiler_params=_cparams(1),
        name="ssm_scan",
    )(p, ar, ai)


def _ssm_y_kernel(u_ref, t_ref, h_ref, wc_ref, y_ref, scr_ref):
    tn = u_ref.shape[1]
    hb = h_ref[0].astype(BF16)
    for jo in range(N_TT):
        cols = slice(jo * MXU_TILE, (jo + 1) * MXU_TILE)
        acc = jnp.dot(hb, wc_ref[0, :, cols], preferred_element_type=F32)
        for ji in range(jo + 1):
            acc += jnp.dot(u_ref[0, :, ji * MXU_TILE:(ji + 1) * MXU_TILE], t_ref[0, jo - ji],
                           preferred_element_type=F32)
        yg = jax.nn.gelu(acc, approximate=True)
        for e in range(MXU_TILE // LANES):
            j = jo * (MXU_TILE // LANES) + e
            scr_ref[pl.ds(j, tn, stride=CHUNK), :] = yg[:, e * LANES:(e + 1) * LANES]
    y_ref[...] = scr_ref[...].astype(BF16)


def _ssm_y(uflat, tt, hprev, wc):
    nc = uflat.shape[1]
    tn = SSM_Y_ROWS
    return pl.pallas_call(
        _ssm_y_kernel,
        grid=(N_SG, nc // tn),
        in_specs=[pl.BlockSpec((1, tn, FLAT), lambda s, i: (s, i, 0)),
                  pl.BlockSpec((1, N_TT, MXU_TILE, MXU_TILE), lambda s, i: (s, 0, 0, 0)),
                  pl.BlockSpec((1, tn, 2 * SG_STATE), lambda s, i: (s, i, 0)),
                  pl.BlockSpec((1, 2 * SG_STATE, FLAT), lambda s, i: (s, 0, 0))],
        out_specs=pl.BlockSpec((tn * CHUNK, LANES), lambda s, i: (i, s)),
        out_shape=jax.ShapeDtypeStruct((nc * CHUNK, SSM_WIDTH), BF16),
        scratch_shapes=[pltpu.VMEM((tn * CHUNK, LANES), F32)],
        compiler_params=_cparams(2),
        name="ssm_y",
    )(uflat, tt, hprev, wc)


def _ssm_matrices(log_dt, a_re, a_im, b_re, b_im, c_re, c_im, d):
    g, n, c = SSM_GROUPS, SSM_STATE, SSM_GROUP
    dt = jnp.exp(log_dt.astype(F32))[:, None]
    a_re = a_re.astype(F32)
    a_im = a_im.astype(F32)
    ks = jnp.arange(CHUNK + 1, dtype=F32)
    mag = jnp.exp((a_re * dt)[..., None] * ks)
    ang = (a_im * dt)[..., None] * ks
    pw_re = mag * jnp.cos(ang)
    pw_im = mag * jnp.sin(ang)
    ab_re, ab_im = pw_re[..., 1], pw_im[..., 1]
    den = jnp.square(a_re) + jnp.square(a_im)
    nr = ab_re - 1.0
    f_re = (nr * a_re + ab_im * a_im) / den
    f_im = (ab_im * a_re - nr * a_im) / den
    b_re = b_re.astype(F32)
    b_im = b_im.astype(F32)
    bb_re = f_re[..., None] * b_re - f_im[..., None] * b_im
    bb_im = f_re[..., None] * b_im + f_im[..., None] * b_re
    c_re = c_re.astype(F32)
    c_im = c_im.astype(F32)
    ca_re = c_re[..., None] * pw_re[:, None] - c_im[..., None] * pw_im[:, None]
    ca_im = c_re[..., None] * pw_im[:, None] + c_im[..., None] * pw_re[:, None]

    kk = (jnp.einsum('gonk,gni->gkio', ca_re[..., :CHUNK], bb_re)
          - jnp.einsum('gonk,gni->gkio', ca_im[..., :CHUNK], bb_im))
    lag0_diag = (np.arange(CHUNK)[:, None, None] == 0) & np.eye(c, dtype=bool)[None]
    kk = kk + jnp.where(lag0_diag[None], d.astype(F32).reshape(g, 1, c, 1), 0.0)
    kc = jnp.transpose(kk.reshape(N_SG, SG_GROUPS, CHUNK, c, c), (0, 2, 1, 3, 4)).reshape(N_SG, CHUNK, LANES, c)

    rp_re = pw_re[..., CHUNK - 1::-1][..., :CHUNK]
    rp_im = pw_im[..., CHUNK - 1::-1][..., :CHUNK]
    v_re = jnp.einsum('gns,gni->gsin', rp_re, bb_re) - jnp.einsum('gns,gni->gsin', rp_im, bb_im)
    v_im = jnp.einsum('gns,gni->gsin', rp_re, bb_im) + jnp.einsum('gns,gni->gsin', rp_im, bb_re)

    def rows_p(v):
        return jnp.transpose(v.reshape(N_SG, SG_GROUPS, CHUNK, c, n), (0, 2, 1, 3, 4)).reshape(N_SG, FLAT, n)

    vc = jnp.stack([rows_p(v_re), rows_p(v_im)], axis=1)

    def rows_c(m):
        return jnp.transpose(m[..., 1:], (0, 2, 3, 1)).reshape(N_SG, SG_STATE, CHUNK * c)

    cm = jnp.stack([rows_c(ca_re), rows_c(-ca_im)], axis=1)

    ar = pw_re[..., CHUNK].reshape(N_SG, 1, SG_STATE)
    ai = pw_im[..., CHUNK].reshape(N_SG, 1, SG_STATE)
    return kc.astype(BF16), vc.astype(BF16), cm.astype(BF16), ar, ai


def _ssm_build_kernel(kc_ref, vc_ref, cm_ref, tt_ref, wp_ref, wc_ref):
    def iota(shape, dim):
        return lax.broadcasted_iota(jnp.int32, shape, dim)

    def spread(nrows):
        r, cidx = iota((nrows, nrows * SG_GROUPS), 0), iota((nrows, nrows * SG_GROUPS), 1)
        return jnp.where(((r >> 4) == (cidx >> 7)) & ((r & 15) == (cidx & 15)), 1.0, 0.0).astype(BF16)

    grp16 = lambda idx: (idx >> 4) & (SG_GROUPS - 1)
    grp64 = lambda idx: (idx >> 6) & (SG_GROUPS - 1)

    e_t = spread(SSM_GROUP)
    keep = grp16(iota((LANES, LANES), 0)) == grp16(iota((LANES, LANES), 1))
    lag_blocks = [jnp.where(keep, jnp.dot(kc_ref[0, k], e_t, preferred_element_type=F32), 0.0).astype(BF16)
                  for k in range(CHUNK)]
    for dlt in range(N_TT):
        for s in range(2):
            for t in range(2):
                k = 2 * dlt + t - s
                blk = lag_blocks[k] if k >= 0 else jnp.zeros((LANES, LANES), BF16)
                tt_ref[0, dlt, s * LANES:(s + 1) * LANES, t * LANES:(t + 1) * LANES] = blk

    r, cidx = iota((SSM_STATE, SG_STATE), 0), iota((SSM_STATE, SG_STATE), 1)
    e_p = jnp.where(r == (cidx & (SSM_STATE - 1)), 1.0, 0.0).astype(BF16)
    keep = grp16(iota((FLAT, SG_STATE), 0)) == grp64(iota((FLAT, SG_STATE), 1))
    for part in range(2):
        full = jnp.dot(vc_ref[0, part], e_p, preferred_element_type=F32)
        wp_ref[0, :, part * SG_STATE:(part + 1) * SG_STATE] = jnp.where(keep, full, 0.0).astype(BF16)

    e_c = spread(CHUNK * SSM_GROUP)
    keep = grp64(iota((SG_STATE, FLAT), 0)) == grp16(iota((SG_STATE, FLAT), 1))
    for part in range(2):
        full = jnp.dot(cm_ref[0, part], e_c, preferred_element_type=F32)
        wc_ref[0, part * SG_STATE:(part + 1) * SG_STATE, :] = jnp.where(keep, full, 0.0).astype(BF16)


def _ssm_build(kc, vc, cm):
    blk = lambda a: pl.BlockSpec((1,) + a.shape[1:], lambda s: (s,) + (0,) * (a.ndim - 1))
    outs = [jax.ShapeDtypeStruct((N_SG, N_TT, MXU_TILE, MXU_TILE), BF16),
            jax.ShapeDtypeStruct((N_SG, FLAT, 2 * SG_STATE), BF16),
            jax.ShapeDtypeStruct((N_SG, 2 * SG_STATE, FLAT), BF16)]
    return pl.pallas_call(
        _ssm_build_kernel,
        grid=(N_SG,),
        in_specs=[blk(kc), blk(vc), blk(cm)],
        out_specs=[blk(o) for o in outs],
        out_shape=outs,
        compiler_params=_cparams(1),
        name="ssm_build",
    )(kc, vc, cm)


ATT_UNITS_IN_FLIGHT = 4
ATT_STEP_QUERIES = 512


def _attn_kernel(q_ref, kp_ref, kc_ref, vp_ref, vc_ref, o_ref, lse_ref, ks_ref, vs_ref):
    nsb, qb = q_ref.shape[0], q_ref.shape[1]
    blk = ATT_SPAN
    first = pl.program_id(1) == 0

    lane = lax.broadcasted_iota(jnp.int32, (1, ATT_MERGED), 1)
    head_sel = (lane % LANES < ATT_HEAD_DIM, lane % LANES >= ATT_HEAD_DIM)
    for sq in range(nsb):
        for src_p, src_c, dst in ((kp_ref, kc_ref, ks_ref), (vp_ref, vc_ref, vs_ref)):
            for e in range(2):
                xp, xc = src_p[sq], src_c[sq]
                dst[sq, e, 0:blk] = jnp.where(head_sel[e], xp, jnp.zeros_like(xp))
                dst[sq, e, blk:blk + qb] = jnp.where(head_sel[e], xc, jnp.zeros_like(xc))

    qi = lax.broadcasted_iota(jnp.int32, (blk, 2 * blk), 0)
    kk = lax.broadcasted_iota(jnp.int32, (blk, 2 * blk), 1)
    band = jnp.logical_and(kk >= qi, kk <= qi + blk)
    band_first = jnp.logical_and(band, jnp.logical_or(kk >= blk, jnp.logical_not(first)))
    low = lax.broadcasted_iota(jnp.int32, (blk, LANES), 1) < ATT_HEAD_DIM
    dn = (((1,), (1,)), ((), ()))

    units = [(sq, j, pair) for sq in range(nsb) for j in range(qb // blk) for pair in range(ATT_MERGED // LANES)]
    for g0 in range(0, len(units), ATT_UNITS_IN_FLIGHT):
        group = units[g0:g0 + ATT_UNITS_IN_FLIGHT]
        scores = []
        for sq, j, pair in group:
            cols = slice(pair * LANES, (pair + 1) * LANES)
            keys = slice(j * blk, (j + 2) * blk)
            kcat = jnp.concatenate([ks_ref[sq, 0, keys, cols], ks_ref[sq, 1, keys, cols]], axis=0)
            s = lax.dot_general(q_ref[sq, j * blk:(j + 1) * blk, cols], kcat, dn,
                                preferred_element_type=F32)
            mask = band_first if j == 0 else band
            scores.append([jnp.where(mask, s[:, e * 2 * blk:(e + 1) * 2 * blk], NEG_INF) for e in range(2)])
        maxes = [[jnp.max(jnp.maximum(se[:, :blk], se[:, blk:]), axis=-1, keepdims=True) for se in su]
                 for su in scores]
        probs = [[jnp.exp2(se - me) for se, me in zip(su, mu)] for su, mu in zip(scores, maxes)]
        dens = [[jnp.sum(pe[:, :blk] + pe[:, blk:], axis=-1, keepdims=True) for pe in pu] for pu in probs]
        for (sq, j, pair), pu, mu, du in zip(group, probs, maxes, dens):
            cols = slice(pair * LANES, (pair + 1) * LANES)
            rows = slice(j * blk, (j + 1) * blk)
            keys = slice(j * blk, (j + 2) * blk)
            vcat = jnp.concatenate([vs_ref[sq, 0, keys, cols], vs_ref[sq, 1, keys, cols]], axis=0)
            pcat = jnp.concatenate([pu[0].astype(BF16), pu[1].astype(BF16)], axis=1)
            num = jnp.dot(pcat, vcat, preferred_element_type=F32)
            o_ref[sq, rows, cols] = (num * jnp.where(low, 1.0 / du[0], 1.0 / du[1])).astype(BF16)
            lse_ref[sq, rows, cols] = jnp.where(low, mu[0] * LN2 + jnp.log(du[0]), mu[1] * LN2 + jnp.log(du[1]))


def _banded_attention(q, k, v):
    nseq, length, w = q.shape
    blk = ATT_SPAN
    qb = min(length, ATT_STEP_QUERIES)
    nsb = ATT_STEP_QUERIES // qb
    per = qb // blk
    cur = pl.BlockSpec((nsb, qb, w), lambda s, n: (s, n, 0))
    prev = pl.BlockSpec((nsb, blk, w), lambda s, n: (s, jnp.maximum(n * per - 1, 0), 0))
    return pl.pallas_call(
        _attn_kernel,
        grid=(nseq // nsb, length // qb),
        in_specs=[cur, prev, cur, prev, cur],
        out_specs=[cur, cur],
        out_shape=[jax.ShapeDtypeStruct(q.shape, BF16), jax.ShapeDtypeStruct(q.shape, F32)],
        scratch_shapes=[pltpu.VMEM((nsb, 2, blk + qb, w), BF16)] * 2,
        compiler_params=_cparams(2),
        name="attn",
    )(q, k, k, v, v)


def _mix_kernel(y_ref, o0_ref, o1_ref, o2_ref, l0_ref, l1_ref, l2_ref, gs_ref, ga_ref, h_ref,
                wglu_ref, bglu_ref, wup_ref, wmix_ref, bmix_ref, g_ref, b_ref, out_ref, scr_ref, acc_ref):
    o_refs = (o0_ref, o1_ref, o2_ref)
    l_refs = (l0_ref, l1_ref, l2_ref)

    def tile(sub, rows):
        def natural(ref, dil):
            n = SUB_ROWS // dil
            if dil == 1:
                return ref[0, 0, rows, :].astype(F32)
            for r in range(dil):
                for e in range(ATT_MERGED // LANES):
                    scr_ref[sub, e, pl.ds(r, n, stride=dil), :] = (
                        ref[0, r, sub * n:(sub + 1) * n, e * LANES:(e + 1) * LANES].astype(F32))
            return jnp.concatenate([scr_ref[sub, e] for e in range(ATT_MERGED // LANES)], axis=-1)

        ls = [natural(l_refs[gi], dil) for gi, dil in enumerate(DILATIONS)]
        m = jnp.maximum(jnp.maximum(ls[0], ls[1]), ls[2])
        es = [jnp.exp(l - m) for l in ls]
        att = es[0] * natural(o_refs[0], DILATIONS[0])
        for gi in (1, 2):
            att += es[gi] * natural(o_refs[gi], DILATIONS[gi])
        att = (att / (es[0] + es[1] + es[2])).astype(BF16)
        y = y_ref[rows, :]
        yield

        for c in range(D_MODEL // MIX_CHUNK):
            cols = slice(c * MIX_CHUNK, (c + 1) * MIX_CHUNK)
            gate_cols = slice(D_MODEL + c * MIX_CHUNK, D_MODEL + (c + 1) * MIX_CHUNK)
            val = jnp.dot(y, wglu_ref[:, cols], preferred_element_type=F32) + bglu_ref[:, cols]
            gate = jnp.dot(y, wglu_ref[:, gate_cols], preferred_element_type=F32) + bglu_ref[:, gate_cols]
            b_att = jnp.dot(att, wup_ref[:, cols], preferred_element_type=F32)
            mixed = (gs_ref[rows, cols].astype(F32) * (val * _sigmoid(gate))
                     + ga_ref[rows, cols].astype(F32) * b_att)
            acc_ref[rows, cols] = mixed.astype(BF16)
            yield

        r = jnp.dot(acc_ref[rows, :], wmix_ref[...], preferred_element_type=F32) + bmix_ref[...]
        yield
        out_ref[rows, :] = _layer_norm(DEEPNORM_ALPHA * h_ref[rows, :] + r, g_ref[...], b_ref[...])

    _stagger([tile(i, rows) for i, rows in _sub_tiles(h_ref.shape[0])], lag=STAGE_LAG)


def _mix(yg, outs, lses, gs, ga, h, wglu, bglu, wup, wmix, bmix, g, b, seq):
    t = h.shape[0]
    tm = 2 * TOKEN_TILE
    per_b = seq // tm
    row = lambda n: pl.BlockSpec((tm, n), lambda i: (i, 0))
    full = lambda a: pl.BlockSpec(a.shape, lambda i: (0,) * a.ndim, pipeline_mode=pl.Buffered(1))
    grp = lambda d: pl.BlockSpec((1, d, tm // d, ATT_MERGED), lambda i: (i // per_b, 0, i % per_b, 0))
    consts = (wglu, bglu, wup, wmix, bmix, g, b)
    return pl.pallas_call(
        _mix_kernel,
        grid=(t // tm,),
        in_specs=([row(SSM_WIDTH)] + [grp(d) for d in DILATIONS] * 2 + [row(D_MODEL)] * 3
                  + [full(a) for a in consts]),
        out_specs=row(D_MODEL),
        out_shape=jax.ShapeDtypeStruct((t, D_MODEL), F32),
        scratch_shapes=[pltpu.VMEM((tm // SUB_ROWS, ATT_MERGED // LANES, SUB_ROWS, LANES), F32),
                        pltpu.VMEM((tm, D_MODEL), BF16)],
        compiler_params=_cparams(1),
        name="mix",
    )(yg, *outs, *lses, gs, ga, h, *consts)


def _mem_kv_kernel(m_ref, w_ref, kv_ref):
    kv_ref[0] = jnp.dot(m_ref[0].astype(BF16), w_ref[...], preferred_element_type=F32).astype(BF16)


def _mem_kv(mem, w):
    bsz, ml, _ = mem.shape
    return pl.pallas_call(
        _mem_kv_kernel,
        grid=(bsz,),
        in_specs=[pl.BlockSpec((1, ml, D_MODEL), lambda b: (b, 0, 0)),
                  pl.BlockSpec(w.shape, lambda b: (0, 0))],
        out_specs=pl.BlockSpec((1, ml, 2 * D_MODEL), lambda b: (b, 0, 0)),
        out_shape=jax.ShapeDtypeStruct((bsz, ml, 2 * D_MODEL), BF16),
        compiler_params=_cparams(1),
        name="mem_kv",
    )(mem, w)


def _xattn_kernel(h_ref, kv_ref, wq_ref, wo_ref, g_ref, b_ref, out_ref):
    dn = (((1,), (1,)), ((), ()))

    def tile(sub, rows):
        h = h_ref[rows, :]
        hb = h.astype(BF16)
        yield
        q = (jnp.dot(hb, wq_ref[...], preferred_element_type=F32) * XATT_SCALE).astype(BF16)
        yield
        outs = []
        for hh in range(XATT_HEADS):
            sl = slice(hh * XATT_HEAD_DIM, (hh + 1) * XATT_HEAD_DIM)
            sv = slice(D_MODEL + hh * XATT_HEAD_DIM, D_MODEL + (hh + 1) * XATT_HEAD_DIM)
            s = lax.dot_general(q[:, sl], kv_ref[0, :, sl], dn, preferred_element_type=F32)
            e = jnp.exp(s - jnp.max(s, axis=-1, keepdims=True))
            p = e / jnp.sum(e, axis=-1, keepdims=True)
            outs.append(jnp.dot(p.astype(BF16), kv_ref[0, :, sv], preferred_element_type=F32).astype(BF16))
            yield
        xo = jnp.dot(jnp.concatenate(outs, axis=-1), wo_ref[...], preferred_element_type=F32)
        yield
        out_ref[rows, :] = _layer_norm(DEEPNORM_ALPHA * h + xo, g_ref[...], b_ref[...])

    _stagger([tile(i, rows) for i, rows in _sub_tiles(h_ref.shape[0])], lag=STAGE_LAG)


def _xattn(h, kv, wq, wo, g, b, seq):
    t = h.shape[0]
    tm = 2 * TOKEN_TILE
    per_b = seq // tm
    row = pl.BlockSpec((tm, D_MODEL), lambda i: (i, 0))
    full = lambda a: pl.BlockSpec(a.shape, lambda i: (0,) * a.ndim)
    return pl.pallas_call(
        _xattn_kernel,
        grid=(t // tm,),
        in_specs=[row, pl.BlockSpec((1,) + kv.shape[1:], lambda i: (i // per_b, 0, 0)),
                  full(wq), full(wo), full(g), full(b)],
        out_specs=row,
        out_shape=jax.ShapeDtypeStruct((t, D_MODEL), F32),
        compiler_params=_cparams(1),
        name="xattn",
    )(h, kv, wq, wo, g, b)


def _ffn_kernel(h_ref, w1_ref, b1_ref, w2_ref, b2_ref, g_ref, b_ref, out_ref, acc_ref):
    def tile(sub, rows):
        h = h_ref[rows, :]
        hb = h.astype(BF16)
        yield
        for c in range(D_FF // FFN_CHUNK):
            cols = slice(c * FFN_CHUNK, (c + 1) * FFN_CHUNK)
            a = jnp.dot(hb, w1_ref[:, cols], preferred_element_type=F32) + b1_ref[:, cols]
            a = jnp.square(jnp.maximum(a, 0.0)).astype(BF16)
            part = jnp.dot(a, w2_ref[cols, :], preferred_element_type=F32)
            if c == 0:
                acc_ref[rows, :] = part
            else:
                acc_ref[rows, :] += part
            yield
        ff = acc_ref[rows, :] + b2_ref[...]
        out_ref[rows, :] = _layer_norm(DEEPNORM_ALPHA * h + ff, g_ref[...], b_ref[...])

    _stagger([tile(i, rows) for i, rows in _sub_tiles(h_ref.shape[0])], lag=STAGE_LAG)


def _ffn(h, w1, b1, w2, b2, g, b):
    t = h.shape[0]
    tm = 2 * TOKEN_TILE
    row = pl.BlockSpec((tm, D_MODEL), lambda i: (i, 0))
    full = lambda a: pl.BlockSpec(a.shape, lambda i: (0,) * a.ndim, pipeline_mode=pl.Buffered(1))
    consts = (w1, b1, w2, b2, g, b)
    return pl.pallas_call(
        _ffn_kernel,
        grid=(t // tm,),
        in_specs=[row] + [full(a) for a in consts],
        out_specs=row,
        out_shape=jax.ShapeDtypeStruct((t, D_MODEL), F32),
        scratch_shapes=[pltpu.VMEM((tm, D_MODEL), F32)],
        compiler_params=_cparams(1),
        name="ffn",
    )(h, *consts)


def _rope_selectors():
    half = ROT_DIM // 2
    sel = np.zeros((ROT_DIM, 3 * LANES), np.float32)
    one = np.zeros((1, LANES), np.float32)
    for lane in range(LANES):
        d = lane % ATT_HEAD_DIM
        if d >= ROT_DIM:
            one[0, lane] = 1.0
            continue
        sel[d % half, lane] = 1.0
        if d < half:
            sel[half + d, LANES + lane] = -1.0
        else:
            sel[d, 2 * LANES + lane] = 1.0
    return jnp.asarray(sel, BF16), jnp.asarray(one)


def _rope_cos_sin(positions):
    inv_freq = ROPE_THETA ** (-jnp.arange(0, ROT_DIM, 2, dtype=F32) / ROT_DIM)
    ang = positions.astype(F32).reshape(-1, 1) * inv_freq
    return jnp.concatenate([jnp.cos(ang), jnp.sin(ang)], axis=-1)


def kernel(x, mem, positions, ln_in_g, ln_in_b, w_in, b_in, ssm_log_dt, ssm_a_re, ssm_a_im, ssm_b_re, ssm_b_im, ssm_c_re, ssm_c_im, ssm_d, w_glu, b_glu, w_att_up, w_mix_out, b_mix_out, ln1_g, ln1_b, w_xq, w_xkv, w_xo, ln2_g, ln2_b, w_ff1, b_ff1, w_ff2, b_ff2, ln3_g, ln3_b):
    bsz, seq, _ = x.shape
    t = bsz * seq
    row2 = lambda a: a.reshape(1, -1).astype(F32)
    cs = _rope_cos_sin(positions)
    h = x.reshape(t, D_MODEL)
    for l in range(DEPTH):
        h, uflat, *qkv, gs, ga = _ln_proj(
            h, row2(ln_in_g), row2(ln_in_b), w_in[l].astype(BF16), row2(b_in[l]), cs, bsz, seq)

        kc, vc, cm, ar, ai = _ssm_matrices(ssm_log_dt[l], ssm_a_re[l], ssm_a_im[l], ssm_b_re[l],
                                           ssm_b_im[l], ssm_c_re[l], ssm_c_im[l], ssm_d[l])
        tt, wp, wc = _ssm_build(kc, vc, cm)
        p = _ssm_p(uflat, wp)
        hprev = _ssm_scan(p, ar, ai, bsz)
        yg = _ssm_y(uflat, tt, hprev, wc)

        outs, lses = [], []
        for gi, dil in enumerate(DILATIONS):
            qg, kg, vg = (a.reshape(bsz * dil, seq // dil, ATT_MERGED) for a in qkv[3 * gi:3 * gi + 3])
            o_g, lse_g = _banded_attention(qg, kg, vg)
            outs.append(o_g.reshape(bsz, dil, seq // dil, ATT_MERGED))
            lses.append(lse_g.reshape(bsz, dil, seq // dil, ATT_MERGED))

        h = _mix(yg, outs, lses, gs, ga, h, w_glu[l].astype(BF16), row2(b_glu[l]),
                 w_att_up[l].astype(BF16), w_mix_out[l].astype(BF16), row2(b_mix_out[l]),
                 row2(ln1_g[l]), row2(ln1_b[l]), seq)

        kv = _mem_kv(mem, w_xkv[l].astype(BF16))
        h = _xattn(h, kv, w_xq[l].astype(BF16), w_xo[l].astype(BF16), row2(ln2_g[l]), row2(ln2_b[l]), seq)
        h = _ffn(h, w_ff1[l].astype(BF16), row2(b_ff1[l]), w_ff2[l].astype(BF16), row2(b_ff2[l]),
                 row2(ln3_g[l]), row2(ln3_b[l]))
    return h.reshape(bsz, seq, D_MODEL)
```

```python
import functools
import math

import jax
import jax.numpy as jnp
import numpy as np
from jax import lax
from jax.experimental import pallas as pl
from jax.experimental.pallas import tpu as pltpu

F32 = jnp.float32
BF16 = jnp.bfloat16

D_MODEL = 1024
SSM_GROUP = 16
SSM_WIDTH = 768
SSM_GROUPS = SSM_WIDTH // SSM_GROUP
SSM_STATE = 64
ATT_HEAD_DIM = 64
ATT_HEADS_PER_GROUP = 4
DILATIONS = (1, 4, 16)
ATT_SPAN = 128
ATT_WIDTH = 768
ATT_MERGED = ATT_HEADS_PER_GROUP * ATT_HEAD_DIM
ATT_SCALE = ATT_HEAD_DIM ** -0.5
ROT_DIM = ATT_HEAD_DIM // 4
ROPE_THETA = 500000.0
XATT_HEADS = 4
XATT_HEAD_DIM = D_MODEL // XATT_HEADS
XATT_SCALE = XATT_HEAD_DIM ** -0.5
D_FF = 4 * D_MODEL
DEPTH = 1
DEEPNORM_ALPHA = (2 * DEPTH) ** 0.25
LN_EPS = 1e-5
NEG_INF = -1e30
LOG2E = math.log2(math.e)
LN2 = math.log(2.0)

LANES = 128
CHUNK = 16
SG_GROUPS = LANES // SSM_GROUP
N_SG = SSM_GROUPS // SG_GROUPS
FLAT = CHUNK * LANES
SG_STATE = SG_GROUPS * SSM_STATE
MXU_TILE = 256
N_TT = FLAT // MXU_TILE

SSM_P_ROWS = 1024
SSM_Y_ROWS = 512
SUB_ROWS = 256
TOKEN_TILE = 512
STAGE_LAG = 1
MIX_CHUNK = 256
FFN_CHUNK = 1024
QKV_OFF = SSM_WIDTH
GATE_OFF = SSM_WIDTH + 3 * ATT_WIDTH
VMEM_LIMIT = 56 * 1024 * 1024


def _cparams(n_axes):
    return pltpu.CompilerParams(dimension_semantics=("parallel",) * n_axes,
                                vmem_limit_bytes=VMEM_LIMIT)


def _layer_norm(x, g, b):
    mu = jnp.mean(x, axis=-1, keepdims=True)
    xc = x - mu
    var = jnp.mean(xc * xc, axis=-1, keepdims=True)
    return xc * lax.rsqrt(var + LN_EPS) * g + b


def _sigmoid(x):
    return 0.5 * jnp.tanh(0.5 * x) + 0.5


def _stagger(tiles, lag):
    tiles = list(tiles)
    live = [True] * len(tiles)
    rnd = 0
    while any(live):
        for i in reversed(range(len(tiles))):
            if live[i] and rnd >= i * lag:
                try:
                    next(tiles[i])
                except StopIteration:
                    live[i] = False
        rnd += 1


def _sub_tiles(tm):
    return [(i, slice(i * SUB_ROWS, (i + 1) * SUB_ROWS)) for i in range(tm // SUB_ROWS)]


def _ln_proj_kernel(x_ref, g_ref, b_ref, w_ref, bi_ref, cs_ref, sel_ref, one_ref,
                    h_ref, u_ref, q0_ref, k0_ref, v0_ref, q1_ref, k1_ref, v1_ref, q2_ref, k2_ref, v2_ref,
                    scr_ref):
    half = ROT_DIM // 2
    qkv_refs = ((q0_ref, k0_ref, v0_ref), (q1_ref, k1_ref, v1_ref), (q2_ref, k2_ref, v2_ref))
    sections = [(0, SSM_WIDTH), (QKV_OFF, ATT_WIDTH), (QKV_OFF + ATT_WIDTH, ATT_WIDTH),
                (QKV_OFF + 2 * ATT_WIDTH, ATT_WIDTH)]

    def tile(sub, rows):
        h = _layer_norm(x_ref[rows, :], g_ref[...], b_ref[...])
        h_ref[rows, :] = h
        hb = h.astype(BF16)
        cs = cs_ref[rows, :]
        cs_hi = cs.astype(BF16)
        cs_lo = (cs - cs_hi.astype(F32)).astype(BF16)
        tab = (jnp.dot(cs_hi, sel_ref[...], preferred_element_type=F32)
               + jnp.dot(cs_lo, sel_ref[...], preferred_element_type=F32))
        cc = tab[:, 0:LANES] + one_ref[...]
        s1 = tab[:, LANES:2 * LANES]
        s2 = tab[:, 2 * LANES:3 * LANES]
        yield

        def stage(off, val):
            for i in range(val.shape[1] // LANES):
                scr_ref[sub, off // LANES + i] = val[:, i * LANES:(i + 1) * LANES]

        def rope(val, scale):
            out = []
            for i in range(ATT_WIDTH // LANES):
                t = val[:, i * LANES:(i + 1) * LANES]
                out.append((t * cc + pltpu.roll(t, LANES - half, 1) * s1 + pltpu.roll(t, half, 1) * s2) * scale)
            return jnp.concatenate(out, axis=-1)

        def emit_groups(a):
            for gi, dil in enumerate(DILATIONS):
                ref = qkv_refs[gi][a]
                cb0 = (QKV_OFF + a * ATT_WIDTH + gi * ATT_MERGED) // LANES
                n = SUB_ROWS // dil
                for r in range(dil):
                    for e in range(ATT_MERGED // LANES):
                        if dil == 1:
                            blk = scr_ref[sub, cb0 + e]
                        else:
                            blk = scr_ref[sub, cb0 + e, pl.ds(r, n, stride=dil), :]
                        ref[0, r, sub * n:(sub + 1) * n, e * LANES:(e + 1) * LANES] = blk.astype(BF16)

        for c, (off, width) in enumerate(sections):
            val = (jnp.dot(hb, w_ref[:, off:off + width], preferred_element_type=F32)
                   + bi_ref[:, off:off + width])
            if c == 0:
                stage(off, val)
                n = SUB_ROWS // CHUNK
                for j in range(CHUNK):
                    for sg in range(N_SG):
                        blk = scr_ref[sub, sg, pl.ds(j, n, stride=CHUNK), :]
                        u_ref[sg, sub * n:(sub + 1) * n, j * LANES:(j + 1) * LANES] = blk.astype(BF16)
            elif c in (1, 2):
                stage(off, rope(val, ATT_SCALE * LOG2E if c == 1 else 1.0))
                emit_groups(c - 1)
            else:
                stage(off, val)
                emit_groups(2)
            yield

    _stagger([tile(i, rows) for i, rows in _sub_tiles(x_ref.shape[0])], lag=STAGE_LAG)


def _ln_proj(x2, g, b, w, bi, cs, bsz, seq):
    t = x2.shape[0]
    tm = 2 * TOKEN_TILE
    per_b = seq // tm
    sel, one = _rope_selectors()
    row = lambda n: pl.BlockSpec((tm, n), lambda i: (i, 0))
    full = lambda a: pl.BlockSpec(a.shape, lambda i: (0,) * a.ndim, pipeline_mode=pl.Buffered(1))
    grp = lambda d: pl.BlockSpec((1, d, tm // d, ATT_MERGED), lambda i: (i // per_b, 0, i % per_b, 0))
    outs = [jax.ShapeDtypeStruct((t, D_MODEL), F32),
            jax.ShapeDtypeStruct((N_SG, t // CHUNK, FLAT), BF16)]
    out_specs = [row(D_MODEL), pl.BlockSpec((N_SG, tm // CHUNK, FLAT), lambda i: (0, i, 0))]
    for d in DILATIONS:
        outs += [jax.ShapeDtypeStruct((bsz, d, seq // d, ATT_MERGED), BF16)] * 3
        out_specs += [grp(d)] * 3
    return pl.pallas_call(
        _ln_proj_kernel,
        grid=(t // tm,),
        in_specs=[row(D_MODEL), full(g), full(b), full(w), full(bi), row(ROT_DIM), full(sel), full(one)],
        out_specs=out_specs,
        out_shape=outs,
        scratch_shapes=[pltpu.VMEM((tm // SUB_ROWS, GATE_OFF // LANES, SUB_ROWS, LANES), F32)],
        compiler_params=_cparams(1),
        name="ln_proj",
    )(x2, g, b, w, bi, cs, sel, one)


def _ssm_p_kernel(u_ref, wp_ref, p_ref):
    p_ref[0] = jnp.dot(u_ref[0], wp_ref[0], preferred_element_type=F32)


def _ssm_p(uflat, wp):
    nc = uflat.shape[1]
    tn = SSM_P_ROWS
    return pl.pallas_call(
        _ssm_p_kernel,
        grid=(N_SG, nc // tn),
        in_specs=[pl.BlockSpec((1, tn, FLAT), lambda s, i: (s, i, 0)),
                  pl.BlockSpec((1, FLAT, 2 * SG_STATE), lambda s, i: (s, 0, 0))],
        out_specs=pl.BlockSpec((1, tn, 2 * SG_STATE), lambda s, i: (s, i, 0)),
        out_shape=jax.ShapeDtypeStruct((N_SG, nc, 2 * SG_STATE), F32),
        compiler_params=_cparams(2),
        name="ssm_p",
    )(uflat, wp)


def _ssm_scan_kernel(p_ref, ar_ref, ai_ref, h_ref, *, steps, nseq):
    ar = ar_ref[0]
    ai = ai_ref[0]

    def body(c, carry):
        new = []
        for b in range(nseq):
            hr, hi = carry[b]
            row = b * steps + c
            h_ref[0, pl.ds(row, 1), 0:SG_STATE] = hr
            h_ref[0, pl.ds(row, 1), SG_STATE:2 * SG_STATE] = hi
            pr = p_ref[0, pl.ds(row, 1), 0:SG_STATE]
            pi = p_ref[0, pl.ds(row, 1), SG_STATE:2 * SG_STATE]
            new.append((ar * hr - ai * hi + pr, ar * hi + ai * hr + pi))
        return tuple(new)

    z = jnp.zeros((1, SG_STATE), F32)
    lax.fori_loop(0, steps, body, ((z, z),) * nseq, unroll=2)


def _ssm_scan(p, ar, ai, bsz):
    nc = p.shape[1]
    return pl.pallas_call(
        functools.partial(_ssm_scan_kernel, steps=nc // bsz, nseq=bsz),
        grid=(N_SG,),
        in_specs=[pl.BlockSpec((1, nc, 2 * SG_STATE), lambda s: (s, 0, 0)),
                  pl.BlockSpec((1, 1, SG_STATE), lambda s: (s, 0, 0)),
                  pl.BlockSpec((1, 1, SG_STATE), lambda s: (s, 0, 0))],
        out_specs=pl.BlockSpec((1, nc, 2 * SG_STATE), lambda s: (s, 0, 0)),
        out_shape=jax.ShapeDtypeStruct(p.shape, F32),
        compiler_params=_cparams(1),
        name="ssm_scan",
    )(p, ar, ai)


def _ssm_y_kernel(u_ref, t_ref, h_ref, wc_ref, y_ref, scr_ref):
    tn = u_ref.shape[1]
    hb = h_ref[0].astype(BF16)
    for jo in range(N_TT):
        cols = slice(jo * MXU_TILE, (jo + 1) * MXU_TILE)
        acc = jnp.dot(hb, wc_ref[0, :, cols], preferred_element_type=F32)
        for ji in range(jo + 1):
            acc += jnp.dot(u_ref[0, :, ji * MXU_TILE:(ji + 1) * MXU_TILE], t_ref[0, jo - ji],
                           preferred_element_type=F32)
        yg = jax.nn.gelu(acc, approximate=True)
        for e in range(MXU_TILE // LANES):
            j = jo * (MXU_TILE // LANES) + e
            scr_ref[pl.ds(j, tn, stride=CHUNK), :] = yg[:, e * LANES:(e + 1) * LANES]
    y_ref[...] = scr_ref[...].astype(BF16)


def _ssm_y(uflat, tt, hprev, wc):
    nc = uflat.shape[1]
    tn = SSM_Y_ROWS
    return pl.pallas_call(
        _ssm_y_kernel,
        grid=(N_SG, nc // tn),
        in_specs=[pl.BlockSpec((1, tn, FLAT), lambda s, i: (s, i, 0)),
                  pl.BlockSpec((1, N_TT, MXU_TILE, MXU_TILE), lambda s, i: (s, 0, 0, 0)),
                  pl.BlockSpec((1, tn, 2 * SG_STATE), lambda s, i: (s, i, 0)),
                  pl.BlockSpec((1, 2 * SG_STATE, FLAT), lambda s, i: (s, 0, 0))],
        out_specs=pl.BlockSpec((tn * CHUNK, LANES), lambda s, i: (i, s)),
        out_shape=jax.ShapeDtypeStruct((nc * CHUNK, SSM_WIDTH), BF16),
        scratch_shapes=[pltpu.VMEM((tn * CHUNK, LANES), F32)],
        compiler_params=_cparams(2),
        name="ssm_y",
    )(uflat, tt, hprev, wc)


def _ssm_matrices(log_dt, a_re, a_im, b_re, b_im, c_re, c_im, d):
    g, n, c = SSM_GROUPS, SSM_STATE, SSM_GROUP
    dt = jnp.exp(log_dt.astype(F32))[:, None]
    a_re = a_re.astype(F32)
    a_im = a_im.astype(F32)
    ks = jnp.arange(CHUNK + 1, dtype=F32)
    mag = jnp.exp((a_re * dt)[..., None] * ks)
    ang = (a_im * dt)[..., None] * ks
    pw_re = mag * jnp.cos(ang)
    pw_im = mag * jnp.sin(ang)
    ab_re, ab_im = pw_re[..., 1], pw_im[..., 1]
    den = jnp.square(a_re) + jnp.square(a_im)
    nr = ab_re - 1.0
    f_re = (nr * a_re + ab_im * a_im) / den
    f_im = (ab_im * a_re - nr * a_im) / den
    b_re = b_re.astype(F32)
    b_im = b_im.astype(F32)
    bb_re = f_re[..., None] * b_re - f_im[..., None] * b_im
    bb_im = f_re[..., None] * b_im + f_im[..., None] * b_re
    c_re = c_re.astype(F32)
    c_im = c_im.astype(F32)
    ca_re = c_re[..., None] * pw_re[:, None] - c_im[..., None] * pw_im[:, None]
    ca_im = c_re[..., None] * pw_im[:, None] + c_im[..., None] * pw_re[:, None]

    kk = (jnp.einsum('gonk,gni->gkio', ca_re[..., :CHUNK], bb_re)
          - jnp.einsum('gonk,gni->gkio', ca_im[..., :CHUNK], bb_im))
    lag0_diag = (np.arange(CHUNK)[:, None, None] == 0) & np.eye(c, dtype=bool)[None]
    kk = kk + jnp.where(lag0_diag[None], d.astype(F32).reshape(g, 1, c, 1), 0.0)
    kc = jnp.transpose(kk.reshape(N_SG, SG_GROUPS, CHUNK, c, c), (0, 2, 1, 3, 4)).reshape(N_SG, CHUNK, LANES, c)

    rp_re = pw_re[..., CHUNK - 1::-1][..., :CHUNK]
    rp_im = pw_im[..., CHUNK - 1::-1][..., :CHUNK]
    v_re = jnp.einsum('gns,gni->gsin', rp_re, bb_re) - jnp.einsum('gns,gni->gsin', rp_im, bb_im)
    v_im = jnp.einsum('gns,gni->gsin', rp_re, bb_im) + jnp.einsum('gns,gni->gsin', rp_im, bb_re)

    def rows_p(v):
        return jnp.transpose(v.reshape(N_SG, SG_GROUPS, CHUNK, c, n), (0, 2, 1, 3, 4)).reshape(N_SG, FLAT, n)

    vc = jnp.stack([rows_p(v_re), rows_p(v_im)], axis=1)

    def rows_c(m):
        return jnp.transpose(m[..., 1:], (0, 2, 3, 1)).reshape(N_SG, SG_STATE, CHUNK * c)

    cm = jnp.stack([rows_c(ca_re), rows_c(-ca_im)], axis=1)

    ar = pw_re[..., CHUNK].reshape(N_SG, 1, SG_STATE)
    ai = pw_im[..., CHUNK].reshape(N_SG, 1, SG_STATE)
    return kc.astype(BF16), vc.astype(BF16), cm.astype(BF16), ar, ai


def _ssm_build_kernel(kc_ref, vc_ref, cm_ref, tt_ref, wp_ref, wc_ref):
    def iota(shape, dim):
        return lax.broadcasted_iota(jnp.int32, shape, dim)

    def spread(nrows):
        r, cidx = iota((nrows, nrows * SG_GROUPS), 0), iota((nrows, nrows * SG_GROUPS), 1)
        return jnp.where(((r >> 4) == (cidx >> 7)) & ((r & 15) == (cidx & 15)), 1.0, 0.0).astype(BF16)

    grp16 = lambda idx: (idx >> 4) & (SG_GROUPS - 1)
    grp64 = lambda idx: (idx >> 6) & (SG_GROUPS - 1)

    e_t = spread(SSM_GROUP)
    keep = grp16(iota((LANES, LANES), 0)) == grp16(iota((LANES, LANES), 1))
    lag_blocks = [jnp.where(keep, jnp.dot(kc_ref[0, k], e_t, preferred_element_type=F32), 0.0).astype(BF16)
                  for k in range(CHUNK)]
    for dlt in range(N_TT):
        for s in range(2):
            for t in range(2):
                k = 2 * dlt + t - s
                blk = lag_blocks[k] if k >= 0 else jnp.zeros((LANES, LANES), BF16)
                tt_ref[0, dlt, s * LANES:(s + 1) * LANES, t * LANES:(t + 1) * LANES] = blk

    r, cidx = iota((SSM_STATE, SG_STATE), 0), iota((SSM_STATE, SG_STATE), 1)
    e_p = jnp.where(r == (cidx & (SSM_STATE - 1)), 1.0, 0.0).astype(BF16)
    keep = grp16(iota((FLAT, SG_STATE), 0)) == grp64(iota((FLAT, SG_STATE), 1))
    for part in range(2):
        full = jnp.dot(vc_ref[0, part], e_p, preferred_element_type=F32)
        wp_ref[0, :, part * SG_STATE:(part + 1) * SG_STATE] = jnp.where(keep, full, 0.0).astype(BF16)

    e_c = spread(CHUNK * SSM_GROUP)
    keep = grp64(iota((SG_STATE, FLAT), 0)) == grp16(iota((SG_STATE, FLAT), 1))
    for part in range(2):
        full = jnp.dot(cm_ref[0, part], e_c, preferred_element_type=F32)
        wc_ref[0, part * SG_STATE:(part + 1) * SG_STATE, :] = jnp.where(keep, full, 0.0).astype(BF16)


def _ssm_build(kc, vc, cm):
    blk = lambda a: pl.BlockSpec((1,) + a.shape[1:], lambda s: (s,) + (0,) * (a.ndim - 1))
    outs = [jax.ShapeDtypeStruct((N_SG, N_TT, MXU_TILE, MXU_TILE), BF16),
            jax.ShapeDtypeStruct((N_SG, FLAT, 2 * SG_STATE), BF16),
            jax.ShapeDtypeStruct((N_SG, 2 * SG_STATE, FLAT), BF16)]
    return pl.pallas_call(
        _ssm_build_kernel,
        grid=(N_SG,),
        in_specs=[blk(kc), blk(vc), blk(cm)],
        out_specs=[blk(o) for o in outs],
        out_shape=outs,
        compiler_params=_cparams(1),
        name="ssm_build",
    )(kc, vc, cm)


ATT_UNITS_IN_FLIGHT = 4
ATT_STEP_QUERIES = 2048


def _attn_kernel(q_ref, kp_ref, kc_ref, vp_ref, vc_ref, o_ref, lse_ref, ks_ref, vs_ref):
    nsb, qb = q_ref.shape[0], q_ref.shape[1]
    blk = ATT_SPAN
    first = pl.program_id(1) == 0

    lane = lax.broadcasted_iota(jnp.int32, (1, ATT_MERGED), 1)
    head_sel = (lane % LANES < ATT_HEAD_DIM, lane % LANES >= ATT_HEAD_DIM)
    for sq in range(nsb):
        for src_p, src_c, dst in ((kp_ref, kc_ref, ks_ref), (vp_ref, vc_ref, vs_ref)):
            for e in range(2):
                xp, xc = src_p[sq], src_c[sq]
                dst[sq, e, 0:blk] = jnp.where(head_sel[e], xp, jnp.zeros_like(xp))
                dst[sq, e, blk:blk + qb] = jnp.where(head_sel[e], xc, jnp.zeros_like(xc))

    qi = lax.broadcasted_iota(jnp.int32, (blk, 2 * blk), 0)
    kk = lax.broadcasted_iota(jnp.int32, (blk, 2 * blk), 1)
    band = jnp.logical_and(kk >= qi, kk <= qi + blk)
    band_first = jnp.logical_and(band, jnp.logical_or(kk >= blk, jnp.logical_not(first)))
    low = lax.broadcasted_iota(jnp.int32, (blk, LANES), 1) < ATT_HEAD_DIM
    dn = (((1,), (1,)), ((), ()))

    units = [(sq, j, pair) for sq in range(nsb) for j in range(qb // blk) for pair in range(ATT_MERGED // LANES)]
    for g0 in range(0, len(units), ATT_UNITS_IN_FLIGHT):
        group = units[g0:g0 + ATT_UNITS_IN_FLIGHT]
        scores = []
        for sq, j, pair in group:
            cols = slice(pair * LANES, (pair + 1) * LANES)
            keys = slice(j * blk, (j + 2) * blk)
            kcat = jnp.concatenate([ks_ref[sq, 0, keys, cols], ks_ref[sq, 1, keys, cols]], axis=0)
            s = lax.dot_general(q_ref[sq, j * blk:(j + 1) * blk, cols], kcat, dn,
                                preferred_element_type=F32)
            mask = band_first if j == 0 else band
            scores.append([jnp.where(mask, s[:, e * 2 * blk:(e + 1) * 2 * blk], NEG_INF) for e in range(2)])
        maxes = [[jnp.max(jnp.maximum(se[:, :blk], se[:, blk:]), axis=-1, keepdims=True) for se in su]
                 for su in scores]
        probs = [[jnp.exp2(se - me) for se, me in zip(su, mu)] for su, mu in zip(scores, maxes)]
        dens = [[jnp.sum(pe[:, :blk] + pe[:, blk:], axis=-1, keepdims=True) for pe in pu] for pu in probs]
        for (sq, j, pair), pu, mu, du in zip(group, probs, maxes, dens):
            cols = slice(pair * LANES, (pair + 1) * LANES)
            rows = slice(j * blk, (j + 1) * blk)
            keys = slice(j * blk, (j + 2) * blk)
            vcat = jnp.concatenate([vs_ref[sq, 0, keys, cols], vs_ref[sq, 1, keys, cols]], axis=0)
            pcat = jnp.concatenate([pu[0].astype(BF16), pu[1].astype(BF16)], axis=1)
            num = jnp.dot(pcat, vcat, preferred_element_type=F32)
            o_ref[sq, rows, cols] = (num * jnp.where(low, 1.0 / du[0], 1.0 / du[1])).astype(BF16)
            lse_ref[sq, rows, cols] = jnp.where(low, mu[0] * LN2 + jnp.log(du[0]), mu[1] * LN2 + jnp.log(du[1]))


def _banded_attention(q, k, v):
    nseq, length, w = q.shape
    blk = ATT_SPAN
    qb = min(length, ATT_STEP_QUERIES)
    nsb = ATT_STEP_QUERIES // qb
    per = qb // blk
    cur = pl.BlockSpec((nsb, qb, w), lambda s, n: (s, n, 0))
    prev = pl.BlockSpec((nsb, blk, w), lambda s, n: (s, jnp.maximum(n * per - 1, 0), 0))
    return pl.pallas_call(
        _attn_kernel,
        grid=(nseq // nsb, length // qb),
        in_specs=[cur, prev, cur, prev, cur],
        out_specs=[cur, cur],
        out_shape=[jax.ShapeDtypeStruct(q.shape, BF16), jax.ShapeDtypeStruct(q.shape, F32)],
        scratch_shapes=[pltpu.VMEM((nsb, 2, blk + qb, w), BF16)] * 2,
        compiler_params=_cparams(2),
        name="attn",
    )(q, k, k, v, v)


def _mix_kernel(y_ref, o0_ref, o1_ref, o2_ref, l0_ref, l1_ref, l2_ref, h_ref, wgate_ref, bgate_ref,
                wglu_ref, bglu_ref, wup_ref, wmix_ref, bmix_ref, g_ref, b_ref, out_ref, scr_ref, acc_ref):
    o_refs = (o0_ref, o1_ref, o2_ref)
    l_refs = (l0_ref, l1_ref, l2_ref)

    def tile(sub, rows):
        def natural(ref, dil):
            n = SUB_ROWS // dil
            if dil == 1:
                return ref[0, 0, rows, :].astype(F32)
            for r in range(dil):
                for e in range(ATT_MERGED // LANES):
                    scr_ref[sub, e, pl.ds(r, n, stride=dil), :] = (
                        ref[0, r, sub * n:(sub + 1) * n, e * LANES:(e + 1) * LANES].astype(F32))
            return jnp.concatenate([scr_ref[sub, e] for e in range(ATT_MERGED // LANES)], axis=-1)

        ls = [natural(l_refs[gi], dil) for gi, dil in enumerate(DILATIONS)]
        m = jnp.maximum(jnp.maximum(ls[0], ls[1]), ls[2])
        es = [jnp.exp(l - m) for l in ls]
        att = es[0] * natural(o_refs[0], DILATIONS[0])
        for gi in (1, 2):
            att += es[gi] * natural(o_refs[gi], DILATIONS[gi])
        att = (att / (es[0] + es[1] + es[2])).astype(BF16)
        y = y_ref[rows, :]
        hb = h_ref[rows, :].astype(BF16)
        yield

        for c in range(D_MODEL // MIX_CHUNK):
            cols = slice(c * MIX_CHUNK, (c + 1) * MIX_CHUNK)
            gate_cols = slice(D_MODEL + c * MIX_CHUNK, D_MODEL + (c + 1) * MIX_CHUNK)
            val = jnp.dot(y, wglu_ref[:, cols], preferred_element_type=F32) + bglu_ref[:, cols]
            gate = jnp.dot(y, wglu_ref[:, gate_cols], preferred_element_type=F32) + bglu_ref[:, gate_cols]
            b_att = jnp.dot(att, wup_ref[:, cols], preferred_element_type=F32)
            g_ssm = jnp.dot(hb, wgate_ref[:, cols], preferred_element_type=F32) + bgate_ref[:, cols]
            g_att = jnp.dot(hb, wgate_ref[:, gate_cols], preferred_element_type=F32) + bgate_ref[:, gate_cols]
            mixed = _sigmoid(g_ssm) * (val * _sigmoid(gate)) + _sigmoid(g_att) * b_att
            acc_ref[rows, cols] = mixed.astype(BF16)
            yield

        r = jnp.dot(acc_ref[rows, :], wmix_ref[...], preferred_element_type=F32) + bmix_ref[...]
        yield
        out_ref[rows, :] = _layer_norm(DEEPNORM_ALPHA * h_ref[rows, :] + r, g_ref[...], b_ref[...])

    _stagger([tile(i, rows) for i, rows in _sub_tiles(h_ref.shape[0])], lag=STAGE_LAG)


def _mix(yg, outs, lses, h, wgate, bgate, wglu, bglu, wup, wmix, bmix, g, b, seq):
    t = h.shape[0]
    tm = 2 * TOKEN_TILE
    per_b = seq // tm
    row = lambda n: pl.BlockSpec((tm, n), lambda i: (i, 0))
    full = lambda a: pl.BlockSpec(a.shape, lambda i: (0,) * a.ndim, pipeline_mode=pl.Buffered(1))
    grp = lambda d: pl.BlockSpec((1, d, tm // d, ATT_MERGED), lambda i: (i // per_b, 0, i % per_b, 0))
    consts = (wgate, bgate, wglu, bglu, wup, wmix, bmix, g, b)
    return pl.pallas_call(
        _mix_kernel,
        grid=(t // tm,),
        in_specs=([row(SSM_WIDTH)] + [grp(d) for d in DILATIONS] * 2 + [row(D_MODEL)]
                  + [full(a) for a in consts]),
        out_specs=row(D_MODEL),
        out_shape=jax.ShapeDtypeStruct((t, D_MODEL), F32),
        scratch_shapes=[pltpu.VMEM((tm // SUB_ROWS, ATT_MERGED // LANES, SUB_ROWS, LANES), F32),
                        pltpu.VMEM((tm, D_MODEL), BF16)],
        compiler_params=_cparams(1),
        name="mix",
    )(yg, *outs, *lses, h, *consts)


def _mem_kv_kernel(m_ref, w_ref, kv_ref):
    kv_ref[0] = jnp.dot(m_ref[0].astype(BF16), w_ref[...], preferred_element_type=F32).astype(BF16)


def _mem_kv(mem, w):
    bsz, ml, _ = mem.shape
    return pl.pallas_call(
        _mem_kv_kernel,
        grid=(bsz,),
        in_specs=[pl.BlockSpec((1, ml, D_MODEL), lambda b: (b, 0, 0)),
                  pl.BlockSpec(w.shape, lambda b: (0, 0))],
        out_specs=pl.BlockSpec((1, ml, 2 * D_MODEL), lambda b: (b, 0, 0)),
        out_shape=jax.ShapeDtypeStruct((bsz, ml, 2 * D_MODEL), BF16),
        compiler_params=_cparams(1),
        name="mem_kv",
    )(mem, w)


def _xattn_kernel(h_ref, kv_ref, wq_ref, wo_ref, g_ref, b_ref, out_ref):
    dn = (((1,), (1,)), ((), ()))

    def tile(sub, rows):
        h = h_ref[rows, :]
        hb = h.astype(BF16)
        yield
        q = (jnp.dot(hb, wq_ref[...], preferred_element_type=F32) * XATT_SCALE).astype(BF16)
        yield
        outs = []
        for hh in range(XATT_HEADS):
            sl = slice(hh * XATT_HEAD_DIM, (hh + 1) * XATT_HEAD_DIM)
            sv = slice(D_MODEL + hh * XATT_HEAD_DIM, D_MODEL + (hh + 1) * XATT_HEAD_DIM)
            s = lax.dot_general(q[:, sl], kv_ref[0, :, sl], dn, preferred_element_type=F32)
            e = jnp.exp(s - jnp.max(s, axis=-1, keepdims=True))
            p = e / jnp.sum(e, axis=-1, keepdims=True)
            outs.append(jnp.dot(p.astype(BF16), kv_ref[0, :, sv], preferred_element_type=F32).astype(BF16))
            yield
        xo = jnp.dot(jnp.concatenate(outs, axis=-1), wo_ref[...], preferred_element_type=F32)
        yield
        out_ref[rows, :] = _layer_norm(DEEPNORM_ALPHA * h + xo, g_ref[...], b_ref[...])

    _stagger([tile(i, rows) for i, rows in _sub_tiles(h_ref.shape[0])], lag=STAGE_LAG)


def _xattn(h, kv, wq, wo, g, b, seq):
    t = h.shape[0]
    tm = 2 * TOKEN_TILE
    per_b = seq // tm
    row = pl.BlockSpec((tm, D_MODEL), lambda i: (i, 0))
    full = lambda a: pl.BlockSpec(a.shape, lambda i: (0,) * a.ndim)
    return pl.pallas_call(
        _xattn_kernel,
        grid=(t // tm,),
        in_specs=[row, pl.BlockSpec((1,) + kv.shape[1:], lambda i: (i // per_b, 0, 0)),
                  full(wq), full(wo), full(g), full(b)],
        out_specs=row,
        out_shape=jax.ShapeDtypeStruct((t, D_MODEL), F32),
        compiler_params=_cparams(1),
        name="xattn",
    )(h, kv, wq, wo, g, b)


def _ffn_kernel(h_ref, w1_ref, b1_ref, w2_ref, b2_ref, g_ref, b_ref, out_ref, acc_ref):
    def tile(sub, rows):
        h = h_ref[rows, :]
        hb = h.astype(BF16)
        yield
        for c in range(D_FF // FFN_CHUNK):
            cols = slice(c * FFN_CHUNK, (c + 1) * FFN_CHUNK)
            a = jnp.dot(hb, w1_ref[:, cols], preferred_element_type=F32) + b1_ref[:, cols]
            a = jnp.square(jnp.maximum(a, 0.0)).astype(BF16)
            part = jnp.dot(a, w2_ref[cols, :], preferred_element_type=F32)
            if c == 0:
                acc_ref[rows, :] = part
            else:
                acc_ref[rows, :] += part
            yield
        ff = acc_ref[rows, :] + b2_ref[...]
        out_ref[rows, :] = _layer_norm(DEEPNORM_ALPHA * h + ff, g_ref[...], b_ref[...])

    _stagger([tile(i, rows) for i, rows in _sub_tiles(h_ref.shape[0])], lag=STAGE_LAG)


def _ffn(h, w1, b1, w2, b2, g, b):
    t = h.shape[0]
    tm = 2 * TOKEN_TILE
    row = pl.BlockSpec((tm, D_MODEL), lambda i: (i, 0))
    full = lambda a: pl.BlockSpec(a.shape, lambda i: (0,) * a.ndim, pipeline_mode=pl.Buffered(1))
    consts = (w1, b1, w2, b2, g, b)
    return pl.pallas_call(
        _ffn_kernel,
        grid=(t // tm,),
        in_specs=[row] + [full(a) for a in consts],
        out_specs=row,
        out_shape=jax.ShapeDtypeStruct((t, D_MODEL), F32),
        scratch_shapes=[pltpu.VMEM((tm, D_MODEL), F32)],
        compiler_params=_cparams(1),
        name="ffn",
    )(h, *consts)


def _rope_selectors():
    half = ROT_DIM // 2
    sel = np.zeros((ROT_DIM, 3 * LANES), np.float32)
    one = np.zeros((1, LANES), np.float32)
    for lane in range(LANES):
        d = lane % ATT_HEAD_DIM
        if d >= ROT_DIM:
            one[0, lane] = 1.0
            continue
        sel[d % half, lane] = 1.0
        if d < half:
            sel[half + d, LANES + lane] = -1.0
        else:
            sel[d, 2 * LANES + lane] = 1.0
    return jnp.asarray(sel, BF16), jnp.asarray(one)


def _rope_cos_sin(positions):
    inv_freq = ROPE_THETA ** (-jnp.arange(0, ROT_DIM, 2, dtype=F32) / ROT_DIM)
    ang = positions.astype(F32).reshape(-1, 1) * inv_freq
    return jnp.concatenate([jnp.cos(ang), jnp.sin(ang)], axis=-1)


def kernel(x, mem, positions, ln_in_g, ln_in_b, w_in, b_in, ssm_log_dt, ssm_a_re, ssm_a_im, ssm_b_re, ssm_b_im, ssm_c_re, ssm_c_im, ssm_d, w_glu, b_glu, w_att_up, w_mix_out, b_mix_out, ln1_g, ln1_b, w_xq, w_xkv, w_xo, ln2_g, ln2_b, w_ff1, b_ff1, w_ff2, b_ff2, ln3_g, ln3_b):
    bsz, seq, _ = x.shape
    t = bsz * seq
    row2 = lambda a: a.reshape(1, -1).astype(F32)
    cs = _rope_cos_sin(positions)
    h = x.reshape(t, D_MODEL)
    for l in range(DEPTH):
        h, uflat, *qkv = _ln_proj(
            h, row2(ln_in_g), row2(ln_in_b), w_in[l, :, :GATE_OFF].astype(BF16), row2(b_in[l, :GATE_OFF]),
            cs, bsz, seq)

        kc, vc, cm, ar, ai = _ssm_matrices(ssm_log_dt[l], ssm_a_re[l], ssm_a_im[l], ssm_b_re[l],
                                           ssm_b_im[l], ssm_c_re[l], ssm_c_im[l], ssm_d[l])
        tt, wp, wc = _ssm_build(kc, vc, cm)
        p = _ssm_p(uflat, wp)
        hprev = _ssm_scan(p, ar, ai, bsz)
        yg = _ssm_y(uflat, tt, hprev, wc)

        outs, lses = [], []
        for gi, dil in enumerate(DILATIONS):
            qg, kg, vg = (a.reshape(bsz * dil, seq // dil, ATT_MERGED) for a in qkv[3 * gi:3 * gi + 3])
            o_g, lse_g = _banded_attention(qg, kg, vg)
            outs.append(o_g.reshape(bsz, dil, seq // dil, ATT_MERGED))
            lses.append(lse_g.reshape(bsz, dil, seq // dil, ATT_MERGED))

        h = _mix(yg, outs, lses, h, w_in[l, :, GATE_OFF:].astype(BF16), row2(b_in[l, GATE_OFF:]),
                 w_glu[l].astype(BF16), row2(b_glu[l]),
                 w_att_up[l].astype(BF16), w_mix_out[l].astype(BF16), row2(b_mix_out[l]),
                 row2(ln1_g[l]), row2(ln1_b[l]), seq)

        kv = _mem_kv(mem, w_xkv[l].astype(BF16))
        h = _xattn(h, kv, w_xq[l].astype(BF16), w_xo[l].astype(BF16), row2(ln2_g[l]), row2(ln2_b[l]), seq)
        h = _ffn(h, w_ff1[l].astype(BF16), row2(b_ff1[l]), w_ff2[l].astype(BF16), row2(b_ff2[l]),
                 row2(ln3_g[l]), row2(ln3_b[l]))
    return h.reshape(bsz, seq, D_MODEL)
```

```python
import functools
import math

import jax
import jax.numpy as jnp
import numpy as np
from jax import lax
from jax.experimental import pallas as pl
from jax.experimental.pallas import tpu as pltpu

F32 = jnp.float32
BF16 = jnp.bfloat16

D_MODEL = 1024
SSM_GROUP = 16
SSM_WIDTH = 768
SSM_GROUPS = SSM_WIDTH // SSM_GROUP
SSM_STATE = 64
ATT_HEAD_DIM = 64
ATT_HEADS_PER_GROUP = 4
DILATIONS = (1, 4, 16)
ATT_SPAN = 128
ATT_WIDTH = 768
ATT_MERGED = ATT_HEADS_PER_GROUP * ATT_HEAD_DIM
ATT_SCALE = ATT_HEAD_DIM ** -0.5
ROT_DIM = ATT_HEAD_DIM // 4
ROPE_THETA = 500000.0
XATT_HEADS = 4
XATT_HEAD_DIM = D_MODEL // XATT_HEADS
XATT_SCALE = XATT_HEAD_DIM ** -0.5
D_FF = 4 * D_MODEL
DEPTH = 1
DEEPNORM_ALPHA = (2 * DEPTH) ** 0.25
LN_EPS = 1e-5
NEG_INF = -1e30
LOG2E = math.log2(math.e)
LN2 = math.log(2.0)

LANES = 128
CHUNK = 16
SG_GROUPS = LANES // SSM_GROUP
N_SG = SSM_GROUPS // SG_GROUPS
FLAT = CHUNK * LANES
SG_STATE = SG_GROUPS * SSM_STATE
MXU_TILE = 256
N_TT = FLAT // MXU_TILE

SSM_Y_ROWS = 512
SUB_ROWS = 256
TOKEN_TILE = 512
STAGE_LAG = 1
MIX_CHUNK = 256
FFN_CHUNK = 1024
QKV_OFF = SSM_WIDTH
GATE_OFF = SSM_WIDTH + 3 * ATT_WIDTH
VMEM_LIMIT = 56 * 1024 * 1024


def _cparams(n_axes):
    return pltpu.CompilerParams(dimension_semantics=("parallel",) * n_axes,
                                vmem_limit_bytes=VMEM_LIMIT)


def _layer_norm(x, g, b):
    mu = jnp.mean(x, axis=-1, keepdims=True)
    xc = x - mu
    var = jnp.mean(xc * xc, axis=-1, keepdims=True)
    return xc * lax.rsqrt(var + LN_EPS) * g + b


def _sigmoid(x):
    return 0.5 * jnp.tanh(0.5 * x) + 0.5


def _stagger(tiles, lag):
    tiles = list(tiles)
    live = [True] * len(tiles)
    rnd = 0
    while any(live):
        for i in reversed(range(len(tiles))):
            if live[i] and rnd >= i * lag:
                try:
                    next(tiles[i])
                except StopIteration:
                    live[i] = False
        rnd += 1


def _sub_tiles(tm):
    return [(i, slice(i * SUB_ROWS, (i + 1) * SUB_ROWS)) for i in range(tm // SUB_ROWS)]


def _ln_proj_kernel(x_ref, g_ref, b_ref, w_ref, bi_ref, cs_ref, sel_ref, one_ref,
                    h_ref, u_ref, q0_ref, k0_ref, v0_ref, q1_ref, k1_ref, v1_ref, q2_ref, k2_ref, v2_ref,
                    scr_ref):
    half = ROT_DIM // 2
    qkv_refs = ((q0_ref, k0_ref, v0_ref), (q1_ref, k1_ref, v1_ref), (q2_ref, k2_ref, v2_ref))
    sections = [(0, SSM_WIDTH), (QKV_OFF, ATT_WIDTH), (QKV_OFF + ATT_WIDTH, ATT_WIDTH),
                (QKV_OFF + 2 * ATT_WIDTH, ATT_WIDTH)]

    def tile(sub, rows):
        h = _layer_norm(x_ref[rows, :], g_ref[...], b_ref[...])
        h_ref[rows, :] = h
        hb = h.astype(BF16)
        cs = cs_ref[rows, :]
        cs_hi = cs.astype(BF16)
        cs_lo = (cs - cs_hi.astype(F32)).astype(BF16)
        tab = (jnp.dot(cs_hi, sel_ref[...], preferred_element_type=F32)
               + jnp.dot(cs_lo, sel_ref[...], preferred_element_type=F32))
        cc = tab[:, 0:LANES] + one_ref[...]
        s1 = tab[:, LANES:2 * LANES]
        s2 = tab[:, 2 * LANES:3 * LANES]
        yield

        def stage(off, val):
            for i in range(val.shape[1] // LANES):
                scr_ref[sub, off // LANES + i] = val[:, i * LANES:(i + 1) * LANES]

        def rope(val, scale):
            out = []
            for i in range(ATT_WIDTH // LANES):
                t = val[:, i * LANES:(i + 1) * LANES]
                out.append((t * cc + pltpu.roll(t, LANES - half, 1) * s1 + pltpu.roll(t, half, 1) * s2) * scale)
            return jnp.concatenate(out, axis=-1)

        def emit_groups(a):
            for gi, dil in enumerate(DILATIONS):
                ref = qkv_refs[gi][a]
                cb0 = (QKV_OFF + a * ATT_WIDTH + gi * ATT_MERGED) // LANES
                n = SUB_ROWS // dil
                for r in range(dil):
                    for e in range(ATT_MERGED // LANES):
                        if dil == 1:
                            blk = scr_ref[sub, cb0 + e]
                        else:
                            blk = scr_ref[sub, cb0 + e, pl.ds(r, n, stride=dil), :]
                        ref[0, r, sub * n:(sub + 1) * n, e * LANES:(e + 1) * LANES] = blk.astype(BF16)

        for c, (off, width) in enumerate(sections):
            val = (jnp.dot(hb, w_ref[:, off:off + width], preferred_element_type=F32)
                   + bi_ref[:, off:off + width])
            if c == 0:
                stage(off, val)
                n = SUB_ROWS // CHUNK
                for j in range(CHUNK):
                    for sg in range(N_SG):
                        blk = scr_ref[sub, sg, pl.ds(j, n, stride=CHUNK), :]
                        u_ref[sg, sub * n:(sub + 1) * n, j * LANES:(j + 1) * LANES] = blk.astype(BF16)
            elif c in (1, 2):
                stage(off, rope(val, ATT_SCALE * LOG2E if c == 1 else 1.0))
                emit_groups(c - 1)
            else:
                stage(off, val)
                emit_groups(2)
            yield

    _stagger([tile(i, rows) for i, rows in _sub_tiles(x_ref.shape[0])], lag=STAGE_LAG)


def _ln_proj(x2, g, b, w, bi, cs, bsz, seq):
    t = x2.shape[0]
    tm = 2 * TOKEN_TILE
    per_b = seq // tm
    sel, one = _rope_selectors()
    row = lambda n: pl.BlockSpec((tm, n), lambda i: (i, 0))
    full = lambda a: pl.BlockSpec(a.shape, lambda i: (0,) * a.ndim, pipeline_mode=pl.Buffered(1))
    grp = lambda d: pl.BlockSpec((1, d, tm // d, ATT_MERGED), lambda i: (i // per_b, 0, i % per_b, 0))
    outs = [jax.ShapeDtypeStruct((t, D_MODEL), F32),
            jax.ShapeDtypeStruct((N_SG, t // CHUNK, FLAT), BF16)]
    out_specs = [row(D_MODEL), pl.BlockSpec((N_SG, tm // CHUNK, FLAT), lambda i: (0, i, 0))]
    for d in DILATIONS:
        outs += [jax.ShapeDtypeStruct((bsz, d, seq // d, ATT_MERGED), BF16)] * 3
        out_specs += [grp(d)] * 3
    return pl.pallas_call(
        _ln_proj_kernel,
        grid=(t // tm,),
        in_specs=[row(D_MODEL), full(g), full(b), full(w), full(bi), row(ROT_DIM), full(sel), full(one)],
        out_specs=out_specs,
        out_shape=outs,
        scratch_shapes=[pltpu.VMEM((tm // SUB_ROWS, GATE_OFF // LANES, SUB_ROWS, LANES), F32)],
        compiler_params=_cparams(1),
        name="ln_proj",
    )(x2, g, b, w, bi, cs, sel, one)


def _ssm_matrices(log_dt, a_re, a_im, b_re, b_im, c_re, c_im, d):
    g, n, c = SSM_GROUPS, SSM_STATE, SSM_GROUP
    dt = jnp.exp(log_dt.astype(F32))[:, None]
    a_re = a_re.astype(F32)
    a_im = a_im.astype(F32)
    ks = jnp.arange(CHUNK + 1, dtype=F32)
    mag = jnp.exp((a_re * dt)[..., None] * ks)
    ang = (a_im * dt)[..., None] * ks
    pw_re = mag * jnp.cos(ang)
    pw_im = mag * jnp.sin(ang)
    ab_re, ab_im = pw_re[..., 1], pw_im[..., 1]
    den = jnp.square(a_re) + jnp.square(a_im)
    nr = ab_re - 1.0
    f_re = (nr * a_re + ab_im * a_im) / den
    f_im = (ab_im * a_re - nr * a_im) / den
    b_re = b_re.astype(F32)
    b_im = b_im.astype(F32)
    bb_re = f_re[..., None] * b_re - f_im[..., None] * b_im
    bb_im = f_re[..., None] * b_im + f_im[..., None] * b_re
    c_re = c_re.astype(F32)
    c_im = c_im.astype(F32)
    ca_re = c_re[..., None] * pw_re[:, None] - c_im[..., None] * pw_im[:, None]
    ca_im = c_re[..., None] * pw_im[:, None] + c_im[..., None] * pw_re[:, None]

    kk = (jnp.einsum('gonk,gni->gkio', ca_re[..., :CHUNK], bb_re)
          - jnp.einsum('gonk,gni->gkio', ca_im[..., :CHUNK], bb_im))
    lag0_diag = (np.arange(CHUNK)[:, None, None] == 0) & np.eye(c, dtype=bool)[None]
    kk = kk + jnp.where(lag0_diag[None], d.astype(F32).reshape(g, 1, c, 1), 0.0)
    kc = jnp.transpose(kk.reshape(N_SG, SG_GROUPS, CHUNK, c, c), (0, 2, 1, 3, 4)).reshape(N_SG, CHUNK, LANES, c)

    rp_re = pw_re[..., CHUNK - 1::-1][..., :CHUNK]
    rp_im = pw_im[..., CHUNK - 1::-1][..., :CHUNK]
    v_re = jnp.einsum('gns,gni->gsin', rp_re, bb_re) - jnp.einsum('gns,gni->gsin', rp_im, bb_im)
    v_im = jnp.einsum('gns,gni->gsin', rp_re, bb_im) + jnp.einsum('gns,gni->gsin', rp_im, bb_re)

    def rows_p(v):
        return jnp.transpose(v.reshape(N_SG, SG_GROUPS, CHUNK, c, n), (0, 2, 1, 3, 4)).reshape(N_SG, FLAT, n)

    vc = jnp.stack([rows_p(v_re), rows_p(v_im)], axis=1)

    def rows_c(m):
        return jnp.transpose(m[..., 1:], (0, 2, 3, 1)).reshape(N_SG, SG_STATE, CHUNK * c)

    cm = jnp.stack([rows_c(ca_re), rows_c(-ca_im)], axis=1)

    ar = pw_re[..., CHUNK].reshape(N_SG, 1, SG_STATE)
    ai = pw_im[..., CHUNK].reshape(N_SG, 1, SG_STATE)
    return kc.astype(BF16), vc.astype(BF16), cm.astype(BF16), ar, ai


def _ssm_kernel(u_ref, kc_ref, vc_ref, cm_ref, ar_ref, ai_ref, y_ref,
                tt_ref, wp_ref, wc_ref, p_ref, scr_ref, *, steps, nseq):
    def iota(shape, dim):
        return lax.broadcasted_iota(jnp.int32, shape, dim)

    def spread(nrows):
        r, cidx = iota((nrows, nrows * SG_GROUPS), 0), iota((nrows, nrows * SG_GROUPS), 1)
        return jnp.where(((r >> 4) == (cidx >> 7)) & ((r & 15) == (cidx & 15)), 1.0, 0.0).astype(BF16)

    grp16 = lambda idx: (idx >> 4) & (SG_GROUPS - 1)
    grp64 = lambda idx: (idx >> 6) & (SG_GROUPS - 1)

    e_t = spread(SSM_GROUP)
    keep = grp16(iota((LANES, LANES), 0)) == grp16(iota((LANES, LANES), 1))
    lag_blocks = [jnp.where(keep, jnp.dot(kc_ref[0, k], e_t, preferred_element_type=F32), 0.0).astype(BF16)
                  for k in range(CHUNK)]
    for dlt in range(N_TT):
        for s in range(2):
            for t in range(2):
                k = 2 * dlt + t - s
                blk = lag_blocks[k] if k >= 0 else jnp.zeros((LANES, LANES), BF16)
                tt_ref[dlt, s * LANES:(s + 1) * LANES, t * LANES:(t + 1) * LANES] = blk

    r, cidx = iota((SSM_STATE, SG_STATE), 0), iota((SSM_STATE, SG_STATE), 1)
    e_p = jnp.where(r == (cidx & (SSM_STATE - 1)), 1.0, 0.0).astype(BF16)
    keep = grp16(iota((FLAT, SG_STATE), 0)) == grp64(iota((FLAT, SG_STATE), 1))
    for part in range(2):
        full = jnp.dot(vc_ref[0, part], e_p, preferred_element_type=F32)
        wp_ref[:, part * SG_STATE:(part + 1) * SG_STATE] = jnp.where(keep, full, 0.0).astype(BF16)

    e_c = spread(CHUNK * SSM_GROUP)
    keep = grp64(iota((SG_STATE, FLAT), 0)) == grp16(iota((SG_STATE, FLAT), 1))
    for part in range(2):
        full = jnp.dot(cm_ref[0, part], e_c, preferred_element_type=F32)
        wc_ref[part * SG_STATE:(part + 1) * SG_STATE, :] = jnp.where(keep, full, 0.0).astype(BF16)

    p_ref[...] = jnp.dot(u_ref[0], wp_ref[...], preferred_element_type=F32)

    ar = ar_ref[0]
    ai = ai_ref[0]

    def body(c, carry):
        new = []
        for b in range(nseq):
            hr, hi = carry[b]
            row = b * steps + c
            pr = p_ref[pl.ds(row, 1), 0:SG_STATE]
            pi = p_ref[pl.ds(row, 1), SG_STATE:2 * SG_STATE]
            p_ref[pl.ds(row, 1), 0:SG_STATE] = hr
            p_ref[pl.ds(row, 1), SG_STATE:2 * SG_STATE] = hi
            new.append((ar * hr - ai * hi + pr, ar * hi + ai * hr + pi))
        return tuple(new)

    z = jnp.zeros((1, SG_STATE), F32)
    lax.fori_loop(0, steps, body, ((z, z),) * nseq, unroll=2)

    tn = scr_ref.shape[0] // CHUNK
    for r0 in range(0, steps * nseq, tn):
        rows = slice(r0, r0 + tn)
        hb = p_ref[rows, :].astype(BF16)
        for jo in range(N_TT):
            cols = slice(jo * MXU_TILE, (jo + 1) * MXU_TILE)
            acc = jnp.dot(hb, wc_ref[:, cols], preferred_element_type=F32)
            for ji in range(jo + 1):
                acc += jnp.dot(u_ref[0, rows, ji * MXU_TILE:(ji + 1) * MXU_TILE], tt_ref[jo - ji],
                               preferred_element_type=F32)
            yg = jax.nn.gelu(acc, approximate=True)
            for e in range(MXU_TILE // LANES):
                j = jo * (MXU_TILE // LANES) + e
                scr_ref[pl.ds(j, tn, stride=CHUNK), :] = yg[:, e * LANES:(e + 1) * LANES]
        y_ref[r0 * CHUNK:(r0 + tn) * CHUNK, :] = scr_ref[...].astype(BF16)


def _ssm(uflat, kc, vc, cm, ar, ai, bsz):
    nc = uflat.shape[1]
    blk = lambda a: pl.BlockSpec((1,) + a.shape[1:], lambda s: (s,) + (0,) * (a.ndim - 1))
    return pl.pallas_call(
        functools.partial(_ssm_kernel, steps=nc // bsz, nseq=bsz),
        grid=(N_SG,),
        in_specs=[blk(uflat), blk(kc), blk(vc), blk(cm), blk(ar), blk(ai)],
        out_specs=pl.BlockSpec((nc * CHUNK, LANES), lambda s: (0, s)),
        out_shape=jax.ShapeDtypeStruct((nc * CHUNK, SSM_WIDTH), BF16),
        scratch_shapes=[pltpu.VMEM((N_TT, MXU_TILE, MXU_TILE), BF16),
                        pltpu.VMEM((FLAT, 2 * SG_STATE), BF16),
                        pltpu.VMEM((2 * SG_STATE, FLAT), BF16),
                        pltpu.VMEM((nc, 2 * SG_STATE), F32),
                        pltpu.VMEM((SSM_Y_ROWS * CHUNK, LANES), F32)],
        compiler_params=_cparams(1),
        name="ssm",
    )(uflat, kc, vc, cm, ar, ai)


ATT_UNITS_IN_FLIGHT = 4
ATT_STEP_QUERIES = 2048


def _attn_kernel(q_ref, kp_ref, kc_ref, vp_ref, vc_ref, o_ref, lse_ref, ks_ref, vs_ref):
    nsb, qb = q_ref.shape[0], q_ref.shape[1]
    blk = ATT_SPAN
    first = pl.program_id(1) == 0

    lane = lax.broadcasted_iota(jnp.int32, (1, ATT_MERGED), 1)
    head_sel = (lane % LANES < ATT_HEAD_DIM, lane % LANES >= ATT_HEAD_DIM)
    for sq in range(nsb):
        for src_p, src_c, dst in ((kp_ref, kc_ref, ks_ref), (vp_ref, vc_ref, vs_ref)):
            for e in range(2):
                xp, xc = src_p[sq], src_c[sq]
                dst[sq, e, 0:blk] = jnp.where(head_sel[e], xp, jnp.zeros_like(xp))
                dst[sq, e, blk:blk + qb] = jnp.where(head_sel[e], xc, jnp.zeros_like(xc))

    qi = lax.broadcasted_iota(jnp.int32, (blk, 2 * blk), 0)
    kk = lax.broadcasted_iota(jnp.int32, (blk, 2 * blk), 1)
    band = jnp.logical_and(kk >= qi, kk <= qi + blk)
    band_first = jnp.logical_and(band, jnp.logical_or(kk >= blk, jnp.logical_not(first)))
    low = lax.broadcasted_iota(jnp.int32, (blk, LANES), 1) < ATT_HEAD_DIM
    dn = (((1,), (1,)), ((), ()))

    units = [(sq, j, pair) for sq in range(nsb) for j in range(qb // blk) for pair in range(ATT_MERGED // LANES)]
    for g0 in range(0, len(units), ATT_UNITS_IN_FLIGHT):
        group = units[g0:g0 + ATT_UNITS_IN_FLIGHT]
        scores = []
        for sq, j, pair in group:
            cols = slice(pair * LANES, (pair + 1) * LANES)
            keys = slice(j * blk, (j + 2) * blk)
            kcat = jnp.concatenate([ks_ref[sq, 0, keys, cols], ks_ref[sq, 1, keys, cols]], axis=0)
            s = lax.dot_general(q_ref[sq, j * blk:(j + 1) * blk, cols], kcat, dn,
                                preferred_element_type=F32)
            mask = band_first if j == 0 else band
            scores.append([jnp.where(mask, s[:, e * 2 * blk:(e + 1) * 2 * blk], NEG_INF) for e in range(2)])
        maxes = [[jnp.max(jnp.maximum(se[:, :blk], se[:, blk:]), axis=-1, keepdims=True) for se in su]
                 for su in scores]
        probs = [[jnp.exp2(se - me) for se, me in zip(su, mu)] for su, mu in zip(scores, maxes)]
        dens = [[jnp.sum(pe[:, :blk] + pe[:, blk:], axis=-1, keepdims=True) for pe in pu] for pu in probs]
        for (sq, j, pair), pu, mu, du in zip(group, probs, maxes, dens):
            cols = slice(pair * LANES, (pair + 1) * LANES)
            rows = slice(j * blk, (j + 1) * blk)
            keys = slice(j * blk, (j + 2) * blk)
            vcat = jnp.concatenate([vs_ref[sq, 0, keys, cols], vs_ref[sq, 1, keys, cols]], axis=0)
            pcat = jnp.concatenate([pu[0].astype(BF16), pu[1].astype(BF16)], axis=1)
            num = jnp.dot(pcat, vcat, preferred_element_type=F32)
            o_ref[sq, rows, cols] = (num * jnp.where(low, 1.0 / du[0], 1.0 / du[1])).astype(BF16)
            lse_ref[sq, rows, cols] = jnp.where(low, mu[0] * LN2 + jnp.log(du[0]), mu[1] * LN2 + jnp.log(du[1]))


def _banded_attention(q, k, v):
    nseq, length, w = q.shape
    blk = ATT_SPAN
    qb = min(length, ATT_STEP_QUERIES)
    nsb = ATT_STEP_QUERIES // qb
    per = qb // blk
    cur = pl.BlockSpec((nsb, qb, w), lambda s, n: (s, n, 0))
    prev = pl.BlockSpec((nsb, blk, w), lambda s, n: (s, jnp.maximum(n * per - 1, 0), 0))
    return pl.pallas_call(
        _attn_kernel,
        grid=(nseq // nsb, length // qb),
        in_specs=[cur, prev, cur, prev, cur],
        out_specs=[cur, cur],
        out_shape=[jax.ShapeDtypeStruct(q.shape, BF16), jax.ShapeDtypeStruct(q.shape, F32)],
        scratch_shapes=[pltpu.VMEM((nsb, 2, blk + qb, w), BF16)] * 2,
        compiler_params=_cparams(2),
        name="attn",
    )(q, k, k, v, v)


def _mix_kernel(y_ref, o0_ref, o1_ref, o2_ref, l0_ref, l1_ref, l2_ref, h_ref, wgate_ref, bgate_ref,
                wglu_ref, bglu_ref, wup_ref, wmix_ref, bmix_ref, g_ref, b_ref, out_ref, scr_ref, acc_ref):
    o_refs = (o0_ref, o1_ref, o2_ref)
    l_refs = (l0_ref, l1_ref, l2_ref)

    def tile(sub, rows):
        def natural(ref, dil):
            n = SUB_ROWS // dil
            if dil == 1:
                return ref[0, 0, rows, :].astype(F32)
            for r in range(dil):
                for e in range(ATT_MERGED // LANES):
                    scr_ref[sub, e, pl.ds(r, n, stride=dil), :] = (
                        ref[0, r, sub * n:(sub + 1) * n, e * LANES:(e + 1) * LANES].astype(F32))
            return jnp.concatenate([scr_ref[sub, e] for e in range(ATT_MERGED // LANES)], axis=-1)

        ls = [natural(l_refs[gi], dil) for gi, dil in enumerate(DILATIONS)]
        m = jnp.maximum(jnp.maximum(ls[0], ls[1]), ls[2])
        es = [jnp.exp(l - m) for l in ls]
        att = es[0] * natural(o_refs[0], DILATIONS[0])
        for gi in (1, 2):
            att += es[gi] * natural(o_refs[gi], DILATIONS[gi])
        att = (att / (es[0] + es[1] + es[2])).astype(BF16)
        y = y_ref[rows, :]
        hb = h_ref[rows, :].astype(BF16)
        yield

        for c in range(D_MODEL // MIX_CHUNK):
            cols = slice(c * MIX_CHUNK, (c + 1) * MIX_CHUNK)
            gate_cols = slice(D_MODEL + c * MIX_CHUNK, D_MODEL + (c + 1) * MIX_CHUNK)
            val = jnp.dot(y, wglu_ref[:, cols], preferred_element_type=F32) + bglu_ref[:, cols]
            gate = jnp.dot(y, wglu_ref[:, gate_cols], preferred_element_type=F32) + bglu_ref[:, gate_cols]
            b_att = jnp.dot(att, wup_ref[:, cols], preferred_element_type=F32)
            g_ssm = jnp.dot(hb, wgate_ref[:, cols], preferred_element_type=F32) + bgate_ref[:, cols]
            g_att = jnp.dot(hb, wgate_ref[:, gate_cols], preferred_element_type=F32) + bgate_ref[:, gate_cols]
            mixed = _sigmoid(g_ssm) * (val * _sigmoid(gate)) + _sigmoid(g_att) * b_att
            acc_ref[rows, cols] = mixed.astype(BF16)
            yield

        r = jnp.dot(acc_ref[rows, :], wmix_ref[...], preferred_element_type=F32) + bmix_ref[...]
        yield
        out_ref[rows, :] = _layer_norm(DEEPNORM_ALPHA * h_ref[rows, :] + r, g_ref[...], b_ref[...])

    _stagger([tile(i, rows) for i, rows in _sub_tiles(h_ref.shape[0])], lag=STAGE_LAG)


def _mix(yg, outs, lses, h, wgate, bgate, wglu, bglu, wup, wmix, bmix, g, b, seq):
    t = h.shape[0]
    tm = 2 * TOKEN_TILE
    per_b = seq // tm
    row = lambda n: pl.BlockSpec((tm, n), lambda i: (i, 0))
    full = lambda a: pl.BlockSpec(a.shape, lambda i: (0,) * a.ndim, pipeline_mode=pl.Buffered(1))
    grp = lambda d: pl.BlockSpec((1, d, tm // d, ATT_MERGED), lambda i: (i // per_b, 0, i % per_b, 0))
    consts = (wgate, bgate, wglu, bglu, wup, wmix, bmix, g, b)
    return pl.pallas_call(
        _mix_kernel,
        grid=(t // tm,),
        in_specs=([row(SSM_WIDTH)] + [grp(d) for d in DILATIONS] * 2 + [row(D_MODEL)]
                  + [full(a) for a in consts]),
        out_specs=row(D_MODEL),
        out_shape=jax.ShapeDtypeStruct((t, D_MODEL), F32),
        scratch_shapes=[pltpu.VMEM((tm // SUB_ROWS, ATT_MERGED // LANES, SUB_ROWS, LANES), F32),
                        pltpu.VMEM((tm, D_MODEL), BF16)],
        compiler_params=_cparams(1),
        name="mix",
    )(yg, *outs, *lses, h, *consts)


def _mem_kv_kernel(m_ref, w_ref, kv_ref):
    kv_ref[0] = jnp.dot(m_ref[0].astype(BF16), w_ref[...], preferred_element_type=F32).astype(BF16)


def _mem_kv(mem, w):
    bsz, ml, _ = mem.shape
    return pl.pallas_call(
        _mem_kv_kernel,
        grid=(bsz,),
        in_specs=[pl.BlockSpec((1, ml, D_MODEL), lambda b: (b, 0, 0)),
                  pl.BlockSpec(w.shape, lambda b: (0, 0))],
        out_specs=pl.BlockSpec((1, ml, 2 * D_MODEL), lambda b: (b, 0, 0)),
        out_shape=jax.ShapeDtypeStruct((bsz, ml, 2 * D_MODEL), BF16),
        compiler_params=_cparams(1),
        name="mem_kv",
    )(mem, w)


def _xattn_kernel(h_ref, kv_ref, wq_ref, wo_ref, g_ref, b_ref, out_ref):
    dn = (((1,), (1,)), ((), ()))

    def tile(sub, rows):
        h = h_ref[rows, :]
        hb = h.astype(BF16)
        yield
        q = (jnp.dot(hb, wq_ref[...], preferred_element_type=F32) * XATT_SCALE).astype(BF16)
        yield
        outs = []
        for hh in range(XATT_HEADS):
            sl = slice(hh * XATT_HEAD_DIM, (hh + 1) * XATT_HEAD_DIM)
            sv = slice(D_MODEL + hh * XATT_HEAD_DIM, D_MODEL + (hh + 1) * XATT_HEAD_DIM)
            s = lax.dot_general(q[:, sl], kv_ref[0, :, sl], dn, preferred_element_type=F32)
            e = jnp.exp(s - jnp.max(s, axis=-1, keepdims=True))
            p = e / jnp.sum(e, axis=-1, keepdims=True)
            outs.append(jnp.dot(p.astype(BF16), kv_ref[0, :, sv], preferred_element_type=F32).astype(BF16))
            yield
        xo = jnp.dot(jnp.concatenate(outs, axis=-1), wo_ref[...], preferred_element_type=F32)
        yield
        out_ref[rows, :] = _layer_norm(DEEPNORM_ALPHA * h + xo, g_ref[...], b_ref[...])

    _stagger([tile(i, rows) for i, rows in _sub_tiles(h_ref.shape[0])], lag=STAGE_LAG)


def _xattn(h, kv, wq, wo, g, b, seq):
    t = h.shape[0]
    tm = 2 * TOKEN_TILE
    per_b = seq // tm
    row = pl.BlockSpec((tm, D_MODEL), lambda i: (i, 0))
    full = lambda a: pl.BlockSpec(a.shape, lambda i: (0,) * a.ndim)
    return pl.pallas_call(
        _xattn_kernel,
        grid=(t // tm,),
        in_specs=[row, pl.BlockSpec((1,) + kv.shape[1:], lambda i: (i // per_b, 0, 0)),
                  full(wq), full(wo), full(g), full(b)],
        out_specs=row,
        out_shape=jax.ShapeDtypeStruct((t, D_MODEL), F32),
        compiler_params=_cparams(1),
        name="xattn",
    )(h, kv, wq, wo, g, b)


def _ffn_kernel(h_ref, w1_ref, b1_ref, w2_ref, b2_ref, g_ref, b_ref, out_ref, acc_ref):
    def tile(sub, rows):
        h = h_ref[rows, :]
        hb = h.astype(BF16)
        yield
        for c in range(D_FF // FFN_CHUNK):
            cols = slice(c * FFN_CHUNK, (c + 1) * FFN_CHUNK)
            a = jnp.dot(hb, w1_ref[:, cols], preferred_element_type=F32) + b1_ref[:, cols]
            a = jnp.square(jnp.maximum(a, 0.0)).astype(BF16)
            part = jnp.dot(a, w2_ref[cols, :], preferred_element_type=F32)
            if c == 0:
                acc_ref[rows, :] = part
            else:
                acc_ref[rows, :] += part
            yield
        ff = acc_ref[rows, :] + b2_ref[...]
        out_ref[rows, :] = _layer_norm(DEEPNORM_ALPHA * h + ff, g_ref[...], b_ref[...])

    _stagger([tile(i, rows) for i, rows in _sub_tiles(h_ref.shape[0])], lag=STAGE_LAG)


def _ffn(h, w1, b1, w2, b2, g, b):
    t = h.shape[0]
    tm = 2 * TOKEN_TILE
    row = pl.BlockSpec((tm, D_MODEL), lambda i: (i, 0))
    full = lambda a: pl.BlockSpec(a.shape, lambda i: (0,) * a.ndim, pipeline_mode=pl.Buffered(1))
    consts = (w1, b1, w2, b2, g, b)
    return pl.pallas_call(
        _ffn_kernel,
        grid=(t // tm,),
        in_specs=[row] + [full(a) for a in consts],
        out_specs=row,
        out_shape=jax.ShapeDtypeStruct((t, D_MODEL), F32),
        scratch_shapes=[pltpu.VMEM((tm, D_MODEL), F32)],
        compiler_params=_cparams(1),
        name="ffn",
    )(h, *consts)


def _rope_selectors():
    half = ROT_DIM // 2
    sel = np.zeros((ROT_DIM, 3 * LANES), np.float32)
    one = np.zeros((1, LANES), np.float32)
    for lane in range(LANES):
        d = lane % ATT_HEAD_DIM
        if d >= ROT_DIM:
            one[0, lane] = 1.0
            continue
        sel[d % half, lane] = 1.0
        if d < half:
            sel[half + d, LANES + lane] = -1.0
        else:
            sel[d, 2 * LANES + lane] = 1.0
    return jnp.asarray(sel, BF16), jnp.asarray(one)


def _rope_cos_sin(positions):
    inv_freq = ROPE_THETA ** (-jnp.arange(0, ROT_DIM, 2, dtype=F32) / ROT_DIM)
    ang = positions.astype(F32).reshape(-1, 1) * inv_freq
    return jnp.concatenate([jnp.cos(ang), jnp.sin(ang)], axis=-1)


def kernel(x, mem, positions, ln_in_g, ln_in_b, w_in, b_in, ssm_log_dt, ssm_a_re, ssm_a_im, ssm_b_re, ssm_b_im, ssm_c_re, ssm_c_im, ssm_d, w_glu, b_glu, w_att_up, w_mix_out, b_mix_out, ln1_g, ln1_b, w_xq, w_xkv, w_xo, ln2_g, ln2_b, w_ff1, b_ff1, w_ff2, b_ff2, ln3_g, ln3_b):
    bsz, seq, _ = x.shape
    t = bsz * seq
    row2 = lambda a: a.reshape(1, -1).astype(F32)
    cs = _rope_cos_sin(positions)
    h = x.reshape(t, D_MODEL)
    for l in range(DEPTH):
        h, uflat, *qkv = _ln_proj(
            h, row2(ln_in_g), row2(ln_in_b), w_in[l, :, :GATE_OFF].astype(BF16), row2(b_in[l, :GATE_OFF]),
            cs, bsz, seq)

        kc, vc, cm, ar, ai = _ssm_matrices(ssm_log_dt[l], ssm_a_re[l], ssm_a_im[l], ssm_b_re[l],
                                           ssm_b_im[l], ssm_c_re[l], ssm_c_im[l], ssm_d[l])
        yg = _ssm(uflat, kc, vc, cm, ar, ai, bsz)

        outs, lses = [], []
        for gi, dil in enumerate(DILATIONS):
            qg, kg, vg = (a.reshape(bsz * dil, seq // dil, ATT_MERGED) for a in qkv[3 * gi:3 * gi + 3])
            o_g, lse_g = _banded_attention(qg, kg, vg)
            outs.append(o_g.reshape(bsz, dil, seq // dil, ATT_MERGED))
            lses.append(lse_g.reshape(bsz, dil, seq // dil, ATT_MERGED))

        h = _mix(yg, outs, lses, h, w_in[l, :, GATE_OFF:].astype(BF16), row2(b_in[l, GATE_OFF:]),
                 w_glu[l].astype(BF16), row2(b_glu[l]),
                 w_att_up[l].astype(BF16), w_mix_out[l].astype(BF16), row2(b_mix_out[l]),
                 row2(ln1_g[l]), row2(ln1_b[l]), seq)

        kv = _mem_kv(mem, w_xkv[l].astype(BF16))
        h = _xattn(h, kv, w_xq[l].astype(BF16), w_xo[l].astype(BF16), row2(ln2_g[l]), row2(ln2_b[l]), seq)
        h = _ffn(h, w_ff1[l].astype(BF16), row2(b_ff1[l]), w_ff2[l].astype(BF16), row2(b_ff2[l]),
                 row2(ln3_g[l]), row2(ln3_b[l]))
    return h.reshape(bsz, seq, D_MODEL)
```

```python
import functools
import math

import jax
import jax.numpy as jnp
import numpy as np
from jax import lax
from jax.experimental import pallas as pl
from jax.experimental.pallas import tpu as pltpu

F32 = jnp.float32
BF16 = jnp.bfloat16

D_MODEL = 1024
SSM_GROUP = 16
SSM_WIDTH = 768
SSM_GROUPS = SSM_WIDTH // SSM_GROUP
SSM_STATE = 64
ATT_HEAD_DIM = 64
ATT_HEADS_PER_GROUP = 4
DILATIONS = (1, 4, 16)
ATT_SPAN = 128
ATT_WIDTH = 768
ATT_MERGED = ATT_HEADS_PER_GROUP * ATT_HEAD_DIM
ATT_SCALE = ATT_HEAD_DIM ** -0.5
ROT_DIM = ATT_HEAD_DIM // 4
ROPE_THETA = 500000.0
XATT_HEADS = 4
XATT_HEAD_DIM = D_MODEL // XATT_HEADS
XATT_SCALE = XATT_HEAD_DIM ** -0.5
D_FF = 4 * D_MODEL
DEPTH = 1
DEEPNORM_ALPHA = (2 * DEPTH) ** 0.25
LN_EPS = 1e-5
NEG_INF = -1e30
LOG2E = math.log2(math.e)
LN2 = math.log(2.0)

LANES = 128
CHUNK = 16
SG_GROUPS = LANES // SSM_GROUP
N_SG = SSM_GROUPS // SG_GROUPS
FLAT = CHUNK * LANES
SG_STATE = SG_GROUPS * SSM_STATE
MXU_TILE = 256
N_TT = FLAT // MXU_TILE

SSM_Y_ROWS = 512
SUB_ROWS = 256
TOKEN_TILE = 512
STAGE_LAG = 1
MIX_CHUNK = 256
FFN_CHUNK = 1024
QKV_OFF = SSM_WIDTH
GATE_OFF = SSM_WIDTH + 3 * ATT_WIDTH
VMEM_LIMIT = 56 * 1024 * 1024


def _cparams(n_axes):
    return pltpu.CompilerParams(dimension_semantics=("parallel",) * n_axes,
                                vmem_limit_bytes=VMEM_LIMIT)


def _layer_norm(x, g, b):
    mu = jnp.mean(x, axis=-1, keepdims=True)
    xc = x - mu
    var = jnp.mean(xc * xc, axis=-1, keepdims=True)
    return xc * lax.rsqrt(var + LN_EPS) * g + b


def _sigmoid(x):
    return 0.5 * jnp.tanh(0.5 * x) + 0.5


def _stagger(tiles, lag):
    tiles = list(tiles)
    live = [True] * len(tiles)
    rnd = 0
    while any(live):
        for i in reversed(range(len(tiles))):
            if live[i] and rnd >= i * lag:
                try:
                    next(tiles[i])
                except StopIteration:
                    live[i] = False
        rnd += 1


def _sub_tiles(tm):
    return [(i, slice(i * SUB_ROWS, (i + 1) * SUB_ROWS)) for i in range(tm // SUB_ROWS)]


def _ln_proj_kernel(x_ref, g_ref, b_ref, w_ref, bi_ref, cs_ref, sel_ref, one_ref,
                    h_ref, u_ref, q0_ref, k0_ref, v0_ref, q1_ref, k1_ref, v1_ref, q2_ref, k2_ref, v2_ref,
                    scr_ref):
    half = ROT_DIM // 2
    qkv_refs = ((q0_ref, k0_ref, v0_ref), (q1_ref, k1_ref, v1_ref), (q2_ref, k2_ref, v2_ref))
    sections = [(0, SSM_WIDTH), (QKV_OFF, ATT_WIDTH), (QKV_OFF + ATT_WIDTH, ATT_WIDTH),
                (QKV_OFF + 2 * ATT_WIDTH, ATT_WIDTH)]

    def tile(sub, rows):
        h = _layer_norm(x_ref[rows, :], g_ref[...], b_ref[...])
        h_ref[rows, :] = h
        hb = h.astype(BF16)
        cs = cs_ref[rows, :]
        cs_hi = cs.astype(BF16)
        cs_lo = (cs - cs_hi.astype(F32)).astype(BF16)
        tab = (jnp.dot(cs_hi, sel_ref[...], preferred_element_type=F32)
               + jnp.dot(cs_lo, sel_ref[...], preferred_element_type=F32))
        cc = tab[:, 0:LANES] + one_ref[...]
        s1 = tab[:, LANES:2 * LANES]
        s2 = tab[:, 2 * LANES:3 * LANES]
        yield

        def stage(off, val):
            for i in range(val.shape[1] // LANES):
                scr_ref[sub, off // LANES + i] = val[:, i * LANES:(i + 1) * LANES]

        def rope(val, scale):
            out = []
            for i in range(ATT_WIDTH // LANES):
                t = val[:, i * LANES:(i + 1) * LANES]
                out.append((t * cc + pltpu.roll(t, LANES - half, 1) * s1 + pltpu.roll(t, half, 1) * s2) * scale)
            return jnp.concatenate(out, axis=-1)

        def emit_groups(a):
            for gi, dil in enumerate(DILATIONS):
                ref = qkv_refs[gi][a]
                cb0 = (QKV_OFF + a * ATT_WIDTH + gi * ATT_MERGED) // LANES
                n = SUB_ROWS // dil
                for r in range(dil):
                    for e in range(ATT_MERGED // LANES):
                        if dil == 1:
                            blk = scr_ref[sub, cb0 + e]
                        else:
                            blk = scr_ref[sub, cb0 + e, pl.ds(r, n, stride=dil), :]
                        ref[0, r, sub * n:(sub + 1) * n, e * LANES:(e + 1) * LANES] = blk.astype(BF16)

        for c, (off, width) in enumerate(sections):
            val = (jnp.dot(hb, w_ref[:, off:off + width], preferred_element_type=F32)
                   + bi_ref[:, off:off + width])
            if c == 0:
                stage(off, val)
                n = SUB_ROWS // CHUNK
                for j in range(CHUNK):
                    for sg in range(N_SG):
                        blk = scr_ref[sub, sg, pl.ds(j, n, stride=CHUNK), :]
                        u_ref[sg, sub * n:(sub + 1) * n, j * LANES:(j + 1) * LANES] = blk.astype(BF16)
            elif c in (1, 2):
                stage(off, rope(val, ATT_SCALE * LOG2E if c == 1 else 1.0))
                emit_groups(c - 1)
            else:
                stage(off, val)
                emit_groups(2)
            yield

    _stagger([tile(i, rows) for i, rows in _sub_tiles(x_ref.shape[0])], lag=STAGE_LAG)


def _ln_proj(x2, g, b, w, bi, cs, bsz, seq):
    t = x2.shape[0]
    tm = 2 * TOKEN_TILE
    per_b = seq // tm
    sel, one = _rope_selectors()
    row = lambda n: pl.BlockSpec((tm, n), lambda i: (i, 0))
    full = lambda a: pl.BlockSpec(a.shape, lambda i: (0,) * a.ndim, pipeline_mode=pl.Buffered(1))
    grp = lambda d: pl.BlockSpec((1, d, tm // d, ATT_MERGED), lambda i: (i // per_b, 0, i % per_b, 0))
    outs = [jax.ShapeDtypeStruct((t, D_MODEL), F32),
            jax.ShapeDtypeStruct((N_SG, t // CHUNK, FLAT), BF16)]
    out_specs = [row(D_MODEL), pl.BlockSpec((N_SG, tm // CHUNK, FLAT), lambda i: (0, i, 0))]
    for d in DILATIONS:
        outs += [jax.ShapeDtypeStruct((bsz, d, seq // d, ATT_MERGED), BF16)] * 3
        out_specs += [grp(d)] * 3
    return pl.pallas_call(
        _ln_proj_kernel,
        grid=(t // tm,),
        in_specs=[row(D_MODEL), full(g), full(b), full(w), full(bi), row(ROT_DIM), full(sel), full(one)],
        out_specs=out_specs,
        out_shape=outs,
        scratch_shapes=[pltpu.VMEM((tm // SUB_ROWS, GATE_OFF // LANES, SUB_ROWS, LANES), F32)],
        compiler_params=_cparams(1),
        name="ln_proj",
    )(x2, g, b, w, bi, cs, sel, one)


def _ssm_matrices(log_dt, a_re, a_im, b_re, b_im, c_re, c_im, d):
    g, n, c = SSM_GROUPS, SSM_STATE, SSM_GROUP
    dt = jnp.exp(log_dt.astype(F32))[:, None]
    a_re = a_re.astype(F32)
    a_im = a_im.astype(F32)
    ks = jnp.arange(CHUNK + 1, dtype=F32)
    mag = jnp.exp((a_re * dt)[..., None] * ks)
    ang = (a_im * dt)[..., None] * ks
    pw_re = mag * jnp.cos(ang)
    pw_im = mag * jnp.sin(ang)
    ab_re, ab_im = pw_re[..., 1], pw_im[..., 1]
    den = jnp.square(a_re) + jnp.square(a_im)
    nr = ab_re - 1.0
    f_re = (nr * a_re + ab_im * a_im) / den
    f_im = (ab_im * a_re - nr * a_im) / den
    b_re = b_re.astype(F32)
    b_im = b_im.astype(F32)
    bb_re = f_re[..., None] * b_re - f_im[..., None] * b_im
    bb_im = f_re[..., None] * b_im + f_im[..., None] * b_re
    c_re = c_re.astype(F32)
    c_im = c_im.astype(F32)
    ca_re = c_re[..., None] * pw_re[:, None] - c_im[..., None] * pw_im[:, None]
    ca_im = c_re[..., None] * pw_im[:, None] + c_im[..., None] * pw_re[:, None]

    kk = (jnp.einsum('gonk,gni->gkio', ca_re[..., :CHUNK], bb_re)
          - jnp.einsum('gonk,gni->gkio', ca_im[..., :CHUNK], bb_im))
    lag0_diag = (np.arange(CHUNK)[:, None, None] == 0) & np.eye(c, dtype=bool)[None]
    kk = kk + jnp.where(lag0_diag[None], d.astype(F32).reshape(g, 1, c, 1), 0.0)
    kc = jnp.transpose(kk.reshape(N_SG, SG_GROUPS, CHUNK, c, c), (0, 2, 1, 3, 4)).reshape(N_SG, CHUNK, LANES, c)

    rp_re = pw_re[..., CHUNK - 1::-1][..., :CHUNK]
    rp_im = pw_im[..., CHUNK - 1::-1][..., :CHUNK]
    v_re = jnp.einsum('gns,gni->gsin', rp_re, bb_re) - jnp.einsum('gns,gni->gsin', rp_im, bb_im)
    v_im = jnp.einsum('gns,gni->gsin', rp_re, bb_im) + jnp.einsum('gns,gni->gsin', rp_im, bb_re)

    def rows_p(v):
        return jnp.transpose(v.reshape(N_SG, SG_GROUPS, CHUNK, c, n), (0, 2, 1, 3, 4)).reshape(N_SG, FLAT, n)

    vc = jnp.stack([rows_p(v_re), rows_p(v_im)], axis=1)

    def rows_c(m):
        return jnp.transpose(m[..., 1:], (0, 2, 3, 1)).reshape(N_SG, SG_STATE, CHUNK * c)

    cm = jnp.stack([rows_c(ca_re), rows_c(-ca_im)], axis=1)

    ar = pw_re[..., CHUNK].reshape(N_SG, 1, SG_STATE)
    ai = pw_im[..., CHUNK].reshape(N_SG, 1, SG_STATE)
    return kc.astype(BF16), vc.astype(BF16), cm.astype(BF16), ar, ai


def _ssm_kernel(u_ref, kc_ref, vc_ref, cm_ref, ar_ref, ai_ref, y_ref,
                tt_ref, wp_ref, wc_ref, p_ref, scr_ref, *, steps, nseq):
    def iota(shape, dim):
        return lax.broadcasted_iota(jnp.int32, shape, dim)

    def spread(nrows):
        r, cidx = iota((nrows, nrows * SG_GROUPS), 0), iota((nrows, nrows * SG_GROUPS), 1)
        return jnp.where(((r >> 4) == (cidx >> 7)) & ((r & 15) == (cidx & 15)), 1.0, 0.0).astype(BF16)

    grp16 = lambda idx: (idx >> 4) & (SG_GROUPS - 1)
    grp64 = lambda idx: (idx >> 6) & (SG_GROUPS - 1)

    e_t = spread(SSM_GROUP)
    keep = grp16(iota((LANES, LANES), 0)) == grp16(iota((LANES, LANES), 1))
    lag_blocks = [jnp.where(keep, jnp.dot(kc_ref[0, k], e_t, preferred_element_type=F32), 0.0).astype(BF16)
                  for k in range(CHUNK)]
    for dlt in range(N_TT):
        for s in range(2):
            for t in range(2):
                k = 2 * dlt + t - s
                blk = lag_blocks[k] if k >= 0 else jnp.zeros((LANES, LANES), BF16)
                tt_ref[dlt, s * LANES:(s + 1) * LANES, t * LANES:(t + 1) * LANES] = blk

    r, cidx = iota((SSM_STATE, SG_STATE), 0), iota((SSM_STATE, SG_STATE), 1)
    e_p = jnp.where(r == (cidx & (SSM_STATE - 1)), 1.0, 0.0).astype(BF16)
    keep = grp16(iota((FLAT, SG_STATE), 0)) == grp64(iota((FLAT, SG_STATE), 1))
    for part in range(2):
        full = jnp.dot(vc_ref[0, part], e_p, preferred_element_type=F32)
        wp_ref[:, part * SG_STATE:(part + 1) * SG_STATE] = jnp.where(keep, full, 0.0).astype(BF16)

    e_c = spread(CHUNK * SSM_GROUP)
    keep = grp64(iota((SG_STATE, FLAT), 0)) == grp16(iota((SG_STATE, FLAT), 1))
    for part in range(2):
        full = jnp.dot(cm_ref[0, part], e_c, preferred_element_type=F32)
        wc_ref[part * SG_STATE:(part + 1) * SG_STATE, :] = jnp.where(keep, full, 0.0).astype(BF16)

    p_ref[...] = jnp.dot(u_ref[0], wp_ref[...], preferred_element_type=F32)

    ar = ar_ref[0]
    ai = ai_ref[0]

    def body(c, carry):
        new = []
        for b in range(nseq):
            hr, hi = carry[b]
            row = b * steps + c
            pr = p_ref[pl.ds(row, 1), 0:SG_STATE]
            pi = p_ref[pl.ds(row, 1), SG_STATE:2 * SG_STATE]
            p_ref[pl.ds(row, 1), 0:SG_STATE] = hr
            p_ref[pl.ds(row, 1), SG_STATE:2 * SG_STATE] = hi
            new.append((ar * hr - ai * hi + pr, ar * hi + ai * hr + pi))
        return tuple(new)

    z = jnp.zeros((1, SG_STATE), F32)
    lax.fori_loop(0, steps, body, ((z, z),) * nseq, unroll=2)

    tn = scr_ref.shape[0] // CHUNK
    for r0 in range(0, steps * nseq, tn):
        rows = slice(r0, r0 + tn)
        hb = p_ref[rows, :].astype(BF16)
        for jo in range(N_TT):
            cols = slice(jo * MXU_TILE, (jo + 1) * MXU_TILE)
            acc = jnp.dot(hb, wc_ref[:, cols], preferred_element_type=F32)
            for ji in range(jo + 1):
                acc += jnp.dot(u_ref[0, rows, ji * MXU_TILE:(ji + 1) * MXU_TILE], tt_ref[jo - ji],
                               preferred_element_type=F32)
            yg = jax.nn.gelu(acc, approximate=True)
            for e in range(MXU_TILE // LANES):
                j = jo * (MXU_TILE // LANES) + e
                scr_ref[pl.ds(j, tn, stride=CHUNK), :] = yg[:, e * LANES:(e + 1) * LANES]
        y_ref[r0 * CHUNK:(r0 + tn) * CHUNK, :] = scr_ref[...].astype(BF16)


def _ssm(uflat, kc, vc, cm, ar, ai, bsz):
    nc = uflat.shape[1]
    blk = lambda a: pl.BlockSpec((1,) + a.shape[1:], lambda s: (s,) + (0,) * (a.ndim - 1))
    return pl.pallas_call(
        functools.partial(_ssm_kernel, steps=nc // bsz, nseq=bsz),
        grid=(N_SG,),
        in_specs=[blk(uflat), blk(kc), blk(vc), blk(cm), blk(ar), blk(ai)],
        out_specs=pl.BlockSpec((nc * CHUNK, LANES), lambda s: (0, s)),
        out_shape=jax.ShapeDtypeStruct((nc * CHUNK, SSM_WIDTH), BF16),
        scratch_shapes=[pltpu.VMEM((N_TT, MXU_TILE, MXU_TILE), BF16),
                        pltpu.VMEM((FLAT, 2 * SG_STATE), BF16),
                        pltpu.VMEM((2 * SG_STATE, FLAT), BF16),
                        pltpu.VMEM((nc, 2 * SG_STATE), F32),
                        pltpu.VMEM((SSM_Y_ROWS * CHUNK, LANES), F32)],
        compiler_params=_cparams(1),
        name="ssm",
    )(uflat, kc, vc, cm, ar, ai)


ATT_UNITS_IN_FLIGHT = 4
ATT_STEP_QUERIES = 2048


def _attn_kernel(q_ref, kp_ref, kc_ref, vp_ref, vc_ref, o_ref, lse_ref, ks_ref, vs_ref):
    nsb, qb = q_ref.shape[0], q_ref.shape[1]
    blk = ATT_SPAN
    first = pl.program_id(1) == 0

    lane = lax.broadcasted_iota(jnp.int32, (1, ATT_MERGED), 1)
    head_sel = (lane % LANES < ATT_HEAD_DIM, lane % LANES >= ATT_HEAD_DIM)
    for sq in range(nsb):
        for src_p, src_c, dst in ((kp_ref, kc_ref, ks_ref), (vp_ref, vc_ref, vs_ref)):
            for e in range(2):
                xp, xc = src_p[sq], src_c[sq]
                dst[sq, e, 0:blk] = jnp.where(head_sel[e], xp, jnp.zeros_like(xp))
                dst[sq, e, blk:blk + qb] = jnp.where(head_sel[e], xc, jnp.zeros_like(xc))

    qi = lax.broadcasted_iota(jnp.int32, (blk, 2 * blk), 0)
    kk = lax.broadcasted_iota(jnp.int32, (blk, 2 * blk), 1)
    band = jnp.logical_and(kk >= qi, kk <= qi + blk)
    band_first = jnp.logical_and(band, jnp.logical_or(kk >= blk, jnp.logical_not(first)))
    low = lax.broadcasted_iota(jnp.int32, (blk, LANES), 1) < ATT_HEAD_DIM
    dn = (((1,), (1,)), ((), ()))

    units = [(sq, j, pair) for sq in range(nsb) for j in range(qb // blk) for pair in range(ATT_MERGED // LANES)]
    for g0 in range(0, len(units), ATT_UNITS_IN_FLIGHT):
        group = units[g0:g0 + ATT_UNITS_IN_FLIGHT]
        scores = []
        for sq, j, pair in group:
            cols = slice(pair * LANES, (pair + 1) * LANES)
            keys = slice(j * blk, (j + 2) * blk)
            kcat = jnp.concatenate([ks_ref[sq, 0, keys, cols], ks_ref[sq, 1, keys, cols]], axis=0)
            s = lax.dot_general(q_ref[sq, j * blk:(j + 1) * blk, cols], kcat, dn,
                                preferred_element_type=F32)
            mask = band_first if j == 0 else band
            scores.append([jnp.where(mask, s[:, e * 2 * blk:(e + 1) * 2 * blk], NEG_INF) for e in range(2)])
        maxes = [[jnp.max(jnp.maximum(se[:, :blk], se[:, blk:]), axis=-1, keepdims=True) for se in su]
                 for su in scores]
        probs = [[jnp.exp2(se - me) for se, me in zip(su, mu)] for su, mu in zip(scores, maxes)]
        dens = [[jnp.sum(pe[:, :blk] + pe[:, blk:], axis=-1, keepdims=True) for pe in pu] for pu in probs]
        for (sq, j, pair), pu, mu, du in zip(group, probs, maxes, dens):
            cols = slice(pair * LANES, (pair + 1) * LANES)
            rows = slice(j * blk, (j + 1) * blk)
            keys = slice(j * blk, (j + 2) * blk)
            vcat = jnp.concatenate([vs_ref[sq, 0, keys, cols], vs_ref[sq, 1, keys, cols]], axis=0)
            pcat = jnp.concatenate([pu[0].astype(BF16), pu[1].astype(BF16)], axis=1)
            num = jnp.dot(pcat, vcat, preferred_element_type=F32)
            o_ref[sq, rows, cols] = (num * jnp.where(low, 1.0 / du[0], 1.0 / du[1])).astype(BF16)
            lse_ref[sq, rows, cols] = jnp.where(low, mu[0] * LN2 + jnp.log(du[0]), mu[1] * LN2 + jnp.log(du[1]))


def _banded_attention(q, k, v):
    nseq, length, w = q.shape
    blk = ATT_SPAN
    qb = min(length, ATT_STEP_QUERIES)
    nsb = ATT_STEP_QUERIES // qb
    per = qb // blk
    cur = pl.BlockSpec((nsb, qb, w), lambda s, n: (s, n, 0))
    prev = pl.BlockSpec((nsb, blk, w), lambda s, n: (s, jnp.maximum(n * per - 1, 0), 0))
    return pl.pallas_call(
        _attn_kernel,
        grid=(nseq // nsb, length // qb),
        in_specs=[cur, prev, cur, prev, cur],
        out_specs=[cur, cur],
        out_shape=[jax.ShapeDtypeStruct(q.shape, BF16), jax.ShapeDtypeStruct(q.shape, F32)],
        scratch_shapes=[pltpu.VMEM((nsb, 2, blk + qb, w), BF16)] * 2,
        compiler_params=_cparams(2),
        name="attn",
    )(q, k, k, v, v)


def _mix_kernel(y_ref, o0_ref, o1_ref, o2_ref, l0_ref, l1_ref, l2_ref, h_ref, wgate_ref, bgate_ref,
                wglu_ref, bglu_ref, wup_ref, wmix_ref, bmix_ref, g_ref, b_ref, out_ref, scr_ref, acc_ref):
    o_refs = (o0_ref, o1_ref, o2_ref)
    l_refs = (l0_ref, l1_ref, l2_ref)

    def tile(sub, rows):
        def natural(ref, dil):
            n = SUB_ROWS // dil
            if dil == 1:
                return ref[0, 0, rows, :].astype(F32)
            for r in range(dil):
                for e in range(ATT_MERGED // LANES):
                    scr_ref[sub, e, pl.ds(r, n, stride=dil), :] = (
                        ref[0, r, sub * n:(sub + 1) * n, e * LANES:(e + 1) * LANES].astype(F32))
            return jnp.concatenate([scr_ref[sub, e] for e in range(ATT_MERGED // LANES)], axis=-1)

        ls = [natural(l_refs[gi], dil) for gi, dil in enumerate(DILATIONS)]
        m = jnp.maximum(jnp.maximum(ls[0], ls[1]), ls[2])
        es = [jnp.exp(l - m) for l in ls]
        att = es[0] * natural(o_refs[0], DILATIONS[0])
        for gi in (1, 2):
            att += es[gi] * natural(o_refs[gi], DILATIONS[gi])
        att = (att / (es[0] + es[1] + es[2])).astype(BF16)
        y = y_ref[rows, :]
        hb = h_ref[rows, :].astype(BF16)
        yield

        for c in range(D_MODEL // MIX_CHUNK):
            cols = slice(c * MIX_CHUNK, (c + 1) * MIX_CHUNK)
            gate_cols = slice(D_MODEL + c * MIX_CHUNK, D_MODEL + (c + 1) * MIX_CHUNK)
            val = jnp.dot(y, wglu_ref[:, cols], preferred_element_type=F32) + bglu_ref[:, cols]
            gate = jnp.dot(y, wglu_ref[:, gate_cols], preferred_element_type=F32) + bglu_ref[:, gate_cols]
            b_att = jnp.dot(att, wup_ref[:, cols], preferred_element_type=F32)
            g_ssm = jnp.dot(hb, wgate_ref[:, cols], preferred_element_type=F32) + bgate_ref[:, cols]
            g_att = jnp.dot(hb, wgate_ref[:, gate_cols], preferred_element_type=F32) + bgate_ref[:, gate_cols]
            mixed = _sigmoid(g_ssm) * (val * _sigmoid(gate)) + _sigmoid(g_att) * b_att
            acc_ref[rows, cols] = mixed.astype(BF16)
            yield

        r = jnp.dot(acc_ref[rows, :], wmix_ref[...], preferred_element_type=F32) + bmix_ref[...]
        yield
        out_ref[rows, :] = _layer_norm(DEEPNORM_ALPHA * h_ref[rows, :] + r, g_ref[...], b_ref[...])

    _stagger([tile(i, rows) for i, rows in _sub_tiles(h_ref.shape[0])], lag=STAGE_LAG)


def _mix(yg, outs, lses, h, wgate, bgate, wglu, bglu, wup, wmix, bmix, g, b, seq):
    t = h.shape[0]
    tm = 2 * TOKEN_TILE
    per_b = seq // tm
    row = lambda n: pl.BlockSpec((tm, n), lambda i: (i, 0))
    full = lambda a: pl.BlockSpec(a.shape, lambda i: (0,) * a.ndim, pipeline_mode=pl.Buffered(1))
    grp = lambda d: pl.BlockSpec((1, d, tm // d, ATT_MERGED), lambda i: (i // per_b, 0, i % per_b, 0))
    consts = (wgate, bgate, wglu, bglu, wup, wmix, bmix, g, b)
    return pl.pallas_call(
        _mix_kernel,
        grid=(t // tm,),
        in_specs=([row(SSM_WIDTH)] + [grp(d) for d in DILATIONS] * 2 + [row(D_MODEL)]
                  + [full(a) for a in consts]),
        out_specs=row(D_MODEL),
        out_shape=jax.ShapeDtypeStruct((t, D_MODEL), F32),
        scratch_shapes=[pltpu.VMEM((tm // SUB_ROWS, ATT_MERGED // LANES, SUB_ROWS, LANES), F32),
                        pltpu.VMEM((tm, D_MODEL), BF16)],
        compiler_params=_cparams(1),
        name="mix",
    )(yg, *outs, *lses, h, *consts)


def _tail_kernel(h_ref, mem_ref, wkv_ref, wq_ref, wo_ref, g2_ref, b2_ref,
                 w1_ref, bf1_ref, w2_ref, bf2_ref, g3_ref, b3_ref, out_ref, kv_ref, acc_ref, *, steps_per_batch):
    dn = (((1,), (1,)), ((), ()))

    @pl.when(pl.program_id(0) % steps_per_batch == 0)
    def _():
        kv_ref[...] = jnp.dot(mem_ref[0].astype(BF16), wkv_ref[...], preferred_element_type=F32).astype(BF16)

    def tile(sub, rows):
        h = h_ref[rows, :]
        hb = h.astype(BF16)
        yield
        q = (jnp.dot(hb, wq_ref[...], preferred_element_type=F32) * XATT_SCALE).astype(BF16)
        yield
        outs = []
        for hh in range(XATT_HEADS):
            sl = slice(hh * XATT_HEAD_DIM, (hh + 1) * XATT_HEAD_DIM)
            sv = slice(D_MODEL + hh * XATT_HEAD_DIM, D_MODEL + (hh + 1) * XATT_HEAD_DIM)
            s = lax.dot_general(q[:, sl], kv_ref[:, sl], dn, preferred_element_type=F32)
            e = jnp.exp(s - jnp.max(s, axis=-1, keepdims=True))
            p = e / jnp.sum(e, axis=-1, keepdims=True)
            outs.append(jnp.dot(p.astype(BF16), kv_ref[:, sv], preferred_element_type=F32).astype(BF16))
            yield
        xo = jnp.dot(jnp.concatenate(outs, axis=-1), wo_ref[...], preferred_element_type=F32)
        yield
        h = _layer_norm(DEEPNORM_ALPHA * h + xo, g2_ref[...], b2_ref[...])
        hb = h.astype(BF16)
        yield
        for c in range(D_FF // FFN_CHUNK):
            cols = slice(c * FFN_CHUNK, (c + 1) * FFN_CHUNK)
            a = jnp.dot(hb, w1_ref[:, cols], preferred_element_type=F32) + bf1_ref[:, cols]
            a = jnp.square(jnp.maximum(a, 0.0)).astype(BF16)
            part = jnp.dot(a, w2_ref[cols, :], preferred_element_type=F32)
            if c == 0:
                acc_ref[rows, :] = part
            else:
                acc_ref[rows, :] += part
            yield
        ff = acc_ref[rows, :] + bf2_ref[...]
        out_ref[rows, :] = _layer_norm(DEEPNORM_ALPHA * h + ff, g3_ref[...], b3_ref[...])

    _stagger([tile(i, rows) for i, rows in _sub_tiles(h_ref.shape[0])], lag=STAGE_LAG)


def _tail(h, mem, wkv, wq, wo, g2, b2, w1, bf1, w2, bf2, g3, b3, seq):
    t = h.shape[0]
    tm = 2 * TOKEN_TILE
    per_b = seq // tm
    row = pl.BlockSpec((tm, D_MODEL), lambda i: (i, 0))
    full = lambda a: pl.BlockSpec(a.shape, lambda i: (0,) * a.ndim, pipeline_mode=pl.Buffered(1))
    consts = (wkv, wq, wo, g2, b2, w1, bf1, w2, bf2, g3, b3)
    return pl.pallas_call(
        functools.partial(_tail_kernel, steps_per_batch=per_b),
        grid=(t // tm,),
        in_specs=[row, pl.BlockSpec((1,) + mem.shape[1:], lambda i: (i // per_b, 0, 0))] + [full(a) for a in consts],
        out_specs=row,
        out_shape=jax.ShapeDtypeStruct((t, D_MODEL), F32),
        scratch_shapes=[pltpu.VMEM((mem.shape[1], 2 * D_MODEL), BF16), pltpu.VMEM((tm, D_MODEL), F32)],
        compiler_params=pltpu.CompilerParams(dimension_semantics=("arbitrary",), vmem_limit_bytes=VMEM_LIMIT),
        name="tail",
    )(h, mem, *consts)


def _rope_selectors():
    half = ROT_DIM // 2
    sel = np.zeros((ROT_DIM, 3 * LANES), np.float32)
    one = np.zeros((1, LANES), np.float32)
    for lane in range(LANES):
        d = lane % ATT_HEAD_DIM
        if d >= ROT_DIM:
            one[0, lane] = 1.0
            continue
        sel[d % half, lane] = 1.0
        if d < half:
            sel[half + d, LANES + lane] = -1.0
        else:
            sel[d, 2 * LANES + lane] = 1.0
    return jnp.asarray(sel, BF16), jnp.asarray(one)


def _rope_cos_sin(positions):
    inv_freq = ROPE_THETA ** (-jnp.arange(0, ROT_DIM, 2, dtype=F32) / ROT_DIM)
    ang = positions.astype(F32).reshape(-1, 1) * inv_freq
    return jnp.concatenate([jnp.cos(ang), jnp.sin(ang)], axis=-1)


def kernel(x, mem, positions, ln_in_g, ln_in_b, w_in, b_in, ssm_log_dt, ssm_a_re, ssm_a_im, ssm_b_re, ssm_b_im, ssm_c_re, ssm_c_im, ssm_d, w_glu, b_glu, w_att_up, w_mix_out, b_mix_out, ln1_g, ln1_b, w_xq, w_xkv, w_xo, ln2_g, ln2_b, w_ff1, b_ff1, w_ff2, b_ff2, ln3_g, ln3_b):
    bsz, seq, _ = x.shape
    t = bsz * seq
    row2 = lambda a: a.reshape(1, -1).astype(F32)
    cs = _rope_cos_sin(positions)
    h = x.reshape(t, D_MODEL)
    for l in range(DEPTH):
        h, uflat, *qkv = _ln_proj(
            h, row2(ln_in_g), row2(ln_in_b), w_in[l, :, :GATE_OFF].astype(BF16), row2(b_in[l, :GATE_OFF]),
            cs, bsz, seq)

        kc, vc, cm, ar, ai = _ssm_matrices(ssm_log_dt[l], ssm_a_re[l], ssm_a_im[l], ssm_b_re[l],
                                           ssm_b_im[l], ssm_c_re[l], ssm_c_im[l], ssm_d[l])
        yg = _ssm(uflat, kc, vc, cm, ar, ai, bsz)

        outs, lses = [], []
        for gi, dil in enumerate(DILATIONS):
            qg, kg, vg = (a.reshape(bsz * dil, seq // dil, ATT_MERGED) for a in qkv[3 * gi:3 * gi + 3])
            o_g, lse_g = _banded_attention(qg, kg, vg)
            outs.append(o_g.reshape(bsz, dil, seq // dil, ATT_MERGED))
            lses.append(lse_g.reshape(bsz, dil, seq // dil, ATT_MERGED))

        h = _mix(yg, outs, lses, h, w_in[l, :, GATE_OFF:].astype(BF16), row2(b_in[l, GATE_OFF:]),
                 w_glu[l].astype(BF16), row2(b_glu[l]),
                 w_att_up[l].astype(BF16), w_mix_out[l].astype(BF16), row2(b_mix_out[l]),
                 row2(ln1_g[l]), row2(ln1_b[l]), seq)

        h = _tail(h, mem, w_xkv[l].astype(BF16), w_xq[l].astype(BF16), w_xo[l].astype(BF16),
                  row2(ln2_g[l]), row2(ln2_b[l]), w_ff1[l].astype(BF16), row2(b_ff1[l]),
                  w_ff2[l].astype(BF16), row2(b_ff2[l]), row2(ln3_g[l]), row2(ln3_b[l]), seq)
    return h.reshape(bsz, seq, D_MODEL)
```

```python
import functools
import math

import jax
import jax.numpy as jnp
import numpy as np
from jax import lax
from jax.experimental import pallas as pl
from jax.experimental.pallas import tpu as pltpu

F32 = jnp.float32
BF16 = jnp.bfloat16

D_MODEL = 1024
SSM_GROUP = 16
SSM_WIDTH = 768
SSM_GROUPS = SSM_WIDTH // SSM_GROUP
SSM_STATE = 64
ATT_HEAD_DIM = 64
ATT_HEADS_PER_GROUP = 4
DILATIONS = (1, 4, 16)
ATT_SPAN = 128
ATT_WIDTH = 768
ATT_MERGED = ATT_HEADS_PER_GROUP * ATT_HEAD_DIM
ATT_SCALE = ATT_HEAD_DIM ** -0.5
ROT_DIM = ATT_HEAD_DIM // 4
ROPE_THETA = 500000.0
XATT_HEADS = 4
XATT_HEAD_DIM = D_MODEL // XATT_HEADS
XATT_SCALE = XATT_HEAD_DIM ** -0.5
D_FF = 4 * D_MODEL
DEPTH = 1
DEEPNORM_ALPHA = (2 * DEPTH) ** 0.25
LN_EPS = 1e-5
NEG_INF = -1e30
LOG2E = math.log2(math.e)
LN2 = math.log(2.0)

LANES = 128
CHUNK = 16
SG_GROUPS = LANES // SSM_GROUP
N_SG = SSM_GROUPS // SG_GROUPS
FLAT = CHUNK * LANES
SG_STATE = SG_GROUPS * SSM_STATE
MXU_TILE = 256
N_TT = FLAT // MXU_TILE

SSM_Y_ROWS = 512
SUB_ROWS = 256
TOKEN_TILE = 512
STAGE_LAG = 1
MIX_CHUNK = 256
FFN_CHUNK = 1024
QKV_OFF = SSM_WIDTH
GATE_OFF = SSM_WIDTH + 3 * ATT_WIDTH
VMEM_LIMIT = 56 * 1024 * 1024


def _cparams(n_axes):
    return pltpu.CompilerParams(dimension_semantics=("parallel",) * n_axes,
                                vmem_limit_bytes=VMEM_LIMIT)


def _layer_norm(x, g, b):
    mu = jnp.mean(x, axis=-1, keepdims=True)
    xc = x - mu
    var = jnp.mean(xc * xc, axis=-1, keepdims=True)
    return xc * lax.rsqrt(var + LN_EPS) * g + b


def _sigmoid(x):
    return 0.5 * jnp.tanh(0.5 * x) + 0.5


def _stagger(tiles, lag):
    tiles = list(tiles)
    live = [True] * len(tiles)
    rnd = 0
    while any(live):
        for i in reversed(range(len(tiles))):
            if live[i] and rnd >= i * lag:
                try:
                    next(tiles[i])
                except StopIteration:
                    live[i] = False
        rnd += 1


def _sub_tiles(tm):
    return [(i, slice(i * SUB_ROWS, (i + 1) * SUB_ROWS)) for i in range(tm // SUB_ROWS)]


def _ln_proj_kernel(x_ref, g_ref, b_ref, w_ref, bi_ref, cs_ref, sel_ref, one_ref,
                    h_ref, u_ref, q0_ref, k0_ref, v0_ref, q1_ref, k1_ref, v1_ref, q2_ref, k2_ref, v2_ref,
                    scr_ref):
    half = ROT_DIM // 2
    qkv_refs = ((q0_ref, k0_ref, v0_ref), (q1_ref, k1_ref, v1_ref), (q2_ref, k2_ref, v2_ref))
    sections = [(0, SSM_WIDTH), (QKV_OFF, ATT_WIDTH), (QKV_OFF + ATT_WIDTH, ATT_WIDTH),
                (QKV_OFF + 2 * ATT_WIDTH, ATT_WIDTH)]

    def tile(sub, rows):
        h = _layer_norm(x_ref[rows, :], g_ref[...], b_ref[...])
        h_ref[rows, :] = h
        hb = h.astype(BF16)
        cs = cs_ref[rows, :]
        cs_hi = cs.astype(BF16)
        cs_lo = (cs - cs_hi.astype(F32)).astype(BF16)
        tab = (jnp.dot(cs_hi, sel_ref[...], preferred_element_type=F32)
               + jnp.dot(cs_lo, sel_ref[...], preferred_element_type=F32))
        cc = tab[:, 0:LANES] + one_ref[...]
        s1 = tab[:, LANES:2 * LANES]
        s2 = tab[:, 2 * LANES:3 * LANES]
        yield

        def stage(off, val):
            for i in range(val.shape[1] // LANES):
                scr_ref[sub, off // LANES + i] = val[:, i * LANES:(i + 1) * LANES]

        def rope(val, scale):
            out = []
            for i in range(ATT_WIDTH // LANES):
                t = val[:, i * LANES:(i + 1) * LANES]
                out.append((t * cc + pltpu.roll(t, LANES - half, 1) * s1 + pltpu.roll(t, half, 1) * s2) * scale)
            return jnp.concatenate(out, axis=-1)

        def emit_groups(a):
            for gi, dil in enumerate(DILATIONS):
                ref = qkv_refs[gi][a]
                cb0 = (QKV_OFF + a * ATT_WIDTH + gi * ATT_MERGED) // LANES
                n = SUB_ROWS // dil
                for r in range(dil):
                    for e in range(ATT_MERGED // LANES):
                        if dil == 1:
                            blk = scr_ref[sub, cb0 + e]
                        else:
                            blk = scr_ref[sub, cb0 + e, pl.ds(r, n, stride=dil), :]
                        ref[0, r, sub * n:(sub + 1) * n, e * LANES:(e + 1) * LANES] = blk.astype(BF16)

        for c, (off, width) in enumerate(sections):
            val = (jnp.dot(hb, w_ref[:, off:off + width], preferred_element_type=F32)
                   + bi_ref[:, off:off + width])
            if c == 0:
                stage(off, val)
                n = SUB_ROWS // CHUNK
                for j in range(CHUNK):
                    for sg in range(N_SG):
                        blk = scr_ref[sub, sg, pl.ds(j, n, stride=CHUNK), :]
                        u_ref[sg, sub * n:(sub + 1) * n, j * LANES:(j + 1) * LANES] = blk.astype(BF16)
            elif c in (1, 2):
                stage(off, rope(val, ATT_SCALE * LOG2E if c == 1 else 1.0))
                emit_groups(c - 1)
            else:
                stage(off, val)
                emit_groups(2)
            yield

    _stagger([tile(i, rows) for i, rows in _sub_tiles(x_ref.shape[0])], lag=STAGE_LAG)


def _ln_proj(x2, g, b, w, bi, cs, bsz, seq):
    t = x2.shape[0]
    tm = 2 * TOKEN_TILE
    per_b = seq // tm
    sel, one = _rope_selectors()
    row = lambda n: pl.BlockSpec((tm, n), lambda i: (i, 0))
    full = lambda a: pl.BlockSpec(a.shape, lambda i: (0,) * a.ndim, pipeline_mode=pl.Buffered(1))
    grp = lambda d: pl.BlockSpec((1, d, tm // d, ATT_MERGED), lambda i: (i // per_b, 0, i % per_b, 0))
    outs = [jax.ShapeDtypeStruct((t, D_MODEL), F32),
            jax.ShapeDtypeStruct((N_SG, t // CHUNK, FLAT), BF16)]
    out_specs = [row(D_MODEL), pl.BlockSpec((N_SG, tm // CHUNK, FLAT), lambda i: (0, i, 0))]
    for d in DILATIONS:
        outs += [jax.ShapeDtypeStruct((bsz, d, seq // d, ATT_MERGED), BF16)] * 3
        out_specs += [grp(d)] * 3
    return pl.pallas_call(
        _ln_proj_kernel,
        grid=(t // tm,),
        in_specs=[row(D_MODEL), full(g), full(b), full(w), full(bi), row(ROT_DIM), full(sel), full(one)],
        out_specs=out_specs,
        out_shape=outs,
        scratch_shapes=[pltpu.VMEM((tm // SUB_ROWS, GATE_OFF // LANES, SUB_ROWS, LANES), F32)],
        compiler_params=_cparams(1),
        name="ln_proj",
    )(x2, g, b, w, bi, cs, sel, one)


def _ssm_matrices(log_dt, a_re, a_im, b_re, b_im, c_re, c_im, d):
    g, n, c = SSM_GROUPS, SSM_STATE, SSM_GROUP
    dt = jnp.exp(log_dt.astype(F32))[:, None]
    a_re = a_re.astype(F32)
    a_im = a_im.astype(F32)
    ks = jnp.arange(CHUNK + 1, dtype=F32)
    mag = jnp.exp((a_re * dt)[..., None] * ks)
    ang = (a_im * dt)[..., None] * ks
    pw_re = mag * jnp.cos(ang)
    pw_im = mag * jnp.sin(ang)
    ab_re, ab_im = pw_re[..., 1], pw_im[..., 1]
    den = jnp.square(a_re) + jnp.square(a_im)
    nr = ab_re - 1.0
    f_re = (nr * a_re + ab_im * a_im) / den
    f_im = (ab_im * a_re - nr * a_im) / den
    b_re = b_re.astype(F32)
    b_im = b_im.astype(F32)
    bb_re = f_re[..., None] * b_re - f_im[..., None] * b_im
    bb_im = f_re[..., None] * b_im + f_im[..., None] * b_re
    pw = jnp.stack([pw_re, pw_im])
    pwrep = jnp.transpose(pw.reshape(2, N_SG, SG_GROUPS, n, CHUNK + 1), (1, 0, 4, 2, 3))
    pwrep = jnp.broadcast_to(pwrep[:, :, :, :, None, :], (N_SG, 2, CHUNK + 1, SG_GROUPS, c, n))
    pwrep = pwrep.reshape(N_SG, 2, CHUNK + 1, LANES, n)
    pwcol = jnp.transpose(pw.reshape(2, N_SG, SG_STATE, CHUNK + 1), (1, 0, 2, 3))
    bbt = jnp.transpose(jnp.stack([bb_re, bb_im]), (0, 1, 3, 2)).reshape(2, N_SG, LANES, n).transpose(1, 0, 2, 3)
    cs = jnp.transpose(jnp.stack([c_re.astype(F32), c_im.astype(F32)]), (0, 1, 3, 2))
    cs = cs.reshape(2, N_SG, SG_STATE, c).transpose(1, 0, 2, 3)
    dmat = (d.astype(F32).reshape(g, c, 1) * jnp.eye(c, dtype=F32)[None]).reshape(N_SG, LANES, c)
    ar = pw_re[..., CHUNK].reshape(N_SG, 1, SG_STATE)
    ai = pw_im[..., CHUNK].reshape(N_SG, 1, SG_STATE)
    return pwrep, pwcol, bbt, cs, dmat, ar, ai


def _ssm_kernel(u_ref, pwrep_ref, pwcol_ref, bbt_ref, cs_ref, dmat_ref, ar_ref, ai_ref, y_ref,
                tt_ref, wp_ref, wc_ref, p_ref, scr_ref, *, steps, nseq):
    def iota(shape, dim):
        return lax.broadcasted_iota(jnp.int32, shape, dim)

    def spread(nrows):
        r, cidx = iota((nrows, nrows * SG_GROUPS), 0), iota((nrows, nrows * SG_GROUPS), 1)
        return jnp.where(((r >> 4) == (cidx >> 7)) & ((r & 15) == (cidx & 15)), 1.0, 0.0).astype(BF16)

    grp16 = lambda idx: (idx >> 4) & (SG_GROUPS - 1)
    grp64 = lambda idx: (idx >> 6) & (SG_GROUPS - 1)


    r, cidx = iota((SSM_STATE, SG_STATE), 0), iota((SSM_STATE, SG_STATE), 1)
    e_p = jnp.where(r == (cidx & (SSM_STATE - 1)), 1.0, 0.0).astype(BF16)
    keep = grp16(iota((LANES, SG_STATE), 0)) == grp64(iota((LANES, SG_STATE), 1))
    b_r, b_i = bbt_ref[0, 0], bbt_ref[0, 1]
    for s in range(CHUNK):
        a_r, a_i = pwrep_ref[0, 0, CHUNK - 1 - s], pwrep_ref[0, 1, CHUNK - 1 - s]
        for part, v in enumerate((a_r * b_r - a_i * b_i, a_r * b_i + a_i * b_r)):
            full = jnp.dot(v.astype(BF16), e_p, preferred_element_type=F32)
            wp_ref[s * LANES:(s + 1) * LANES, part * SG_STATE:(part + 1) * SG_STATE] = (
                jnp.where(keep, full, 0.0).astype(BF16))

    c_r, c_i = cs_ref[0, 0], cs_ref[0, 1]
    cstack = jnp.concatenate([c_r, -c_i], axis=0).astype(BF16)
    e_t = spread(SSM_GROUP)
    keep = grp16(iota((LANES, LANES), 0)) == grp16(iota((LANES, LANES), 1))
    lag_blocks = []
    for k in range(CHUNK):
        slab = wp_ref[(CHUNK - 1 - k) * LANES:(CHUNK - k) * LANES, :]
        kk = jnp.dot(slab, cstack, preferred_element_type=F32)
        if k == 0:
            kk = kk + dmat_ref[0]
        lag_blocks.append(jnp.where(keep, jnp.dot(kk.astype(BF16), e_t, preferred_element_type=F32), 0.0)
                          .astype(BF16))
    for dlt in range(N_TT):
        for s in range(2):
            for t in range(2):
                k = 2 * dlt + t - s
                blk = lag_blocks[k] if k >= 0 else jnp.zeros((LANES, LANES), BF16)
                tt_ref[dlt, s * LANES:(s + 1) * LANES, t * LANES:(t + 1) * LANES] = blk

    keep = grp64(iota((SG_STATE, LANES), 0)) == grp16(iota((SG_STATE, LANES), 1))
    for t in range(CHUNK):
        a_r = pwcol_ref[0, 0, :, t + 1:t + 2]
        a_i = pwcol_ref[0, 1, :, t + 1:t + 2]
        for part, v in enumerate((c_r * a_r - c_i * a_i, -(c_r * a_i + c_i * a_r))):
            full = jnp.dot(v.astype(BF16), e_t, preferred_element_type=F32)
            wc_ref[part * SG_STATE:(part + 1) * SG_STATE, t * LANES:(t + 1) * LANES] = (
                jnp.where(keep, full, 0.0).astype(BF16))

    p_ref[...] = jnp.dot(u_ref[0], wp_ref[...], preferred_element_type=F32)

    ar = ar_ref[0]
    ai = ai_ref[0]

    def body(c, carry):
        new = []
        for b in range(nseq):
            hr, hi = carry[b]
            row = b * steps + c
            pr = p_ref[pl.ds(row, 1), 0:SG_STATE]
            pi = p_ref[pl.ds(row, 1), SG_STATE:2 * SG_STATE]
            p_ref[pl.ds(row, 1), 0:SG_STATE] = hr
            p_ref[pl.ds(row, 1), SG_STATE:2 * SG_STATE] = hi
            new.append((ar * hr - ai * hi + pr, ar * hi + ai * hr + pi))
        return tuple(new)

    z = jnp.zeros((1, SG_STATE), F32)
    lax.fori_loop(0, steps, body, ((z, z),) * nseq, unroll=2)

    tn = scr_ref.shape[0] // CHUNK
    for r0 in range(0, steps * nseq, tn):
        rows = slice(r0, r0 + tn)
        hb = p_ref[rows, :].astype(BF16)
        for jo in range(N_TT):
            cols = slice(jo * MXU_TILE, (jo + 1) * MXU_TILE)
            acc = jnp.dot(hb, wc_ref[:, cols], preferred_element_type=F32)
            for ji in range(jo + 1):
                acc += jnp.dot(u_ref[0, rows, ji * MXU_TILE:(ji + 1) * MXU_TILE], tt_ref[jo - ji],
                               preferred_element_type=F32)
            yg = jax.nn.gelu(acc, approximate=True)
            for e in range(MXU_TILE // LANES):
                j = jo * (MXU_TILE // LANES) + e
                scr_ref[pl.ds(j, tn, stride=CHUNK), :] = yg[:, e * LANES:(e + 1) * LANES]
        y_ref[r0 * CHUNK:(r0 + tn) * CHUNK, :] = scr_ref[...].astype(BF16)


def _ssm(uflat, params, bsz):
    nc = uflat.shape[1]
    blk = lambda a: pl.BlockSpec((1,) + a.shape[1:], lambda s: (s,) + (0,) * (a.ndim - 1))
    return pl.pallas_call(
        functools.partial(_ssm_kernel, steps=nc // bsz, nseq=bsz),
        grid=(N_SG,),
        in_specs=[blk(uflat)] + [blk(a) for a in params],
        out_specs=pl.BlockSpec((nc * CHUNK, LANES), lambda s: (0, s)),
        out_shape=jax.ShapeDtypeStruct((nc * CHUNK, SSM_WIDTH), BF16),
        scratch_shapes=[pltpu.VMEM((N_TT, MXU_TILE, MXU_TILE), BF16),
                        pltpu.VMEM((FLAT, 2 * SG_STATE), BF16),
                        pltpu.VMEM((2 * SG_STATE, FLAT), BF16),
                        pltpu.VMEM((nc, 2 * SG_STATE), F32),
                        pltpu.VMEM((SSM_Y_ROWS * CHUNK, LANES), F32)],
        compiler_params=_cparams(1),
        name="ssm",
    )(uflat, *params)


ATT_UNITS_IN_FLIGHT = 4
ATT_STEP_QUERIES = 2048


def _attn_kernel(q_ref, kp_ref, kc_ref, vp_ref, vc_ref, o_ref, lse_ref, ks_ref, vs_ref):
    nsb, qb = q_ref.shape[0], q_ref.shape[1]
    blk = ATT_SPAN
    first = pl.program_id(1) == 0

    lane = lax.broadcasted_iota(jnp.int32, (1, ATT_MERGED), 1)
    head_sel = (lane % LANES < ATT_HEAD_DIM, lane % LANES >= ATT_HEAD_DIM)
    for sq in range(nsb):
        for src_p, src_c, dst in ((kp_ref, kc_ref, ks_ref), (vp_ref, vc_ref, vs_ref)):
            for e in range(2):
                xp, xc = src_p[sq], src_c[sq]
                dst[sq, e, 0:blk] = jnp.where(head_sel[e], xp, jnp.zeros_like(xp))
                dst[sq, e, blk:blk + qb] = jnp.where(head_sel[e], xc, jnp.zeros_like(xc))

    qi = lax.broadcasted_iota(jnp.int32, (blk, 2 * blk), 0)
    kk = lax.broadcasted_iota(jnp.int32, (blk, 2 * blk), 1)
    band = jnp.logical_and(kk >= qi, kk <= qi + blk)
    band_first = jnp.logical_and(band, jnp.logical_or(kk >= blk, jnp.logical_not(first)))
    low = lax.broadcasted_iota(jnp.int32, (blk, LANES), 1) < ATT_HEAD_DIM
    dn = (((1,), (1,)), ((), ()))

    units = [(sq, j, pair) for sq in range(nsb) for j in range(qb // blk) for pair in range(ATT_MERGED // LANES)]
    for g0 in range(0, len(units), ATT_UNITS_IN_FLIGHT):
        group = units[g0:g0 + ATT_UNITS_IN_FLIGHT]
        scores = []
        for sq, j, pair in group:
            cols = slice(pair * LANES, (pair + 1) * LANES)
            keys = slice(j * blk, (j + 2) * blk)
            kcat = jnp.concatenate([ks_ref[sq, 0, keys, cols], ks_ref[sq, 1, keys, cols]], axis=0)
            s = lax.dot_general(q_ref[sq, j * blk:(j + 1) * blk, cols], kcat, dn,
                                preferred_element_type=F32)
            mask = band_first if j == 0 else band
            scores.append([jnp.where(mask, s[:, e * 2 * blk:(e + 1) * 2 * blk], NEG_INF) for e in range(2)])
        maxes = [[jnp.max(jnp.maximum(se[:, :blk], se[:, blk:]), axis=-1, keepdims=True) for se in su]
                 for su in scores]
        probs = [[jnp.exp2(se - me) for se, me in zip(su, mu)] for su, mu in zip(scores, maxes)]
        dens = [[jnp.sum(pe[:, :blk] + pe[:, blk:], axis=-1, keepdims=True) for pe in pu] for pu in probs]
        for (sq, j, pair), pu, mu, du in zip(group, probs, maxes, dens):
            cols = slice(pair * LANES, (pair + 1) * LANES)
            rows = slice(j * blk, (j + 1) * blk)
            keys = slice(j * blk, (j + 2) * blk)
            vcat = jnp.concatenate([vs_ref[sq, 0, keys, cols], vs_ref[sq, 1, keys, cols]], axis=0)
            pcat = jnp.concatenate([pu[0].astype(BF16), pu[1].astype(BF16)], axis=1)
            num = jnp.dot(pcat, vcat, preferred_element_type=F32)
            o_ref[sq, rows, cols] = (num * jnp.where(low, 1.0 / du[0], 1.0 / du[1])).astype(BF16)
            lse_ref[sq, rows, cols] = jnp.where(low, mu[0] * LN2 + jnp.log(du[0]), mu[1] * LN2 + jnp.log(du[1]))


def _banded_attention(q, k, v):
    nseq, length, w = q.shape
    blk = ATT_SPAN
    qb = min(length, ATT_STEP_QUERIES)
    nsb = ATT_STEP_QUERIES // qb
    per = qb // blk
    cur = pl.BlockSpec((nsb, qb, w), lambda s, n: (s, n, 0))
    prev = pl.BlockSpec((nsb, blk, w), lambda s, n: (s, jnp.maximum(n * per - 1, 0), 0))
    return pl.pallas_call(
        _attn_kernel,
        grid=(nseq // nsb, length // qb),
        in_specs=[cur, prev, cur, prev, cur],
        out_specs=[cur, cur],
        out_shape=[jax.ShapeDtypeStruct(q.shape, BF16), jax.ShapeDtypeStruct(q.shape, F32)],
        scratch_shapes=[pltpu.VMEM((nsb, 2, blk + qb, w), BF16)] * 2,
        compiler_params=_cparams(2),
        name="attn",
    )(q, k, k, v, v)


def _mix_kernel(y_ref, o0_ref, o1_ref, o2_ref, l0_ref, l1_ref, l2_ref, h_ref, wgate_ref, bgate_ref,
                wglu_ref, bglu_ref, wup_ref, wmix_ref, bmix_ref, g_ref, b_ref, out_ref, scr_ref, acc_ref):
    o_refs = (o0_ref, o1_ref, o2_ref)
    l_refs = (l0_ref, l1_ref, l2_ref)

    def tile(sub, rows):
        def natural(ref, dil):
            n = SUB_ROWS // dil
            if dil == 1:
                return ref[0, 0, rows, :].astype(F32)
            for r in range(dil):
                for e in range(ATT_MERGED // LANES):
                    scr_ref[sub, e, pl.ds(r, n, stride=dil), :] = (
                        ref[0, r, sub * n:(sub + 1) * n, e * LANES:(e + 1) * LANES].astype(F32))
            return jnp.concatenate([scr_ref[sub, e] for e in range(ATT_MERGED // LANES)], axis=-1)

        ls = [natural(l_refs[gi], dil) for gi, dil in enumerate(DILATIONS)]
        m = jnp.maximum(jnp.maximum(ls[0], ls[1]), ls[2])
        es = [jnp.exp(l - m) for l in ls]
        att = es[0] * natural(o_refs[0], DILATIONS[0])
        for gi in (1, 2):
            att += es[gi] * natural(o_refs[gi], DILATIONS[gi])
        att = (att / (es[0] + es[1] + es[2])).astype(BF16)
        y = y_ref[rows, :]
        hb = h_ref[rows, :].astype(BF16)
        yield

        for c in range(D_MODEL // MIX_CHUNK):
            cols = slice(c * MIX_CHUNK, (c + 1) * MIX_CHUNK)
            gate_cols = slice(D_MODEL + c * MIX_CHUNK, D_MODEL + (c + 1) * MIX_CHUNK)
            val = jnp.dot(y, wglu_ref[:, cols], preferred_element_type=F32) + bglu_ref[:, cols]
            gate = jnp.dot(y, wglu_ref[:, gate_cols], preferred_element_type=F32) + bglu_ref[:, gate_cols]
            b_att = jnp.dot(att, wup_ref[:, cols], preferred_element_type=F32)
            g_ssm = jnp.dot(hb, wgate_ref[:, cols], preferred_element_type=F32) + bgate_ref[:, cols]
            g_att = jnp.dot(hb, wgate_ref[:, gate_cols], preferred_element_type=F32) + bgate_ref[:, gate_cols]
            mixed = _sigmoid(g_ssm) * (val * _sigmoid(gate)) + _sigmoid(g_att) * b_att
            acc_ref[rows, cols] = mixed.astype(BF16)
            yield

        r = jnp.dot(acc_ref[rows, :], wmix_ref[...], preferred_element_type=F32) + bmix_ref[...]
        yield
        out_ref[rows, :] = _layer_norm(DEEPNORM_ALPHA * h_ref[rows, :] + r, g_ref[...], b_ref[...])

    _stagger([tile(i, rows) for i, rows in _sub_tiles(h_ref.shape[0])], lag=STAGE_LAG)


def _mix(yg, outs, lses, h, wgate, bgate, wglu, bglu, wup, wmix, bmix, g, b, seq):
    t = h.shape[0]
    tm = 2 * TOKEN_TILE
    per_b = seq // tm
    row = lambda n: pl.BlockSpec((tm, n), lambda i: (i, 0))
    full = lambda a: pl.BlockSpec(a.shape, lambda i: (0,) * a.ndim, pipeline_mode=pl.Buffered(1))
    grp = lambda d: pl.BlockSpec((1, d, tm // d, ATT_MERGED), lambda i: (i // per_b, 0, i % per_b, 0))
    consts = (wgate, bgate, wglu, bglu, wup, wmix, bmix, g, b)
    return pl.pallas_call(
        _mix_kernel,
        grid=(t // tm,),
        in_specs=([row(SSM_WIDTH)] + [grp(d) for d in DILATIONS] * 2 + [row(D_MODEL)]
                  + [full(a) for a in consts]),
        out_specs=row(D_MODEL),
        out_shape=jax.ShapeDtypeStruct((t, D_MODEL), F32),
        scratch_shapes=[pltpu.VMEM((tm // SUB_ROWS, ATT_MERGED // LANES, SUB_ROWS, LANES), F32),
                        pltpu.VMEM((tm, D_MODEL), BF16)],
        compiler_params=_cparams(1),
        name="mix",
    )(yg, *outs, *lses, h, *consts)


def _tail_kernel(h_ref, mem_ref, wkv_ref, wq_ref, wo_ref, g2_ref, b2_ref,
                 w1_ref, bf1_ref, w2_ref, bf2_ref, g3_ref, b3_ref, out_ref, kv_ref, acc_ref, *, steps_per_batch):
    dn = (((1,), (1,)), ((), ()))

    @pl.when(pl.program_id(0) % steps_per_batch == 0)
    def _():
        kv_ref[...] = jnp.dot(mem_ref[0].astype(BF16), wkv_ref[...], preferred_element_type=F32).astype(BF16)

    def tile(sub, rows):
        h = h_ref[rows, :]
        hb = h.astype(BF16)
        yield
        q = (jnp.dot(hb, wq_ref[...], preferred_element_type=F32) * XATT_SCALE).astype(BF16)
        yield
        outs = []
        for hh in range(XATT_HEADS):
            sl = slice(hh * XATT_HEAD_DIM, (hh + 1) * XATT_HEAD_DIM)
            sv = slice(D_MODEL + hh * XATT_HEAD_DIM, D_MODEL + (hh + 1) * XATT_HEAD_DIM)
            s = lax.dot_general(q[:, sl], kv_ref[:, sl], dn, preferred_element_type=F32)
            e = jnp.exp(s - jnp.max(s, axis=-1, keepdims=True))
            p = e / jnp.sum(e, axis=-1, keepdims=True)
            outs.append(jnp.dot(p.astype(BF16), kv_ref[:, sv], preferred_element_type=F32).astype(BF16))
            yield
        xo = jnp.dot(jnp.concatenate(outs, axis=-1), wo_ref[...], preferred_element_type=F32)
        yield
        h = _layer_norm(DEEPNORM_ALPHA * h + xo, g2_ref[...], b2_ref[...])
        hb = h.astype(BF16)
        yield
        for c in range(D_FF // FFN_CHUNK):
            cols = slice(c * FFN_CHUNK, (c + 1) * FFN_CHUNK)
            a = jnp.dot(hb, w1_ref[:, cols], preferred_element_type=F32) + bf1_ref[:, cols]
            a = jnp.square(jnp.maximum(a, 0.0)).astype(BF16)
            part = jnp.dot(a, w2_ref[cols, :], preferred_element_type=F32)
            if c == 0:
                acc_ref[rows, :] = part
            else:
                acc_ref[rows, :] += part
            yield
        ff = acc_ref[rows, :] + bf2_ref[...]
        out_ref[rows, :] = _layer_norm(DEEPNORM_ALPHA * h + ff, g3_ref[...], b3_ref[...])

    _stagger([tile(i, rows) for i, rows in _sub_tiles(h_ref.shape[0])], lag=STAGE_LAG)


def _tail(h, mem, wkv, wq, wo, g2, b2, w1, bf1, w2, bf2, g3, b3, seq):
    t = h.shape[0]
    tm = 2 * TOKEN_TILE
    per_b = seq // tm
    row = pl.BlockSpec((tm, D_MODEL), lambda i: (i, 0))
    full = lambda a: pl.BlockSpec(a.shape, lambda i: (0,) * a.ndim, pipeline_mode=pl.Buffered(1))
    consts = (wkv, wq, wo, g2, b2, w1, bf1, w2, bf2, g3, b3)
    return pl.pallas_call(
        functools.partial(_tail_kernel, steps_per_batch=per_b),
        grid=(t // tm,),
        in_specs=[row, pl.BlockSpec((1,) + mem.shape[1:], lambda i: (i // per_b, 0, 0))] + [full(a) for a in consts],
        out_specs=row,
        out_shape=jax.ShapeDtypeStruct((t, D_MODEL), F32),
        scratch_shapes=[pltpu.VMEM((mem.shape[1], 2 * D_MODEL), BF16), pltpu.VMEM((tm, D_MODEL), F32)],
        compiler_params=pltpu.CompilerParams(dimension_semantics=("arbitrary",), vmem_limit_bytes=VMEM_LIMIT),
        name="tail",
    )(h, mem, *consts)


def _rope_selectors():
    half = ROT_DIM // 2
    sel = np.zeros((ROT_DIM, 3 * LANES), np.float32)
    one = np.zeros((1, LANES), np.float32)
    for lane in range(LANES):
        d = lane % ATT_HEAD_DIM
        if d >= ROT_DIM:
            one[0, lane] = 1.0
            continue
        sel[d % half, lane] = 1.0
        if d < half:
            sel[half + d, LANES + lane] = -1.0
        else:
            sel[d, 2 * LANES + lane] = 1.0
    return jnp.asarray(sel, BF16), jnp.asarray(one)


def _rope_cos_sin(positions):
    inv_freq = ROPE_THETA ** (-jnp.arange(0, ROT_DIM, 2, dtype=F32) / ROT_DIM)
    ang = positions.astype(F32).reshape(-1, 1) * inv_freq
    return jnp.concatenate([jnp.cos(ang), jnp.sin(ang)], axis=-1)


def kernel(x, mem, positions, ln_in_g, ln_in_b, w_in, b_in, ssm_log_dt, ssm_a_re, ssm_a_im, ssm_b_re, ssm_b_im, ssm_c_re, ssm_c_im, ssm_d, w_glu, b_glu, w_att_up, w_mix_out, b_mix_out, ln1_g, ln1_b, w_xq, w_xkv, w_xo, ln2_g, ln2_b, w_ff1, b_ff1, w_ff2, b_ff2, ln3_g, ln3_b):
    bsz, seq, _ = x.shape
    t = bsz * seq
    row2 = lambda a: a.reshape(1, -1).astype(F32)
    cs = _rope_cos_sin(positions)
    h = x.reshape(t, D_MODEL)
    for l in range(DEPTH):
        h, uflat, *qkv = _ln_proj(
            h, row2(ln_in_g), row2(ln_in_b), w_in[l, :, :GATE_OFF].astype(BF16), row2(b_in[l, :GATE_OFF]),
            cs, bsz, seq)

        ssm_params = _ssm_matrices(ssm_log_dt[l], ssm_a_re[l], ssm_a_im[l], ssm_b_re[l],
                                   ssm_b_im[l], ssm_c_re[l], ssm_c_im[l], ssm_d[l])
        yg = _ssm(uflat, ssm_params, bsz)

        outs, lses = [], []
        for gi, dil in enumerate(DILATIONS):
            qg, kg, vg = (a.reshape(bsz * dil, seq // dil, ATT_MERGED) for a in qkv[3 * gi:3 * gi + 3])
            o_g, lse_g = _banded_attention(qg, kg, vg)
            outs.append(o_g.reshape(bsz, dil, seq // dil, ATT_MERGED))
            lses.append(lse_g.reshape(bsz, dil, seq // dil, ATT_MERGED))

        h = _mix(yg, outs, lses, h, w_in[l, :, GATE_OFF:].astype(BF16), row2(b_in[l, GATE_OFF:]),
                 w_glu[l].astype(BF16), row2(b_glu[l]),
                 w_att_up[l].astype(BF16), w_mix_out[l].astype(BF16), row2(b_mix_out[l]),
                 row2(ln1_g[l]), row2(ln1_b[l]), seq)

        h = _tail(h, mem, w_xkv[l].astype(BF16), w_xq[l].astype(BF16), w_xo[l].astype(BF16),
                  row2(ln2_g[l]), row2(ln2_b[l]), w_ff1[l].astype(BF16), row2(b_ff1[l]),
                  w_ff2[l].astype(BF16), row2(b_ff2[l]), row2(ln3_g[l]), row2(ln3_b[l]), seq)
    return h.reshape(bsz, seq, D_MODEL)
```

```python
import functools
import math

import jax
import jax.numpy as jnp
import numpy as np
from jax import lax
from jax.experimental import pallas as pl
from jax.experimental.pallas import tpu as pltpu

F32 = jnp.float32
BF16 = jnp.bfloat16

D_MODEL = 1024
SSM_GROUP = 16
SSM_WIDTH = 768
SSM_GROUPS = SSM_WIDTH // SSM_GROUP
SSM_STATE = 64
ATT_HEAD_DIM = 64
ATT_HEADS_PER_GROUP = 4
DILATIONS = (1, 4, 16)
ATT_SPAN = 128
ATT_WIDTH = 768
ATT_MERGED = ATT_HEADS_PER_GROUP * ATT_HEAD_DIM
ATT_SCALE = ATT_HEAD_DIM ** -0.5
ROT_DIM = ATT_HEAD_DIM // 4
ROPE_THETA = 500000.0
XATT_HEADS = 4
XATT_HEAD_DIM = D_MODEL // XATT_HEADS
XATT_SCALE = XATT_HEAD_DIM ** -0.5
D_FF = 4 * D_MODEL
DEPTH = 1
DEEPNORM_ALPHA = (2 * DEPTH) ** 0.25
LN_EPS = 1e-5
NEG_INF = -1e30
LOG2E = math.log2(math.e)
LN2 = math.log(2.0)

LANES = 128
CHUNK = 16
SG_GROUPS = LANES // SSM_GROUP
N_SG = SSM_GROUPS // SG_GROUPS
FLAT = CHUNK * LANES
SG_STATE = SG_GROUPS * SSM_STATE
MXU_TILE = 256
N_TT = FLAT // MXU_TILE

SSM_Y_ROWS = 512
SUB_ROWS = 256
TOKEN_TILE = 512
STAGE_LAG = 1
MIX_CHUNK = 256
FFN_CHUNK = 1024
QKV_OFF = SSM_WIDTH
GATE_OFF = SSM_WIDTH + 3 * ATT_WIDTH
VMEM_LIMIT = 56 * 1024 * 1024


def _cparams(n_axes):
    return pltpu.CompilerParams(dimension_semantics=("parallel",) * n_axes,
                                vmem_limit_bytes=VMEM_LIMIT)


def _layer_norm(x, g, b):
    mu = jnp.mean(x, axis=-1, keepdims=True)
    xc = x - mu
    var = jnp.mean(xc * xc, axis=-1, keepdims=True)
    return xc * lax.rsqrt(var + LN_EPS) * g + b


def _sigmoid(x):
    return 0.5 * jnp.tanh(0.5 * x) + 0.5


def _stagger(tiles, lag):
    tiles = list(tiles)
    live = [True] * len(tiles)
    rnd = 0
    while any(live):
        for i in reversed(range(len(tiles))):
            if live[i] and rnd >= i * lag:
                try:
                    next(tiles[i])
                except StopIteration:
                    live[i] = False
        rnd += 1


def _sub_tiles(tm):
    return [(i, slice(i * SUB_ROWS, (i + 1) * SUB_ROWS)) for i in range(tm // SUB_ROWS)]


def _ln_proj_kernel(x_ref, g_ref, b_ref, w_ref, bi_ref, cs_ref, sel_ref, one_ref,
                    h_ref, u_ref, q0_ref, k0_ref, v0_ref, q1_ref, k1_ref, v1_ref, q2_ref, k2_ref, v2_ref,
                    scr_ref):
    half = ROT_DIM // 2
    qkv_refs = ((q0_ref, k0_ref, v0_ref), (q1_ref, k1_ref, v1_ref), (q2_ref, k2_ref, v2_ref))
    sections = [(0, SSM_WIDTH), (QKV_OFF, ATT_WIDTH), (QKV_OFF + ATT_WIDTH, ATT_WIDTH),
                (QKV_OFF + 2 * ATT_WIDTH, ATT_WIDTH)]

    def tile(sub, rows):
        h = _layer_norm(x_ref[rows, :], g_ref[...], b_ref[...])
        h_ref[rows, :] = h
        hb = h.astype(BF16)
        cs = cs_ref[rows, :]
        cs_hi = cs.astype(BF16)
        cs_lo = (cs - cs_hi.astype(F32)).astype(BF16)
        tab = (jnp.dot(cs_hi, sel_ref[...], preferred_element_type=F32)
               + jnp.dot(cs_lo, sel_ref[...], preferred_element_type=F32))
        cc = tab[:, 0:LANES] + one_ref[...]
        s1 = tab[:, LANES:2 * LANES]
        s2 = tab[:, 2 * LANES:3 * LANES]
        yield

        def stage(off, val):
            for i in range(val.shape[1] // LANES):
                scr_ref[sub, off // LANES + i] = val[:, i * LANES:(i + 1) * LANES]

        def rope(val, scale):
            out = []
            for i in range(ATT_WIDTH // LANES):
                t = val[:, i * LANES:(i + 1) * LANES]
                out.append((t * cc + pltpu.roll(t, LANES - half, 1) * s1 + pltpu.roll(t, half, 1) * s2) * scale)
            return jnp.concatenate(out, axis=-1)

        def emit_groups(a):
            for gi, dil in enumerate(DILATIONS):
                ref = qkv_refs[gi][a]
                cb0 = (QKV_OFF + a * ATT_WIDTH + gi * ATT_MERGED) // LANES
                n = SUB_ROWS // dil
                for r in range(dil):
                    for e in range(ATT_MERGED // LANES):
                        if dil == 1:
                            blk = scr_ref[sub, cb0 + e]
                        else:
                            blk = scr_ref[sub, cb0 + e, pl.ds(r, n, stride=dil), :]
                        ref[0, r, sub * n:(sub + 1) * n, e * LANES:(e + 1) * LANES] = blk.astype(BF16)

        for c, (off, width) in enumerate(sections):
            val = (jnp.dot(hb, w_ref[:, off:off + width], preferred_element_type=F32)
                   + bi_ref[:, off:off + width])
            if c == 0:
                stage(off, val)
                n = SUB_ROWS // CHUNK
                for j in range(CHUNK):
                    for sg in range(N_SG):
                        blk = scr_ref[sub, sg, pl.ds(j, n, stride=CHUNK), :]
                        u_ref[sg, sub * n:(sub + 1) * n, j * LANES:(j + 1) * LANES] = blk.astype(BF16)
            elif c in (1, 2):
                stage(off, rope(val, ATT_SCALE * LOG2E if c == 1 else 1.0))
                emit_groups(c - 1)
            else:
                stage(off, val)
                emit_groups(2)
            yield

    _stagger([tile(i, rows) for i, rows in _sub_tiles(x_ref.shape[0])], lag=STAGE_LAG)


def _ln_proj(x2, g, b, w, bi, cs, bsz, seq):
    t = x2.shape[0]
    tm = 2 * TOKEN_TILE
    per_b = seq // tm
    sel, one = _rope_selectors()
    row = lambda n: pl.BlockSpec((tm, n), lambda i: (i, 0))
    full = lambda a: pl.BlockSpec(a.shape, lambda i: (0,) * a.ndim, pipeline_mode=pl.Buffered(1))
    grp = lambda d: pl.BlockSpec((1, d, tm // d, ATT_MERGED), lambda i: (i // per_b, 0, i % per_b, 0))
    outs = [jax.ShapeDtypeStruct((t, D_MODEL), F32),
            jax.ShapeDtypeStruct((N_SG, t // CHUNK, FLAT), BF16)]
    out_specs = [row(D_MODEL), pl.BlockSpec((N_SG, tm // CHUNK, FLAT), lambda i: (0, i, 0))]
    for d in DILATIONS:
        outs += [jax.ShapeDtypeStruct((bsz, d, seq // d, ATT_MERGED), BF16)] * 3
        out_specs += [grp(d)] * 3
    return pl.pallas_call(
        _ln_proj_kernel,
        grid=(t // tm,),
        in_specs=[row(D_MODEL), full(g), full(b), full(w), full(bi), row(ROT_DIM), full(sel), full(one)],
        out_specs=out_specs,
        out_shape=outs,
        scratch_shapes=[pltpu.VMEM((tm // SUB_ROWS, GATE_OFF // LANES, SUB_ROWS, LANES), F32)],
        compiler_params=_cparams(1),
        name="ln_proj",
    )(x2, g, b, w, bi, cs, sel, one)


def _ssm_matrices(log_dt, a_re, a_im, b_re, b_im, c_re, c_im, d):
    g, n, c = SSM_GROUPS, SSM_STATE, SSM_GROUP
    dt = jnp.exp(log_dt.astype(F32))[:, None]
    a_re = a_re.astype(F32)
    a_im = a_im.astype(F32)
    ks = jnp.arange(CHUNK + 1, dtype=F32)
    mag = jnp.exp((a_re * dt)[..., None] * ks)
    ang = (a_im * dt)[..., None] * ks
    pw_re = mag * jnp.cos(ang)
    pw_im = mag * jnp.sin(ang)
    ab_re, ab_im = pw_re[..., 1], pw_im[..., 1]
    den = jnp.square(a_re) + jnp.square(a_im)
    nr = ab_re - 1.0
    f_re = (nr * a_re + ab_im * a_im) / den
    f_im = (ab_im * a_re - nr * a_im) / den
    b_re = b_re.astype(F32)
    b_im = b_im.astype(F32)
    bb_re = f_re[..., None] * b_re - f_im[..., None] * b_im
    bb_im = f_re[..., None] * b_im + f_im[..., None] * b_re
    pw = jnp.stack([pw_re, pw_im])
    pwrep = jnp.transpose(pw.reshape(2, N_SG, SG_GROUPS, n, CHUNK + 1), (1, 0, 4, 2, 3))
    pwrep = jnp.broadcast_to(pwrep[:, :, :, :, None, :], (N_SG, 2, CHUNK + 1, SG_GROUPS, c, n))
    pwrep = pwrep.reshape(N_SG, 2, CHUNK + 1, LANES, n)
    pwcol = jnp.transpose(pw.reshape(2, N_SG, SG_STATE, CHUNK + 1), (1, 0, 2, 3))
    bbt = jnp.transpose(jnp.stack([bb_re, bb_im]), (0, 1, 3, 2)).reshape(2, N_SG, LANES, n).transpose(1, 0, 2, 3)
    cs = jnp.transpose(jnp.stack([c_re.astype(F32), c_im.astype(F32)]), (0, 1, 3, 2))
    cs = cs.reshape(2, N_SG, SG_STATE, c).transpose(1, 0, 2, 3)
    dmat = (d.astype(F32).reshape(g, c, 1) * jnp.eye(c, dtype=F32)[None]).reshape(N_SG, LANES, c)
    ar = pw_re[..., CHUNK].reshape(N_SG, 1, SG_STATE)
    ai = pw_im[..., CHUNK].reshape(N_SG, 1, SG_STATE)
    return pwrep, pwcol, bbt, cs, dmat, ar, ai


def _ssm_kernel(u_ref, pwrep_ref, pwcol_ref, bbt_ref, cs_ref, dmat_ref, ar_ref, ai_ref, y_ref,
                tt_ref, wp_ref, wc_ref, p_ref, scr_ref, *, steps, nseq):
    def iota(shape, dim):
        return lax.broadcasted_iota(jnp.int32, shape, dim)

    def spread(nrows):
        r, cidx = iota((nrows, nrows * SG_GROUPS), 0), iota((nrows, nrows * SG_GROUPS), 1)
        return jnp.where(((r >> 4) == (cidx >> 7)) & ((r & 15) == (cidx & 15)), 1.0, 0.0).astype(BF16)

    grp16 = lambda idx: (idx >> 4) & (SG_GROUPS - 1)
    grp64 = lambda idx: (idx >> 6) & (SG_GROUPS - 1)


    r, cidx = iota((SSM_STATE, SG_STATE), 0), iota((SSM_STATE, SG_STATE), 1)
    e_p = jnp.where(r == (cidx & (SSM_STATE - 1)), 1.0, 0.0).astype(BF16)
    keep = grp16(iota((LANES, SG_STATE), 0)) == grp64(iota((LANES, SG_STATE), 1))
    b_r, b_i = bbt_ref[0, 0], bbt_ref[0, 1]
    for s in range(CHUNK):
        a_r, a_i = pwrep_ref[0, 0, CHUNK - 1 - s], pwrep_ref[0, 1, CHUNK - 1 - s]
        for part, v in enumerate((a_r * b_r - a_i * b_i, a_r * b_i + a_i * b_r)):
            full = jnp.dot(v.astype(BF16), e_p, preferred_element_type=F32)
            wp_ref[s * LANES:(s + 1) * LANES, part * SG_STATE:(part + 1) * SG_STATE] = (
                jnp.where(keep, full, 0.0).astype(BF16))

    c_r, c_i = cs_ref[0, 0], cs_ref[0, 1]
    cstack = jnp.concatenate([c_r, -c_i], axis=0).astype(BF16)
    e_t = spread(SSM_GROUP)
    keep = grp16(iota((LANES, LANES), 0)) == grp16(iota((LANES, LANES), 1))
    lag_blocks = []
    for k in range(CHUNK):
        slab = wp_ref[(CHUNK - 1 - k) * LANES:(CHUNK - k) * LANES, :]
        kk = jnp.dot(slab, cstack, preferred_element_type=F32)
        if k == 0:
            kk = kk + dmat_ref[0]
        lag_blocks.append(jnp.where(keep, jnp.dot(kk.astype(BF16), e_t, preferred_element_type=F32), 0.0)
                          .astype(BF16))
    for dlt in range(N_TT):
        for s in range(2):
            for t in range(2):
                k = 2 * dlt + t - s
                blk = lag_blocks[k] if k >= 0 else jnp.zeros((LANES, LANES), BF16)
                tt_ref[dlt, s * LANES:(s + 1) * LANES, t * LANES:(t + 1) * LANES] = blk

    keep = grp64(iota((SG_STATE, LANES), 0)) == grp16(iota((SG_STATE, LANES), 1))
    for t in range(CHUNK):
        a_r = pwcol_ref[0, 0, :, t + 1:t + 2]
        a_i = pwcol_ref[0, 1, :, t + 1:t + 2]
        for part, v in enumerate((c_r * a_r - c_i * a_i, -(c_r * a_i + c_i * a_r))):
            full = jnp.dot(v.astype(BF16), e_t, preferred_element_type=F32)
            wc_ref[part * SG_STATE:(part + 1) * SG_STATE, t * LANES:(t + 1) * LANES] = (
                jnp.where(keep, full, 0.0).astype(BF16))

    p_ref[...] = jnp.dot(u_ref[0], wp_ref[...], preferred_element_type=F32)

    ar = ar_ref[0]
    ai = ai_ref[0]

    def body(c, carry):
        new = []
        for b in range(nseq):
            hr, hi = carry[b]
            row = b * steps + c
            pr = p_ref[pl.ds(row, 1), 0:SG_STATE]
            pi = p_ref[pl.ds(row, 1), SG_STATE:2 * SG_STATE]
            p_ref[pl.ds(row, 1), 0:SG_STATE] = hr
            p_ref[pl.ds(row, 1), SG_STATE:2 * SG_STATE] = hi
            new.append((ar * hr - ai * hi + pr, ar * hi + ai * hr + pi))
        return tuple(new)

    z = jnp.zeros((1, SG_STATE), F32)
    lax.fori_loop(0, steps, body, ((z, z),) * nseq, unroll=2)

    tn = scr_ref.shape[0] // CHUNK
    for r0 in range(0, steps * nseq, tn):
        rows = slice(r0, r0 + tn)
        hb = p_ref[rows, :].astype(BF16)
        for jo in range(N_TT):
            cols = slice(jo * MXU_TILE, (jo + 1) * MXU_TILE)
            acc = jnp.dot(hb, wc_ref[:, cols], preferred_element_type=F32)
            for ji in range(jo + 1):
                acc += jnp.dot(u_ref[0, rows, ji * MXU_TILE:(ji + 1) * MXU_TILE], tt_ref[jo - ji],
                               preferred_element_type=F32)
            yg = jax.nn.gelu(acc, approximate=True)
            for e in range(MXU_TILE // LANES):
                j = jo * (MXU_TILE // LANES) + e
                scr_ref[pl.ds(j, tn, stride=CHUNK), :] = yg[:, e * LANES:(e + 1) * LANES]
        y_ref[r0 * CHUNK:(r0 + tn) * CHUNK, :] = scr_ref[...].astype(BF16)


def _ssm(uflat, params, bsz):
    nc = uflat.shape[1]
    blk = lambda a: pl.BlockSpec((1,) + a.shape[1:], lambda s: (s,) + (0,) * (a.ndim - 1))
    return pl.pallas_call(
        functools.partial(_ssm_kernel, steps=nc // bsz, nseq=bsz),
        grid=(N_SG,),
        in_specs=[blk(uflat)] + [blk(a) for a in params],
        out_specs=pl.BlockSpec((nc * CHUNK, LANES), lambda s: (0, s)),
        out_shape=jax.ShapeDtypeStruct((nc * CHUNK, SSM_WIDTH), BF16),
        scratch_shapes=[pltpu.VMEM((N_TT, MXU_TILE, MXU_TILE), BF16),
                        pltpu.VMEM((FLAT, 2 * SG_STATE), BF16),
                        pltpu.VMEM((2 * SG_STATE, FLAT), BF16),
                        pltpu.VMEM((nc, 2 * SG_STATE), F32),
                        pltpu.VMEM((SSM_Y_ROWS * CHUNK, LANES), F32)],
        compiler_params=_cparams(1),
        name="ssm",
    )(uflat, *params)


ATT_UNITS_IN_FLIGHT = 4
ATT_STEP_QUERIES = 2048


def _attn_kernel(*refs, n_cast):
    q_ref, kp_ref, kc_ref, vp_ref, vc_ref = refs[:5]
    cast_in = refs[5:5 + n_cast]
    o_ref, lse_ref = refs[5 + n_cast:7 + n_cast]
    cast_out = refs[7 + n_cast:7 + 2 * n_cast]
    ks_ref, vs_ref = refs[7 + 2 * n_cast:]
    for src, dst in zip(cast_in, cast_out):
        dst[...] = src[...].astype(BF16)

    nsb, qb = q_ref.shape[0], q_ref.shape[1]
    blk = ATT_SPAN
    first = pl.program_id(1) == 0

    lane = lax.broadcasted_iota(jnp.int32, (1, ATT_MERGED), 1)
    head_sel = (lane % LANES < ATT_HEAD_DIM, lane % LANES >= ATT_HEAD_DIM)
    for sq in range(nsb):
        for src_p, src_c, dst in ((kp_ref, kc_ref, ks_ref), (vp_ref, vc_ref, vs_ref)):
            for e in range(2):
                xp, xc = src_p[sq], src_c[sq]
                dst[sq, e, 0:blk] = jnp.where(head_sel[e], xp, jnp.zeros_like(xp))
                dst[sq, e, blk:blk + qb] = jnp.where(head_sel[e], xc, jnp.zeros_like(xc))

    qi = lax.broadcasted_iota(jnp.int32, (blk, 2 * blk), 0)
    kk = lax.broadcasted_iota(jnp.int32, (blk, 2 * blk), 1)
    band = jnp.logical_and(kk >= qi, kk <= qi + blk)
    band_first = jnp.logical_and(band, jnp.logical_or(kk >= blk, jnp.logical_not(first)))
    low = lax.broadcasted_iota(jnp.int32, (blk, LANES), 1) < ATT_HEAD_DIM
    dn = (((1,), (1,)), ((), ()))

    units = [(sq, j, pair) for sq in range(nsb) for j in range(qb // blk) for pair in range(ATT_MERGED // LANES)]
    for g0 in range(0, len(units), ATT_UNITS_IN_FLIGHT):
        group = units[g0:g0 + ATT_UNITS_IN_FLIGHT]
        scores = []
        for sq, j, pair in group:
            cols = slice(pair * LANES, (pair + 1) * LANES)
            keys = slice(j * blk, (j + 2) * blk)
            kcat = jnp.concatenate([ks_ref[sq, 0, keys, cols], ks_ref[sq, 1, keys, cols]], axis=0)
            s = lax.dot_general(q_ref[sq, j * blk:(j + 1) * blk, cols], kcat, dn,
                                preferred_element_type=F32)
            mask = band_first if j == 0 else band
            scores.append([jnp.where(mask, s[:, e * 2 * blk:(e + 1) * 2 * blk], NEG_INF) for e in range(2)])
        maxes = [[jnp.max(jnp.maximum(se[:, :blk], se[:, blk:]), axis=-1, keepdims=True) for se in su]
                 for su in scores]
        probs = [[jnp.exp2(se - me) for se, me in zip(su, mu)] for su, mu in zip(scores, maxes)]
        dens = [[jnp.sum(pe[:, :blk] + pe[:, blk:], axis=-1, keepdims=True) for pe in pu] for pu in probs]
        for (sq, j, pair), pu, mu, du in zip(group, probs, maxes, dens):
            cols = slice(pair * LANES, (pair + 1) * LANES)
            rows = slice(j * blk, (j + 1) * blk)
            keys = slice(j * blk, (j + 2) * blk)
            vcat = jnp.concatenate([vs_ref[sq, 0, keys, cols], vs_ref[sq, 1, keys, cols]], axis=0)
            pcat = jnp.concatenate([pu[0].astype(BF16), pu[1].astype(BF16)], axis=1)
            num = jnp.dot(pcat, vcat, preferred_element_type=F32)
            o_ref[sq, rows, cols] = (num * jnp.where(low, 1.0 / du[0], 1.0 / du[1])).astype(BF16)
            lse_ref[sq, rows, cols] = jnp.where(low, mu[0] * LN2 + jnp.log(du[0]), mu[1] * LN2 + jnp.log(du[1]))


def _banded_attention(q, k, v, casts=()):
    nseq, length, w = q.shape
    blk = ATT_SPAN
    qb = min(length, ATT_STEP_QUERIES)
    nsb = ATT_STEP_QUERIES // qb
    per = qb // blk
    grid = (nseq // nsb, length // qb)
    n_steps = grid[0] * grid[1]
    cur = pl.BlockSpec((nsb, qb, w), lambda s, n: (s, n, 0))
    prev = pl.BlockSpec((nsb, blk, w), lambda s, n: (s, jnp.maximum(n * per - 1, 0), 0))
    cast_in_specs, cast_out_specs, cast_shapes = [], [], []
    for arr, window in casts:
        rows = arr.shape[0] // n_steps
        col_blk, width = (0, arr.shape[1]) if window is None else window
        cast_in_specs.append(pl.BlockSpec((rows, width), lambda s, n, cb=col_blk: (s * grid[1] + n, cb)))
        cast_out_specs.append(pl.BlockSpec((rows, width), lambda s, n: (s * grid[1] + n, 0)))
        cast_shapes.append(jax.ShapeDtypeStruct((arr.shape[0], width), BF16))
    outs = pl.pallas_call(
        functools.partial(_attn_kernel, n_cast=len(casts)),
        grid=grid,
        in_specs=[cur, prev, cur, prev, cur] + cast_in_specs,
        out_specs=[cur, cur] + cast_out_specs,
        out_shape=[jax.ShapeDtypeStruct(q.shape, BF16), jax.ShapeDtypeStruct(q.shape, F32)] + cast_shapes,
        scratch_shapes=[pltpu.VMEM((nsb, 2, blk + qb, w), BF16)] * 2,
        compiler_params=_cparams(2),
        name="attn",
    )(q, k, k, v, v, *[arr for arr, _ in casts])
    return outs[0], outs[1], outs[2:]


def _mix_kernel(y_ref, o0_ref, o1_ref, o2_ref, l0_ref, l1_ref, l2_ref, h_ref, wgs_ref, wga_ref, bgate_ref,
                wglu_ref, bglu_ref, wup_ref, wmix_ref, bmix_ref, g_ref, b_ref, out_ref, scr_ref, acc_ref):
    o_refs = (o0_ref, o1_ref, o2_ref)
    l_refs = (l0_ref, l1_ref, l2_ref)

    def tile(sub, rows):
        def natural(ref, dil):
            n = SUB_ROWS // dil
            if dil == 1:
                return ref[0, 0, rows, :].astype(F32)
            for r in range(dil):
                for e in range(ATT_MERGED // LANES):
                    scr_ref[sub, e, pl.ds(r, n, stride=dil), :] = (
                        ref[0, r, sub * n:(sub + 1) * n, e * LANES:(e + 1) * LANES].astype(F32))
            return jnp.concatenate([scr_ref[sub, e] for e in range(ATT_MERGED // LANES)], axis=-1)

        ls = [natural(l_refs[gi], dil) for gi, dil in enumerate(DILATIONS)]
        m = jnp.maximum(jnp.maximum(ls[0], ls[1]), ls[2])
        es = [jnp.exp(l - m) for l in ls]
        att = es[0] * natural(o_refs[0], DILATIONS[0])
        for gi in (1, 2):
            att += es[gi] * natural(o_refs[gi], DILATIONS[gi])
        att = (att / (es[0] + es[1] + es[2])).astype(BF16)
        y = y_ref[rows, :]
        hb = h_ref[rows, :].astype(BF16)
        yield

        for c in range(D_MODEL // MIX_CHUNK):
            cols = slice(c * MIX_CHUNK, (c + 1) * MIX_CHUNK)
            gate_cols = slice(D_MODEL + c * MIX_CHUNK, D_MODEL + (c + 1) * MIX_CHUNK)
            val = jnp.dot(y, wglu_ref[:, cols], preferred_element_type=F32) + bglu_ref[:, cols]
            gate = jnp.dot(y, wglu_ref[:, gate_cols], preferred_element_type=F32) + bglu_ref[:, gate_cols]
            b_att = jnp.dot(att, wup_ref[:, cols], preferred_element_type=F32)
            g_ssm = jnp.dot(hb, wgs_ref[:, cols], preferred_element_type=F32) + bgate_ref[:, cols]
            g_att = jnp.dot(hb, wga_ref[:, cols], preferred_element_type=F32) + bgate_ref[:, gate_cols]
            mixed = _sigmoid(g_ssm) * (val * _sigmoid(gate)) + _sigmoid(g_att) * b_att
            acc_ref[rows, cols] = mixed.astype(BF16)
            yield

        r = jnp.dot(acc_ref[rows, :], wmix_ref[...], preferred_element_type=F32) + bmix_ref[...]
        yield
        out_ref[rows, :] = _layer_norm(DEEPNORM_ALPHA * h_ref[rows, :] + r, g_ref[...], b_ref[...])

    _stagger([tile(i, rows) for i, rows in _sub_tiles(h_ref.shape[0])], lag=STAGE_LAG)


def _mix(yg, outs, lses, h, wgs, wga, bgate, wglu, bglu, wup, wmix, bmix, g, b, seq):
    t = h.shape[0]
    tm = 2 * TOKEN_TILE
    per_b = seq // tm
    row = lambda n: pl.BlockSpec((tm, n), lambda i: (i, 0))
    full = lambda a: pl.BlockSpec(a.shape, lambda i: (0,) * a.ndim, pipeline_mode=pl.Buffered(1))
    grp = lambda d: pl.BlockSpec((1, d, tm // d, ATT_MERGED), lambda i: (i // per_b, 0, i % per_b, 0))
    consts = (wgs, wga, bgate, wglu, bglu, wup, wmix, bmix, g, b)
    return pl.pallas_call(
        _mix_kernel,
        grid=(t // tm,),
        in_specs=([row(SSM_WIDTH)] + [grp(d) for d in DILATIONS] * 2 + [row(D_MODEL)]
                  + [full(a) for a in consts]),
        out_specs=row(D_MODEL),
        out_shape=jax.ShapeDtypeStruct((t, D_MODEL), F32),
        scratch_shapes=[pltpu.VMEM((tm // SUB_ROWS, ATT_MERGED // LANES, SUB_ROWS, LANES), F32),
                        pltpu.VMEM((tm, D_MODEL), BF16)],
        compiler_params=_cparams(1),
        name="mix",
    )(yg, *outs, *lses, h, *consts)


def _tail_kernel(h_ref, mem_ref, wkv_ref, wq_ref, wo_ref, g2_ref, b2_ref,
                 w1_ref, bf1_ref, w2_ref, bf2_ref, g3_ref, b3_ref, out_ref, kv_ref, acc_ref, *, steps_per_batch):
    dn = (((1,), (1,)), ((), ()))

    @pl.when(pl.program_id(0) % steps_per_batch == 0)
    def _():
        kv_ref[...] = jnp.dot(mem_ref[0].astype(BF16), wkv_ref[...], preferred_element_type=F32).astype(BF16)

    def tile(sub, rows):
        h = h_ref[rows, :]
        hb = h.astype(BF16)
        yield
        q = (jnp.dot(hb, wq_ref[...], preferred_element_type=F32) * XATT_SCALE).astype(BF16)
        yield
        outs = []
        for hh in range(XATT_HEADS):
            sl = slice(hh * XATT_HEAD_DIM, (hh + 1) * XATT_HEAD_DIM)
            sv = slice(D_MODEL + hh * XATT_HEAD_DIM, D_MODEL + (hh + 1) * XATT_HEAD_DIM)
            s = lax.dot_general(q[:, sl], kv_ref[:, sl], dn, preferred_element_type=F32)
            e = jnp.exp(s - jnp.max(s, axis=-1, keepdims=True))
            p = e / jnp.sum(e, axis=-1, keepdims=True)
            outs.append(jnp.dot(p.astype(BF16), kv_ref[:, sv], preferred_element_type=F32).astype(BF16))
            yield
        xo = jnp.dot(jnp.concatenate(outs, axis=-1), wo_ref[...], preferred_element_type=F32)
        yield
        h = _layer_norm(DEEPNORM_ALPHA * h + xo, g2_ref[...], b2_ref[...])
        hb = h.astype(BF16)
        yield
        for c in range(D_FF // FFN_CHUNK):
            cols = slice(c * FFN_CHUNK, (c + 1) * FFN_CHUNK)
            a = jnp.dot(hb, w1_ref[:, cols], preferred_element_type=F32) + bf1_ref[:, cols]
            a = jnp.square(jnp.maximum(a, 0.0)).astype(BF16)
            part = jnp.dot(a, w2_ref[cols, :], preferred_element_type=F32)
            if c == 0:
                acc_ref[rows, :] = part
            else:
                acc_ref[rows, :] += part
            yield
        ff = acc_ref[rows, :] + bf2_ref[...]
        out_ref[rows, :] = _layer_norm(DEEPNORM_ALPHA * h + ff, g3_ref[...], b3_ref[...])

    _stagger([tile(i, rows) for i, rows in _sub_tiles(h_ref.shape[0])], lag=STAGE_LAG)


def _tail(h, mem, wkv, wq, wo, g2, b2, w1, bf1, w2, bf2, g3, b3, seq):
    t = h.shape[0]
    tm = 2 * TOKEN_TILE
    per_b = seq // tm
    row = pl.BlockSpec((tm, D_MODEL), lambda i: (i, 0))
    full = lambda a: pl.BlockSpec(a.shape, lambda i: (0,) * a.ndim, pipeline_mode=pl.Buffered(1))
    consts = (wkv, wq, wo, g2, b2, w1, bf1, w2, bf2, g3, b3)
    return pl.pallas_call(
        functools.partial(_tail_kernel, steps_per_batch=per_b),
        grid=(t // tm,),
        in_specs=[row, pl.BlockSpec((1,) + mem.shape[1:], lambda i: (i // per_b, 0, 0))] + [full(a) for a in consts],
        out_specs=row,
        out_shape=jax.ShapeDtypeStruct((t, D_MODEL), F32),
        scratch_shapes=[pltpu.VMEM((mem.shape[1], 2 * D_MODEL), BF16), pltpu.VMEM((tm, D_MODEL), F32)],
        compiler_params=pltpu.CompilerParams(dimension_semantics=("arbitrary",), vmem_limit_bytes=VMEM_LIMIT),
        name="tail",
    )(h, mem, *consts)


def _rope_selectors():
    half = ROT_DIM // 2
    sel = np.zeros((ROT_DIM, 3 * LANES), np.float32)
    one = np.zeros((1, LANES), np.float32)
    for lane in range(LANES):
        d = lane % ATT_HEAD_DIM
        if d >= ROT_DIM:
            one[0, lane] = 1.0
            continue
        sel[d % half, lane] = 1.0
        if d < half:
            sel[half + d, LANES + lane] = -1.0
        else:
            sel[d, 2 * LANES + lane] = 1.0
    return jnp.asarray(sel, BF16), jnp.asarray(one)


def _rope_cos_sin(positions):
    inv_freq = ROPE_THETA ** (-jnp.arange(0, ROT_DIM, 2, dtype=F32) / ROT_DIM)
    ang = positions.astype(F32).reshape(-1, 1) * inv_freq
    return jnp.concatenate([jnp.cos(ang), jnp.sin(ang)], axis=-1)


def kernel(x, mem, positions, ln_in_g, ln_in_b, w_in, b_in, ssm_log_dt, ssm_a_re, ssm_a_im, ssm_b_re, ssm_b_im, ssm_c_re, ssm_c_im, ssm_d, w_glu, b_glu, w_att_up, w_mix_out, b_mix_out, ln1_g, ln1_b, w_xq, w_xkv, w_xo, ln2_g, ln2_b, w_ff1, b_ff1, w_ff2, b_ff2, ln3_g, ln3_b):
    bsz, seq, _ = x.shape
    t = bsz * seq
    row2 = lambda a: a.reshape(1, -1).astype(F32)
    cs = _rope_cos_sin(positions)
    h = x.reshape(t, D_MODEL)
    for l in range(DEPTH):
        h, uflat, *qkv = _ln_proj(
            h, row2(ln_in_g), row2(ln_in_b), w_in[l, :, :GATE_OFF].astype(BF16), row2(b_in[l, :GATE_OFF]),
            cs, bsz, seq)

        ssm_params = _ssm_matrices(ssm_log_dt[l], ssm_a_re[l], ssm_a_im[l], ssm_b_re[l],
                                   ssm_b_im[l], ssm_c_re[l], ssm_c_im[l], ssm_d[l])
        yg = _ssm(uflat, ssm_params, bsz)

        gate_cb = GATE_OFF // D_MODEL
        casts = ([(w_ff1[l], None)],
                 [(w_ff2[l], None)],
                 [(w_in[l], (gate_cb, D_MODEL)), (w_in[l], (gate_cb + 1, D_MODEL)), (w_glu[l], None),
                  (w_att_up[l], None), (w_mix_out[l], None), (w_xkv[l], None), (w_xq[l], None), (w_xo[l], None)])
        outs, lses, cast = [], [], []
        for gi, dil in enumerate(DILATIONS):
            qg, kg, vg = (a.reshape(bsz * dil, seq // dil, ATT_MERGED) for a in qkv[3 * gi:3 * gi + 3])
            o_g, lse_g, cast_g = _banded_attention(qg, kg, vg, casts[gi])
            outs.append(o_g.reshape(bsz, dil, seq // dil, ATT_MERGED))
            lses.append(lse_g.reshape(bsz, dil, seq // dil, ATT_MERGED))
            cast.extend(cast_g)
        wb_ff1, wb_ff2, wb_gs, wb_ga, wb_glu, wb_up, wb_mix, wb_xkv, wb_xq, wb_xo = cast

        h = _mix(yg, outs, lses, h, wb_gs, wb_ga, row2(b_in[l, GATE_OFF:]), wb_glu, row2(b_glu[l]),
                 wb_up, wb_mix, row2(b_mix_out[l]), row2(ln1_g[l]), row2(ln1_b[l]), seq)

        h = _tail(h, mem, wb_xkv, wb_xq, wb_xo, row2(ln2_g[l]), row2(ln2_b[l]), wb_ff1, row2(b_ff1[l]),
                  wb_ff2, row2(b_ff2[l]), row2(ln3_g[l]), row2(ln3_b[l]), seq)
    return h.reshape(bsz, seq, D_MODEL)
```

```python
import functools
import math

import jax
import jax.numpy as jnp
import numpy as np
from jax import lax
from jax.experimental import pallas as pl
from jax.experimental.pallas import tpu as pltpu

F32 = jnp.float32
BF16 = jnp.bfloat16

D_MODEL = 1024
SSM_GROUP = 16
SSM_WIDTH = 768
SSM_GROUPS = SSM_WIDTH // SSM_GROUP
SSM_STATE = 64
ATT_HEAD_DIM = 64
ATT_HEADS_PER_GROUP = 4
DILATIONS = (1, 4, 16)
ATT_SPAN = 128
ATT_WIDTH = 768
ATT_MERGED = ATT_HEADS_PER_GROUP * ATT_HEAD_DIM
ATT_SCALE = ATT_HEAD_DIM ** -0.5
ROT_DIM = ATT_HEAD_DIM // 4
ROPE_THETA = 500000.0
XATT_HEADS = 4
XATT_HEAD_DIM = D_MODEL // XATT_HEADS
XATT_SCALE = XATT_HEAD_DIM ** -0.5
D_FF = 4 * D_MODEL
DEPTH = 1
DEEPNORM_ALPHA = (2 * DEPTH) ** 0.25
LN_EPS = 1e-5
NEG_INF = -1e30
LOG2E = math.log2(math.e)
LN2 = math.log(2.0)

LANES = 128
CHUNK = 16
SG_GROUPS = LANES // SSM_GROUP
N_SG = SSM_GROUPS // SG_GROUPS
FLAT = CHUNK * LANES
SG_STATE = SG_GROUPS * SSM_STATE
MXU_TILE = 256
N_TT = FLAT // MXU_TILE

SSM_Y_ROWS = 512
SUB_ROWS = 256
TOKEN_TILE = 512
STAGE_LAG = 1
MIX_CHUNK = 256
FFN_CHUNK = 1024
QKV_OFF = SSM_WIDTH
GATE_OFF = SSM_WIDTH + 3 * ATT_WIDTH
VMEM_LIMIT = 56 * 1024 * 1024


def _cparams(n_axes):
    return pltpu.CompilerParams(dimension_semantics=("parallel",) * n_axes,
                                vmem_limit_bytes=VMEM_LIMIT)


def _layer_norm(x, g, b):
    mu = jnp.mean(x, axis=-1, keepdims=True)
    xc = x - mu
    var = jnp.mean(xc * xc, axis=-1, keepdims=True)
    return xc * lax.rsqrt(var + LN_EPS) * g + b


def _sigmoid(x):
    return 0.5 * jnp.tanh(0.5 * x) + 0.5


def _stagger(tiles, lag):
    tiles = list(tiles)
    live = [True] * len(tiles)
    rnd = 0
    while any(live):
        for i in reversed(range(len(tiles))):
            if live[i] and rnd >= i * lag:
                try:
                    next(tiles[i])
                except StopIteration:
                    live[i] = False
        rnd += 1


def _sub_tiles(tm):
    return [(i, slice(i * SUB_ROWS, (i + 1) * SUB_ROWS)) for i in range(tm // SUB_ROWS)]


def _ln_proj_kernel(x_ref, g_ref, b_ref, w_ref, bi_ref, cs_ref, sel_ref, one_ref,
                    h_ref, u_ref, q0_ref, k0_ref, v0_ref, q1_ref, k1_ref, v1_ref, q2_ref, k2_ref, v2_ref,
                    scr_ref):
    half = ROT_DIM // 2
    qkv_refs = ((q0_ref, k0_ref, v0_ref), (q1_ref, k1_ref, v1_ref), (q2_ref, k2_ref, v2_ref))
    sections = [(0, SSM_WIDTH), (QKV_OFF, ATT_WIDTH), (QKV_OFF + ATT_WIDTH, ATT_WIDTH),
                (QKV_OFF + 2 * ATT_WIDTH, ATT_WIDTH)]

    def tile(sub, rows):
        h = _layer_norm(x_ref[rows, :], g_ref[...], b_ref[...])
        h_ref[rows, :] = h
        hb = h.astype(BF16)
        cs = cs_ref[rows, :]
        cs_hi = cs.astype(BF16)
        cs_lo = (cs - cs_hi.astype(F32)).astype(BF16)
        tab = (jnp.dot(cs_hi, sel_ref[...], preferred_element_type=F32)
               + jnp.dot(cs_lo, sel_ref[...], preferred_element_type=F32))
        cc = tab[:, 0:LANES] + one_ref[...]
        s1 = tab[:, LANES:2 * LANES]
        s2 = tab[:, 2 * LANES:3 * LANES]
        yield

        def stage(off, val):
            for i in range(val.shape[1] // LANES):
                scr_ref[sub, off // LANES + i] = val[:, i * LANES:(i + 1) * LANES]

        def rope(val, scale):
            tc, t1, t2 = (cc, s1, s2) if scale == 1.0 else (cc * scale, s1 * scale, s2 * scale)
            out = []
            for i in range(ATT_WIDTH // LANES):
                t = val[:, i * LANES:(i + 1) * LANES]
                out.append(t * tc + pltpu.roll(t, LANES - half, 1) * t1 + pltpu.roll(t, half, 1) * t2)
            return jnp.concatenate(out, axis=-1)

        def emit_groups(a):
            for gi, dil in enumerate(DILATIONS):
                ref = qkv_refs[gi][a]
                cb0 = (QKV_OFF + a * ATT_WIDTH + gi * ATT_MERGED) // LANES
                n = SUB_ROWS // dil
                for r in range(dil):
                    for e in range(ATT_MERGED // LANES):
                        if dil == 1:
                            blk = scr_ref[sub, cb0 + e]
                        else:
                            blk = scr_ref[sub, cb0 + e, pl.ds(r, n, stride=dil), :]
                        ref[0, r, sub * n:(sub + 1) * n, e * LANES:(e + 1) * LANES] = blk.astype(BF16)

        for c, (off, width) in enumerate(sections):
            val = (jnp.dot(hb, w_ref[:, off:off + width], preferred_element_type=F32)
                   + bi_ref[:, off:off + width])
            if c == 0:
                stage(off, val)
                n = SUB_ROWS // CHUNK
                for j in range(CHUNK):
                    for sg in range(N_SG):
                        blk = scr_ref[sub, sg, pl.ds(j, n, stride=CHUNK), :]
                        u_ref[sg, sub * n:(sub + 1) * n, j * LANES:(j + 1) * LANES] = blk.astype(BF16)
            elif c in (1, 2):
                stage(off, rope(val, ATT_SCALE * LOG2E if c == 1 else 1.0))
                emit_groups(c - 1)
            else:
                stage(off, val)
                emit_groups(2)
            yield

    _stagger([tile(i, rows) for i, rows in _sub_tiles(x_ref.shape[0])], lag=STAGE_LAG)


def _ln_proj(x2, g, b, w, bi, cs, bsz, seq):
    t = x2.shape[0]
    tm = 2 * TOKEN_TILE
    per_b = seq // tm
    sel, one = _rope_selectors()
    row = lambda n: pl.BlockSpec((tm, n), lambda i: (i, 0))
    full = lambda a: pl.BlockSpec(a.shape, lambda i: (0,) * a.ndim, pipeline_mode=pl.Buffered(1))
    grp = lambda d: pl.BlockSpec((1, d, tm // d, ATT_MERGED), lambda i: (i // per_b, 0, i % per_b, 0))
    outs = [jax.ShapeDtypeStruct((t, D_MODEL), F32),
            jax.ShapeDtypeStruct((N_SG, t // CHUNK, FLAT), BF16)]
    out_specs = [row(D_MODEL), pl.BlockSpec((N_SG, tm // CHUNK, FLAT), lambda i: (0, i, 0))]
    for d in DILATIONS:
        outs += [jax.ShapeDtypeStruct((bsz, d, seq // d, ATT_MERGED), BF16)] * 3
        out_specs += [grp(d)] * 3
    return pl.pallas_call(
        _ln_proj_kernel,
        grid=(t // tm,),
        in_specs=[row(D_MODEL), full(g), full(b), full(w), full(bi), row(ROT_DIM), full(sel), full(one)],
        out_specs=out_specs,
        out_shape=outs,
        scratch_shapes=[pltpu.VMEM((tm // SUB_ROWS, GATE_OFF // LANES, SUB_ROWS, LANES), F32)],
        compiler_params=_cparams(1),
        name="ln_proj",
    )(x2, g, b, w, bi, cs, sel, one)


def _ssm_matrices(log_dt, a_re, a_im, b_re, b_im, c_re, c_im, d):
    g, n, c = SSM_GROUPS, SSM_STATE, SSM_GROUP
    dt = jnp.exp(log_dt.astype(F32))[:, None]
    a_re = a_re.astype(F32)
    a_im = a_im.astype(F32)
    ks = jnp.arange(CHUNK + 1, dtype=F32)
    mag = jnp.exp((a_re * dt)[..., None] * ks)
    ang = (a_im * dt)[..., None] * ks
    pw_re = mag * jnp.cos(ang)
    pw_im = mag * jnp.sin(ang)
    ab_re, ab_im = pw_re[..., 1], pw_im[..., 1]
    den = jnp.square(a_re) + jnp.square(a_im)
    nr = ab_re - 1.0
    f_re = (nr * a_re + ab_im * a_im) / den
    f_im = (ab_im * a_re - nr * a_im) / den
    b_re = b_re.astype(F32)
    b_im = b_im.astype(F32)
    bb_re = f_re[..., None] * b_re - f_im[..., None] * b_im
    bb_im = f_re[..., None] * b_im + f_im[..., None] * b_re
    pw = jnp.stack([pw_re, pw_im])
    pwrep = jnp.transpose(pw.reshape(2, N_SG, SG_GROUPS, n, CHUNK + 1), (1, 0, 4, 2, 3))
    pwrep = jnp.broadcast_to(pwrep[:, :, :, :, None, :], (N_SG, 2, CHUNK + 1, SG_GROUPS, c, n))
    pwrep = pwrep.reshape(N_SG, 2, CHUNK + 1, LANES, n)
    pwcol = jnp.transpose(pw.reshape(2, N_SG, SG_STATE, CHUNK + 1), (1, 0, 2, 3))
    bbt = jnp.transpose(jnp.stack([bb_re, bb_im]), (0, 1, 3, 2)).reshape(2, N_SG, LANES, n).transpose(1, 0, 2, 3)
    cs = jnp.transpose(jnp.stack([c_re.astype(F32), c_im.astype(F32)]), (0, 1, 3, 2))
    cs = cs.reshape(2, N_SG, SG_STATE, c).transpose(1, 0, 2, 3)
    dmat = (d.astype(F32).reshape(g, c, 1) * jnp.eye(c, dtype=F32)[None]).reshape(N_SG, LANES, c)
    ar = pw_re[..., CHUNK].reshape(N_SG, 1, SG_STATE)
    ai = pw_im[..., CHUNK].reshape(N_SG, 1, SG_STATE)
    return pwrep, pwcol, bbt, cs, dmat, ar, ai


def _ssm_kernel(u_ref, pwrep_ref, pwcol_ref, bbt_ref, cs_ref, dmat_ref, ar_ref, ai_ref, y_ref,
                tt_ref, wp_ref, wc_ref, p_ref, scr_ref, toep_ref, *, steps, nseq):
    def iota(shape, dim):
        return lax.broadcasted_iota(jnp.int32, shape, dim)

    def spread(nrows):
        r, cidx = iota((nrows, nrows * SG_GROUPS), 0), iota((nrows, nrows * SG_GROUPS), 1)
        return jnp.where(((r >> 4) == (cidx >> 7)) & ((r & 15) == (cidx & 15)), 1.0, 0.0).astype(BF16)

    grp16 = lambda idx: (idx >> 4) & (SG_GROUPS - 1)
    grp64 = lambda idx: (idx >> 6) & (SG_GROUPS - 1)


    r, cidx = iota((SSM_STATE, SG_STATE), 0), iota((SSM_STATE, SG_STATE), 1)
    e_p = jnp.where(r == (cidx & (SSM_STATE - 1)), 1.0, 0.0).astype(BF16)
    keep = grp16(iota((LANES, SG_STATE), 0)) == grp64(iota((LANES, SG_STATE), 1))
    b_r, b_i = bbt_ref[0, 0], bbt_ref[0, 1]
    for s in range(CHUNK):
        a_r, a_i = pwrep_ref[0, 0, CHUNK - 1 - s], pwrep_ref[0, 1, CHUNK - 1 - s]
        for part, v in enumerate((a_r * b_r - a_i * b_i, a_r * b_i + a_i * b_r)):
            full = jnp.dot(v.astype(BF16), e_p, preferred_element_type=F32)
            wp_ref[s * LANES:(s + 1) * LANES, part * SG_STATE:(part + 1) * SG_STATE] = (
                jnp.where(keep, full, 0.0).astype(BF16))

    c_r, c_i = cs_ref[0, 0], cs_ref[0, 1]
    cstack = jnp.concatenate([c_r, -c_i], axis=0).astype(BF16)
    e_t = spread(SSM_GROUP)
    keep = grp16(iota((LANES, LANES), 0)) == grp16(iota((LANES, LANES), 1))
    lag_blocks = []
    for k in range(CHUNK):
        slab = wp_ref[(CHUNK - 1 - k) * LANES:(CHUNK - k) * LANES, :]
        kk = jnp.dot(slab, cstack, preferred_element_type=F32)
        if k == 0:
            kk = kk + dmat_ref[0]
        lag_blocks.append(jnp.where(keep, jnp.dot(kk.astype(BF16), e_t, preferred_element_type=F32), 0.0)
                          .astype(BF16))
    for dlt in range(N_TT):
        for s in range(2):
            for t in range(2):
                k = 2 * dlt + t - s
                blk = lag_blocks[k] if k >= 0 else jnp.zeros((LANES, LANES), BF16)
                tt_ref[dlt, s * LANES:(s + 1) * LANES, t * LANES:(t + 1) * LANES] = blk

    keep = grp64(iota((SG_STATE, LANES), 0)) == grp16(iota((SG_STATE, LANES), 1))
    for t in range(CHUNK):
        a_r = pwcol_ref[0, 0, :, t + 1:t + 2]
        a_i = pwcol_ref[0, 1, :, t + 1:t + 2]
        for part, v in enumerate((c_r * a_r - c_i * a_i, -(c_r * a_i + c_i * a_r))):
            full = jnp.dot(v.astype(BF16), e_t, preferred_element_type=F32)
            wc_ref[part * SG_STATE:(part + 1) * SG_STATE, t * LANES:(t + 1) * LANES] = (
                jnp.where(keep, full, 0.0).astype(BF16))

    p_ref[...] = jnp.dot(u_ref[0], wp_ref[...], preferred_element_type=F32)

    ar = ar_ref[0]
    ai = ai_ref[0]

    def scan_step(c, carry):
        new = []
        for b in range(nseq):
            hr, hi = carry[b]
            row = b * steps + c
            pr = p_ref[row:row + 1, 0:SG_STATE]
            pi = p_ref[row:row + 1, SG_STATE:2 * SG_STATE]
            p_ref[row:row + 1, 0:SG_STATE] = hr
            p_ref[row:row + 1, SG_STATE:2 * SG_STATE] = hi
            new.append((ar * hr - ai * hi + pr, ar * hi + ai * hr + pi))
        return new

    tn = scr_ref.shape[0] // CHUNK
    units = [(r0, jo) for r0 in range(0, steps * nseq, tn) for jo in range(N_TT)]
    steps_per_unit = steps // len(units)
    z = jnp.zeros((1, SG_STATE), F32)
    carry = [(z, z)] * nseq
    for ui, (r0, jo) in enumerate(units):
        rows = slice(r0, r0 + tn)
        acc = jnp.dot(u_ref[0, rows, 0:MXU_TILE], tt_ref[jo], preferred_element_type=F32)
        for ji in range(1, jo + 1):
            acc += jnp.dot(u_ref[0, rows, ji * MXU_TILE:(ji + 1) * MXU_TILE], tt_ref[jo - ji],
                           preferred_element_type=F32)
        toep_ref[rows, jo * MXU_TILE:(jo + 1) * MXU_TILE] = acc
        for c in range(ui * steps_per_unit, (ui + 1) * steps_per_unit):
            carry = scan_step(c, carry)

    for r0 in range(0, steps * nseq, tn):
        rows = slice(r0, r0 + tn)
        hb = p_ref[rows, :].astype(BF16)
        for jo in range(N_TT):
            cols = slice(jo * MXU_TILE, (jo + 1) * MXU_TILE)
            acc = toep_ref[rows, cols] + jnp.dot(hb, wc_ref[:, cols], preferred_element_type=F32)
            yg = jax.nn.gelu(acc, approximate=True)
            for e in range(MXU_TILE // LANES):
                j = jo * (MXU_TILE // LANES) + e
                scr_ref[pl.ds(j, tn, stride=CHUNK), :] = yg[:, e * LANES:(e + 1) * LANES]
        y_ref[r0 * CHUNK:(r0 + tn) * CHUNK, :] = scr_ref[...].astype(BF16)


def _ssm(uflat, params, bsz):
    nc = uflat.shape[1]
    blk = lambda a: pl.BlockSpec((1,) + a.shape[1:], lambda s: (s,) + (0,) * (a.ndim - 1))
    return pl.pallas_call(
        functools.partial(_ssm_kernel, steps=nc // bsz, nseq=bsz),
        grid=(N_SG,),
        in_specs=[blk(uflat)] + [blk(a) for a in params],
        out_specs=pl.BlockSpec((nc * CHUNK, LANES), lambda s: (0, s)),
        out_shape=jax.ShapeDtypeStruct((nc * CHUNK, SSM_WIDTH), BF16),
        scratch_shapes=[pltpu.VMEM((N_TT, MXU_TILE, MXU_TILE), BF16),
                        pltpu.VMEM((FLAT, 2 * SG_STATE), BF16),
                        pltpu.VMEM((2 * SG_STATE, FLAT), BF16),
                        pltpu.VMEM((nc, 2 * SG_STATE), F32),
                        pltpu.VMEM((SSM_Y_ROWS * CHUNK, LANES), F32),
                        pltpu.VMEM((nc, FLAT), F32)],
        compiler_params=_cparams(1),
        name="ssm",
    )(uflat, *params)


ATT_UNITS_IN_FLIGHT = 4
ATT_STEP_QUERIES = (4096, 4096, 2048)


def _attn_kernel(*refs, n_cast):
    q_ref, kp_ref, kc_ref, vp_ref, vc_ref = refs[:5]
    cast_in = refs[5:5 + n_cast]
    o_ref, lse_ref = refs[5 + n_cast:7 + n_cast]
    cast_out = refs[7 + n_cast:7 + 2 * n_cast]
    ks_ref, vs_ref = refs[7 + 2 * n_cast:]
    for src, dst in zip(cast_in, cast_out):
        dst[...] = src[...].astype(BF16)

    nsb, qb = q_ref.shape[0], q_ref.shape[1]
    blk = ATT_SPAN
    first = pl.program_id(1) == 0

    lane = lax.broadcasted_iota(jnp.int32, (1, ATT_MERGED), 1)
    head_sel = (lane % LANES < ATT_HEAD_DIM, lane % LANES >= ATT_HEAD_DIM)
    for sq in range(nsb):
        for src_p, src_c, dst in ((kp_ref, kc_ref, ks_ref), (vp_ref, vc_ref, vs_ref)):
            for e in range(2):
                xp, xc = src_p[sq], src_c[sq]
                dst[sq, e, 0:blk] = jnp.where(head_sel[e], xp, jnp.zeros_like(xp))
                dst[sq, e, blk:blk + qb] = jnp.where(head_sel[e], xc, jnp.zeros_like(xc))

    qi = lax.broadcasted_iota(jnp.int32, (blk, 2 * blk), 0)
    kk = lax.broadcasted_iota(jnp.int32, (blk, 2 * blk), 1)
    band = jnp.logical_and(kk >= qi, kk <= qi + blk)
    band_first = jnp.logical_and(band, jnp.logical_or(kk >= blk, jnp.logical_not(first)))
    low = lax.broadcasted_iota(jnp.int32, (blk, LANES), 1) < ATT_HEAD_DIM
    dn = (((1,), (1,)), ((), ()))

    units = [(sq, j, pair) for sq in range(nsb) for j in range(qb // blk) for pair in range(ATT_MERGED // LANES)]
    for g0 in range(0, len(units), ATT_UNITS_IN_FLIGHT):
        group = units[g0:g0 + ATT_UNITS_IN_FLIGHT]
        scores = []
        for sq, j, pair in group:
            cols = slice(pair * LANES, (pair + 1) * LANES)
            keys = slice(j * blk, (j + 2) * blk)
            kcat = jnp.concatenate([ks_ref[sq, 0, keys, cols], ks_ref[sq, 1, keys, cols]], axis=0)
            s = lax.dot_general(q_ref[sq, j * blk:(j + 1) * blk, cols], kcat, dn,
                                preferred_element_type=F32)
            mask = band_first if j == 0 else band
            scores.append([jnp.where(mask, s[:, e * 2 * blk:(e + 1) * 2 * blk], NEG_INF) for e in range(2)])
        maxes = [[jnp.max(jnp.maximum(se[:, :blk], se[:, blk:]), axis=-1, keepdims=True) for se in su]
                 for su in scores]
        probs = [[jnp.exp2(se - me) for se, me in zip(su, mu)] for su, mu in zip(scores, maxes)]
        dens = [[jnp.sum(pe[:, :blk] + pe[:, blk:], axis=-1, keepdims=True) for pe in pu] for pu in probs]
        for (sq, j, pair), pu, mu, du in zip(group, probs, maxes, dens):
            cols = slice(pair * LANES, (pair + 1) * LANES)
            rows = slice(j * blk, (j + 1) * blk)
            keys = slice(j * blk, (j + 2) * blk)
            vcat = jnp.concatenate([vs_ref[sq, 0, keys, cols], vs_ref[sq, 1, keys, cols]], axis=0)
            pcat = jnp.concatenate([pu[0].astype(BF16), pu[1].astype(BF16)], axis=1)
            num = jnp.dot(pcat, vcat, preferred_element_type=F32)
            o_ref[sq, rows, cols] = (num * jnp.where(low, 1.0 / du[0], 1.0 / du[1])).astype(BF16)
            lse_ref[sq, rows, cols] = jnp.where(low, mu[0] * LN2 + jnp.log(du[0]), mu[1] * LN2 + jnp.log(du[1]))


def _banded_attention(q, k, v, step_queries, casts=()):
    nseq, length, w = q.shape
    blk = ATT_SPAN
    qb = min(length, step_queries)
    nsb = step_queries // qb
    per = qb // blk
    grid = (nseq // nsb, length // qb)
    n_steps = grid[0] * grid[1]
    cur = pl.BlockSpec((nsb, qb, w), lambda s, n: (s, n, 0))
    prev = pl.BlockSpec((nsb, blk, w), lambda s, n: (s, jnp.maximum(n * per - 1, 0), 0))
    cast_in_specs, cast_out_specs, cast_shapes = [], [], []
    for arr, window in casts:
        rows = arr.shape[0] // n_steps
        col_blk, width = (0, arr.shape[1]) if window is None else window
        cast_in_specs.append(pl.BlockSpec((rows, width), lambda s, n, cb=col_blk: (s * grid[1] + n, cb)))
        cast_out_specs.append(pl.BlockSpec((rows, width), lambda s, n: (s * grid[1] + n, 0)))
        cast_shapes.append(jax.ShapeDtypeStruct((arr.shape[0], width), BF16))
    outs = pl.pallas_call(
        functools.partial(_attn_kernel, n_cast=len(casts)),
        grid=grid,
        in_specs=[cur, prev, cur, prev, cur] + cast_in_specs,
        out_specs=[cur, cur] + cast_out_specs,
        out_shape=[jax.ShapeDtypeStruct(q.shape, BF16), jax.ShapeDtypeStruct(q.shape, F32)] + cast_shapes,
        scratch_shapes=[pltpu.VMEM((nsb, 2, blk + qb, w), BF16)] * 2,
        compiler_params=_cparams(2),
        name="attn",
    )(q, k, k, v, v, *[arr for arr, _ in casts])
    return outs[0], outs[1], outs[2:]


def _mix_kernel(y_ref, o0_ref, o1_ref, o2_ref, l0_ref, l1_ref, l2_ref, h_ref, wgs_ref, wga_ref, bgate_ref,
                wglu_ref, bglu_ref, wup_ref, wmix_ref, bmix_ref, g_ref, b_ref, out_ref, scr_ref, acc_ref):
    o_refs = (o0_ref, o1_ref, o2_ref)
    l_refs = (l0_ref, l1_ref, l2_ref)

    def tile(sub, rows):
        def natural(ref, dil):
            n = SUB_ROWS // dil
            if dil == 1:
                return ref[0, 0, rows, :].astype(F32)
            for r in range(dil):
                for e in range(ATT_MERGED // LANES):
                    scr_ref[sub, e, pl.ds(r, n, stride=dil), :] = (
                        ref[0, r, sub * n:(sub + 1) * n, e * LANES:(e + 1) * LANES].astype(F32))
            return jnp.concatenate([scr_ref[sub, e] for e in range(ATT_MERGED // LANES)], axis=-1)

        ls = [natural(l_refs[gi], dil) for gi, dil in enumerate(DILATIONS)]
        m = jnp.maximum(jnp.maximum(ls[0], ls[1]), ls[2])
        es = [jnp.exp(l - m) for l in ls]
        att = es[0] * natural(o_refs[0], DILATIONS[0])
        for gi in (1, 2):
            att += es[gi] * natural(o_refs[gi], DILATIONS[gi])
        att = (att / (es[0] + es[1] + es[2])).astype(BF16)
        y = y_ref[rows, :]
        hb = h_ref[rows, :].astype(BF16)
        yield

        for c in range(D_MODEL // MIX_CHUNK):
            cols = slice(c * MIX_CHUNK, (c + 1) * MIX_CHUNK)
            gate_cols = slice(D_MODEL + c * MIX_CHUNK, D_MODEL + (c + 1) * MIX_CHUNK)
            val = jnp.dot(y, wglu_ref[:, cols], preferred_element_type=F32) + bglu_ref[:, cols]
            gate = jnp.dot(y, wglu_ref[:, gate_cols], preferred_element_type=F32) + bglu_ref[:, gate_cols]
            b_att = jnp.dot(att, wup_ref[:, cols], preferred_element_type=F32)
            g_ssm = jnp.dot(hb, wgs_ref[:, cols], preferred_element_type=F32) + bgate_ref[:, cols]
            g_att = jnp.dot(hb, wga_ref[:, cols], preferred_element_type=F32) + bgate_ref[:, gate_cols]
            mixed = _sigmoid(g_ssm) * (val * _sigmoid(gate)) + _sigmoid(g_att) * b_att
            acc_ref[rows, cols] = mixed.astype(BF16)
            yield

        r = jnp.dot(acc_ref[rows, :], wmix_ref[...], preferred_element_type=F32) + bmix_ref[...]
        yield
        out_ref[rows, :] = _layer_norm(DEEPNORM_ALPHA * h_ref[rows, :] + r, g_ref[...], b_ref[...])

    _stagger([tile(i, rows) for i, rows in _sub_tiles(h_ref.shape[0])], lag=STAGE_LAG)


def _mix(yg, outs, lses, h, wgs, wga, bgate, wglu, bglu, wup, wmix, bmix, g, b, seq):
    t = h.shape[0]
    tm = 2 * TOKEN_TILE
    per_b = seq // tm
    row = lambda n: pl.BlockSpec((tm, n), lambda i: (i, 0))
    full = lambda a: pl.BlockSpec(a.shape, lambda i: (0,) * a.ndim, pipeline_mode=pl.Buffered(1))
    grp = lambda d: pl.BlockSpec((1, d, tm // d, ATT_MERGED), lambda i: (i // per_b, 0, i % per_b, 0))
    consts = (wgs, wga, bgate, wglu, bglu, wup, wmix, bmix, g, b)
    return pl.pallas_call(
        _mix_kernel,
        grid=(t // tm,),
        in_specs=([row(SSM_WIDTH)] + [grp(d) for d in DILATIONS] * 2 + [row(D_MODEL)]
                  + [full(a) for a in consts]),
        out_specs=row(D_MODEL),
        out_shape=jax.ShapeDtypeStruct((t, D_MODEL), F32),
        scratch_shapes=[pltpu.VMEM((tm // SUB_ROWS, ATT_MERGED // LANES, SUB_ROWS, LANES), F32),
                        pltpu.VMEM((tm, D_MODEL), BF16)],
        compiler_params=_cparams(1),
        name="mix",
    )(yg, *outs, *lses, h, *consts)


def _tail_kernel(h_ref, mem_ref, wkv_ref, wq_ref, wo_ref, g2_ref, b2_ref,
                 w1_ref, bf1_ref, w2_ref, bf2_ref, g3_ref, b3_ref, out_ref, kv_ref, acc_ref, *, steps_per_batch):
    dn = (((1,), (1,)), ((), ()))

    @pl.when(pl.program_id(0) % steps_per_batch == 0)
    def _():
        kv_ref[...] = jnp.dot(mem_ref[0].astype(BF16), wkv_ref[...], preferred_element_type=F32).astype(BF16)

    def tile(sub, rows):
        h = h_ref[rows, :]
        hb = h.astype(BF16)
        yield
        q = (jnp.dot(hb, wq_ref[...], preferred_element_type=F32) * XATT_SCALE).astype(BF16)
        yield
        outs = []
        for hh in range(XATT_HEADS):
            sl = slice(hh * XATT_HEAD_DIM, (hh + 1) * XATT_HEAD_DIM)
            sv = slice(D_MODEL + hh * XATT_HEAD_DIM, D_MODEL + (hh + 1) * XATT_HEAD_DIM)
            s = lax.dot_general(q[:, sl], kv_ref[:, sl], dn, preferred_element_type=F32)
            e = jnp.exp(s - jnp.max(s, axis=-1, keepdims=True))
            p = e / jnp.sum(e, axis=-1, keepdims=True)
            outs.append(jnp.dot(p.astype(BF16), kv_ref[:, sv], preferred_element_type=F32).astype(BF16))
            yield
        xo = jnp.dot(jnp.concatenate(outs, axis=-1), wo_ref[...], preferred_element_type=F32)
        yield
        h = _layer_norm(DEEPNORM_ALPHA * h + xo, g2_ref[...], b2_ref[...])
        hb = h.astype(BF16)
        yield
        for c in range(D_FF // FFN_CHUNK):
            cols = slice(c * FFN_CHUNK, (c + 1) * FFN_CHUNK)
            a = jnp.dot(hb, w1_ref[:, cols], preferred_element_type=F32) + bf1_ref[:, cols]
            a = jnp.square(jnp.maximum(a, 0.0)).astype(BF16)
            part = jnp.dot(a, w2_ref[cols, :], preferred_element_type=F32)
            if c == 0:
                acc_ref[rows, :] = part
            else:
                acc_ref[rows, :] += part
            yield
        ff = acc_ref[rows, :] + bf2_ref[...]
        out_ref[rows, :] = _layer_norm(DEEPNORM_ALPHA * h + ff, g3_ref[...], b3_ref[...])

    _stagger([tile(i, rows) for i, rows in _sub_tiles(h_ref.shape[0])], lag=STAGE_LAG)


def _tail(h, mem, wkv, wq, wo, g2, b2, w1, bf1, w2, bf2, g3, b3, seq):
    t = h.shape[0]
    tm = 2 * TOKEN_TILE
    per_b = seq // tm
    row = pl.BlockSpec((tm, D_MODEL), lambda i: (i, 0))
    full = lambda a: pl.BlockSpec(a.shape, lambda i: (0,) * a.ndim, pipeline_mode=pl.Buffered(1))
    consts = (wkv, wq, wo, g2, b2, w1, bf1, w2, bf2, g3, b3)
    return pl.pallas_call(
        functools.partial(_tail_kernel, steps_per_batch=per_b),
        grid=(t // tm,),
        in_specs=[row, pl.BlockSpec((1,) + mem.shape[1:], lambda i: (i // per_b, 0, 0))] + [full(a) for a in consts],
        out_specs=row,
        out_shape=jax.ShapeDtypeStruct((t, D_MODEL), F32),
        scratch_shapes=[pltpu.VMEM((mem.shape[1], 2 * D_MODEL), BF16), pltpu.VMEM((tm, D_MODEL), F32)],
        compiler_params=pltpu.CompilerParams(dimension_semantics=("arbitrary",), vmem_limit_bytes=VMEM_LIMIT),
        name="tail",
    )(h, mem, *consts)


def _rope_selectors():
    half = ROT_DIM // 2
    sel = np.zeros((ROT_DIM, 3 * LANES), np.float32)
    one = np.zeros((1, LANES), np.float32)
    for lane in range(LANES):
        d = lane % ATT_HEAD_DIM
        if d >= ROT_DIM:
            one[0, lane] = 1.0
            continue
        sel[d % half, lane] = 1.0
        if d < half:
            sel[half + d, LANES + lane] = -1.0
        else:
            sel[d, 2 * LANES + lane] = 1.0
    return jnp.asarray(sel, BF16), jnp.asarray(one)


def _rope_cos_sin(positions):
    inv_freq = ROPE_THETA ** (-jnp.arange(0, ROT_DIM, 2, dtype=F32) / ROT_DIM)
    ang = positions.astype(F32).reshape(-1, 1) * inv_freq
    return jnp.concatenate([jnp.cos(ang), jnp.sin(ang)], axis=-1)


def kernel(x, mem, positions, ln_in_g, ln_in_b, w_in, b_in, ssm_log_dt, ssm_a_re, ssm_a_im, ssm_b_re, ssm_b_im, ssm_c_re, ssm_c_im, ssm_d, w_glu, b_glu, w_att_up, w_mix_out, b_mix_out, ln1_g, ln1_b, w_xq, w_xkv, w_xo, ln2_g, ln2_b, w_ff1, b_ff1, w_ff2, b_ff2, ln3_g, ln3_b):
    bsz, seq, _ = x.shape
    t = bsz * seq
    row2 = lambda a: a.reshape(1, -1).astype(F32)
    cs = _rope_cos_sin(positions)
    h = x.reshape(t, D_MODEL)
    for l in range(DEPTH):
        h, uflat, *qkv = _ln_proj(
            h, row2(ln_in_g), row2(ln_in_b), w_in[l, :, :GATE_OFF].astype(BF16), row2(b_in[l, :GATE_OFF]),
            cs, bsz, seq)

        ssm_params = _ssm_matrices(ssm_log_dt[l], ssm_a_re[l], ssm_a_im[l], ssm_b_re[l],
                                   ssm_b_im[l], ssm_c_re[l], ssm_c_im[l], ssm_d[l])
        yg = _ssm(uflat, ssm_params, bsz)

        gate_cb = GATE_OFF // D_MODEL
        casts = ([(w_ff1[l], None)],
                 [(w_ff2[l], None)],
                 [(w_in[l], (gate_cb, D_MODEL)), (w_in[l], (gate_cb + 1, D_MODEL)), (w_glu[l], None),
                  (w_att_up[l], None), (w_mix_out[l], None), (w_xkv[l], None), (w_xq[l], None), (w_xo[l], None)])
        outs, lses, cast = [], [], []
        for gi, dil in enumerate(DILATIONS):
            qg, kg, vg = (a.reshape(bsz * dil, seq // dil, ATT_MERGED) for a in qkv[3 * gi:3 * gi + 3])
            o_g, lse_g, cast_g = _banded_attention(qg, kg, vg, ATT_STEP_QUERIES[gi], casts[gi])
            outs.append(o_g.reshape(bsz, dil, seq // dil, ATT_MERGED))
            lses.append(lse_g.reshape(bsz, dil, seq // dil, ATT_MERGED))
            cast.extend(cast_g)
        wb_ff1, wb_ff2, wb_gs, wb_ga, wb_glu, wb_up, wb_mix, wb_xkv, wb_xq, wb_xo = cast

        h = _mix(yg, outs, lses, h, wb_gs, wb_ga, row2(b_in[l, GATE_OFF:]), wb_glu, row2(b_glu[l]),
                 wb_up, wb_mix, row2(b_mix_out[l]), row2(ln1_g[l]), row2(ln1_b[l]), seq)

        h = _tail(h, mem, wb_xkv, wb_xq, wb_xo, row2(ln2_g[l]), row2(ln2_b[l]), wb_ff1, row2(b_ff1[l]),
                  wb_ff2, row2(b_ff2[l]), row2(ln3_g[l]), row2(ln3_b[l]), seq)
    return h.reshape(bsz, seq, D_MODEL)
```

```python
import functools
import math

import jax
import jax.numpy as jnp
import numpy as np
from jax import lax
from jax.experimental import pallas as pl
from jax.experimental.pallas import tpu as pltpu

F32 = jnp.float32
BF16 = jnp.bfloat16

D_MODEL = 1024
SSM_GROUP = 16
SSM_WIDTH = 768
SSM_GROUPS = SSM_WIDTH // SSM_GROUP
SSM_STATE = 64
ATT_HEAD_DIM = 64
ATT_HEADS_PER_GROUP = 4
DILATIONS = (1, 4, 16)
ATT_SPAN = 128
ATT_WIDTH = 768
ATT_MERGED = ATT_HEADS_PER_GROUP * ATT_HEAD_DIM
ATT_SCALE = ATT_HEAD_DIM ** -0.5
ROT_DIM = ATT_HEAD_DIM // 4
ROPE_THETA = 500000.0
XATT_HEADS = 4
XATT_HEAD_DIM = D_MODEL // XATT_HEADS
XATT_SCALE = XATT_HEAD_DIM ** -0.5
D_FF = 4 * D_MODEL
DEPTH = 1
DEEPNORM_ALPHA = (2 * DEPTH) ** 0.25
LN_EPS = 1e-5
NEG_INF = -1e30
LOG2E = math.log2(math.e)
LN2 = math.log(2.0)

LANES = 128
CHUNK = 16
SG_GROUPS = LANES // SSM_GROUP
N_SG = SSM_GROUPS // SG_GROUPS
FLAT = CHUNK * LANES
SG_STATE = SG_GROUPS * SSM_STATE
MXU_TILE = 256
N_TT = FLAT // MXU_TILE

SSM_Y_ROWS = 512
SUB_ROWS = 256
TOKEN_TILE = 512
RELAYOUT_STRIDE = 4
STAGE_LAG = 1
MIX_CHUNK = 256
FFN_CHUNK = 1024
QKV_OFF = SSM_WIDTH
GATE_OFF = SSM_WIDTH + 3 * ATT_WIDTH
VMEM_LIMIT = 56 * 1024 * 1024


def _cparams(n_axes):
    return pltpu.CompilerParams(dimension_semantics=("parallel",) * n_axes,
                                vmem_limit_bytes=VMEM_LIMIT)


def _layer_norm(x, g, b):
    mu = jnp.mean(x, axis=-1, keepdims=True)
    xc = x - mu
    var = jnp.mean(xc * xc, axis=-1, keepdims=True)
    return xc * lax.rsqrt(var + LN_EPS) * g + b


def _sigmoid(x):
    return 0.5 * jnp.tanh(0.5 * x) + 0.5


def _stagger(tiles, lag):
    tiles = list(tiles)
    live = [True] * len(tiles)
    rnd = 0
    while any(live):
        for i in reversed(range(len(tiles))):
            if live[i] and rnd >= i * lag:
                try:
                    next(tiles[i])
                except StopIteration:
                    live[i] = False
        rnd += 1


def _sub_tiles(tm):
    return [(i, slice(i * SUB_ROWS, (i + 1) * SUB_ROWS)) for i in range(tm // SUB_ROWS)]


def _ln_proj_kernel(x_ref, g_ref, b_ref, w_ref, bi_ref, cs_ref, sel_ref, one_ref,
                    h_ref, u_ref, q0_ref, k0_ref, v0_ref, q1_ref, k1_ref, v1_ref, q2_ref, k2_ref, v2_ref,
                    scr_ref, tmp_ref):
    half = ROT_DIM // 2
    qkv_refs = ((q0_ref, k0_ref, v0_ref), (q1_ref, k1_ref, v1_ref), (q2_ref, k2_ref, v2_ref))
    sections = [(0, SSM_WIDTH), (QKV_OFF, ATT_WIDTH), (QKV_OFF + ATT_WIDTH, ATT_WIDTH),
                (QKV_OFF + 2 * ATT_WIDTH, ATT_WIDTH)]

    def tile(sub, rows):
        h = _layer_norm(x_ref[rows, :], g_ref[...], b_ref[...])
        h_ref[rows, :] = h
        hb = h.astype(BF16)
        cs = cs_ref[rows, :]
        cs_hi = cs.astype(BF16)
        cs_lo = (cs - cs_hi.astype(F32)).astype(BF16)
        tab = (jnp.dot(cs_hi, sel_ref[...], preferred_element_type=F32)
               + jnp.dot(cs_lo, sel_ref[...], preferred_element_type=F32))
        cc = tab[:, 0:LANES] + one_ref[...]
        s1 = tab[:, LANES:2 * LANES]
        s2 = tab[:, 2 * LANES:3 * LANES]
        yield

        def stage(off, val):
            for i in range(val.shape[1] // LANES):
                scr_ref[sub, off // LANES + i] = val[:, i * LANES:(i + 1) * LANES]

        def rope(val, scale):
            tc, t1, t2 = (cc, s1, s2) if scale == 1.0 else (cc * scale, s1 * scale, s2 * scale)
            out = []
            for i in range(ATT_WIDTH // LANES):
                t = val[:, i * LANES:(i + 1) * LANES]
                out.append(t * tc + pltpu.roll(t, LANES - half, 1) * t1 + pltpu.roll(t, half, 1) * t2)
            return jnp.concatenate(out, axis=-1)

        def rows_mod16(cb, slot, emit):
            for m in range(RELAYOUT_STRIDE):
                tmp_ref[sub, slot, m] = scr_ref[sub, cb, pl.ds(m, SUB_ROWS // RELAYOUT_STRIDE,
                                                               stride=RELAYOUT_STRIDE), :]
            for j in range(CHUNK):
                q, m = divmod(j, RELAYOUT_STRIDE)
                emit(j, tmp_ref[sub, slot, m, pl.ds(q, SUB_ROWS // CHUNK, stride=RELAYOUT_STRIDE), :])

        def emit_groups(a):
            for gi, dil in enumerate(DILATIONS):
                ref = qkv_refs[gi][a]
                cb0 = (QKV_OFF + a * ATT_WIDTH + gi * ATT_MERGED) // LANES
                n = SUB_ROWS // dil
                for e in range(ATT_MERGED // LANES):
                    def put(r, blk, ref=ref, n=n, e=e):
                        ref[0, r, sub * n:(sub + 1) * n, e * LANES:(e + 1) * LANES] = blk.astype(BF16)
                    if dil == 1:
                        put(0, scr_ref[sub, cb0 + e])
                    elif dil == RELAYOUT_STRIDE:
                        for r in range(dil):
                            put(r, scr_ref[sub, cb0 + e, pl.ds(r, n, stride=dil), :])
                    else:
                        rows_mod16(cb0 + e, a * (ATT_MERGED // LANES) + e, put)

        for c, (off, width) in enumerate(sections):
            val = (jnp.dot(hb, w_ref[:, off:off + width], preferred_element_type=F32)
                   + bi_ref[:, off:off + width])
            if c == 0:
                stage(off, val)
                n = SUB_ROWS // CHUNK
                for sg in range(N_SG):
                    def put_u(j, blk, sg=sg):
                        u_ref[sg, sub * n:(sub + 1) * n, j * LANES:(j + 1) * LANES] = blk.astype(BF16)
                    rows_mod16(sg, sg, put_u)
            elif c in (1, 2):
                stage(off, rope(val, ATT_SCALE * LOG2E if c == 1 else 1.0))
                emit_groups(c - 1)
            else:
                stage(off, val)
                emit_groups(2)
            yield

    _stagger([tile(i, rows) for i, rows in _sub_tiles(x_ref.shape[0])], lag=STAGE_LAG)


def _ln_proj(x2, g, b, w, bi, cs, bsz, seq):
    t = x2.shape[0]
    tm = 2 * TOKEN_TILE
    per_b = seq // tm
    sel, one = _rope_selectors()
    row = lambda n: pl.BlockSpec((tm, n), lambda i: (i, 0))
    full = lambda a: pl.BlockSpec(a.shape, lambda i: (0,) * a.ndim, pipeline_mode=pl.Buffered(1))
    grp = lambda d: pl.BlockSpec((1, d, tm // d, ATT_MERGED), lambda i: (i // per_b, 0, i % per_b, 0))
    outs = [jax.ShapeDtypeStruct((t, D_MODEL), F32),
            jax.ShapeDtypeStruct((N_SG, t // CHUNK, FLAT), BF16)]
    out_specs = [row(D_MODEL), pl.BlockSpec((N_SG, tm // CHUNK, FLAT), lambda i: (0, i, 0))]
    for d in DILATIONS:
        outs += [jax.ShapeDtypeStruct((bsz, d, seq // d, ATT_MERGED), BF16)] * 3
        out_specs += [grp(d)] * 3
    return pl.pallas_call(
        _ln_proj_kernel,
        grid=(t // tm,),
        in_specs=[row(D_MODEL), full(g), full(b), full(w), full(bi), row(ROT_DIM), full(sel), full(one)],
        out_specs=out_specs,
        out_shape=outs,
        scratch_shapes=[pltpu.VMEM((tm // SUB_ROWS, GATE_OFF // LANES, SUB_ROWS, LANES), F32),
                        pltpu.VMEM((tm // SUB_ROWS, SSM_WIDTH // LANES, RELAYOUT_STRIDE,
                                    SUB_ROWS // RELAYOUT_STRIDE, LANES), F32)],
        compiler_params=_cparams(1),
        name="ln_proj",
    )(x2, g, b, w, bi, cs, sel, one)


def _ssm_matrices(log_dt, a_re, a_im, b_re, b_im, c_re, c_im, d):
    g, n, c = SSM_GROUPS, SSM_STATE, SSM_GROUP
    dt = jnp.exp(log_dt.astype(F32))[:, None]
    a_re = a_re.astype(F32)
    a_im = a_im.astype(F32)
    ks = jnp.arange(CHUNK + 1, dtype=F32)
    mag = jnp.exp((a_re * dt)[..., None] * ks)
    ang = (a_im * dt)[..., None] * ks
    pw_re = mag * jnp.cos(ang)
    pw_im = mag * jnp.sin(ang)
    ab_re, ab_im = pw_re[..., 1], pw_im[..., 1]
    den = jnp.square(a_re) + jnp.square(a_im)
    nr = ab_re - 1.0
    f_re = (nr * a_re + ab_im * a_im) / den
    f_im = (ab_im * a_re - nr * a_im) / den
    b_re = b_re.astype(F32)
    b_im = b_im.astype(F32)
    bb_re = f_re[..., None] * b_re - f_im[..., None] * b_im
    bb_im = f_re[..., None] * b_im + f_im[..., None] * b_re
    pw = jnp.stack([pw_re, pw_im])
    pwrep = jnp.transpose(pw.reshape(2, N_SG, SG_GROUPS, n, CHUNK + 1), (1, 0, 4, 2, 3))
    pwrep = jnp.broadcast_to(pwrep[:, :, :, :, None, :], (N_SG, 2, CHUNK + 1, SG_GROUPS, c, n))
    pwrep = pwrep.reshape(N_SG, 2, CHUNK + 1, LANES, n)
    pwcol = jnp.transpose(pw.reshape(2, N_SG, SG_STATE, CHUNK + 1), (1, 0, 2, 3))
    bbt = jnp.transpose(jnp.stack([bb_re, bb_im]), (0, 1, 3, 2)).reshape(2, N_SG, LANES, n).transpose(1, 0, 2, 3)
    cs = jnp.transpose(jnp.stack([c_re.astype(F32), c_im.astype(F32)]), (0, 1, 3, 2))
    cs = cs.reshape(2, N_SG, SG_STATE, c).transpose(1, 0, 2, 3)
    dmat = (d.astype(F32).reshape(g, c, 1) * jnp.eye(c, dtype=F32)[None]).reshape(N_SG, LANES, c)
    ar = pw_re[..., CHUNK].reshape(N_SG, 1, SG_STATE)
    ai = pw_im[..., CHUNK].reshape(N_SG, 1, SG_STATE)
    return pwrep, pwcol, bbt, cs, dmat, ar, ai


def _ssm_kernel(u_ref, pwrep_ref, pwcol_ref, bbt_ref, cs_ref, dmat_ref, ar_ref, ai_ref, y_ref,
                tt_ref, wp_ref, wc_ref, p_ref, scr_ref, toep_ref, *, steps, nseq):
    def iota(shape, dim):
        return lax.broadcasted_iota(jnp.int32, shape, dim)

    def spread(nrows):
        r, cidx = iota((nrows, nrows * SG_GROUPS), 0), iota((nrows, nrows * SG_GROUPS), 1)
        return jnp.where(((r >> 4) == (cidx >> 7)) & ((r & 15) == (cidx & 15)), 1.0, 0.0).astype(BF16)

    grp16 = lambda idx: (idx >> 4) & (SG_GROUPS - 1)
    grp64 = lambda idx: (idx >> 6) & (SG_GROUPS - 1)


    r, cidx = iota((SSM_STATE, SG_STATE), 0), iota((SSM_STATE, SG_STATE), 1)
    e_p = jnp.where(r == (cidx & (SSM_STATE - 1)), 1.0, 0.0).astype(BF16)
    keep = grp16(iota((LANES, SG_STATE), 0)) == grp64(iota((LANES, SG_STATE), 1))
    b_r, b_i = bbt_ref[0, 0], bbt_ref[0, 1]
    for s in range(CHUNK):
        a_r, a_i = pwrep_ref[0, 0, CHUNK - 1 - s], pwrep_ref[0, 1, CHUNK - 1 - s]
        for part, v in enumerate((a_r * b_r - a_i * b_i, a_r * b_i + a_i * b_r)):
            full = jnp.dot(v.astype(BF16), e_p, preferred_element_type=F32)
            wp_ref[s * LANES:(s + 1) * LANES, part * SG_STATE:(part + 1) * SG_STATE] = (
                jnp.where(keep, full, 0.0).astype(BF16))

    c_r, c_i = cs_ref[0, 0], cs_ref[0, 1]
    cstack = jnp.concatenate([c_r, -c_i], axis=0).astype(BF16)
    e_t = spread(SSM_GROUP)
    keep = grp16(iota((LANES, LANES), 0)) == grp16(iota((LANES, LANES), 1))
    lag_blocks = []
    for k in range(CHUNK):
        slab = wp_ref[(CHUNK - 1 - k) * LANES:(CHUNK - k) * LANES, :]
        kk = jnp.dot(slab, cstack, preferred_element_type=F32)
        if k == 0:
            kk = kk + dmat_ref[0]
        lag_blocks.append(jnp.where(keep, jnp.dot(kk.astype(BF16), e_t, preferred_element_type=F32), 0.0)
                          .astype(BF16))
    for dlt in range(N_TT):
        for s in range(2):
            for t in range(2):
                k = 2 * dlt + t - s
                blk = lag_blocks[k] if k >= 0 else jnp.zeros((LANES, LANES), BF16)
                tt_ref[dlt, s * LANES:(s + 1) * LANES, t * LANES:(t + 1) * LANES] = blk

    keep = grp64(iota((SG_STATE, LANES), 0)) == grp16(iota((SG_STATE, LANES), 1))
    for t in range(CHUNK):
        a_r = pwcol_ref[0, 0, :, t + 1:t + 2]
        a_i = pwcol_ref[0, 1, :, t + 1:t + 2]
        for part, v in enumerate((c_r * a_r - c_i * a_i, -(c_r * a_i + c_i * a_r))):
            full = jnp.dot(v.astype(BF16), e_t, preferred_element_type=F32)
            wc_ref[part * SG_STATE:(part + 1) * SG_STATE, t * LANES:(t + 1) * LANES] = (
                jnp.where(keep, full, 0.0).astype(BF16))

    p_ref[...] = jnp.dot(u_ref[0], wp_ref[...], preferred_element_type=F32)

    ar = ar_ref[0]
    ai = ai_ref[0]

    def scan_step(c, carry):
        new = []
        for b in range(nseq):
            hr, hi = carry[b]
            row = b * steps + c
            pr = p_ref[row:row + 1, 0:SG_STATE]
            pi = p_ref[row:row + 1, SG_STATE:2 * SG_STATE]
            p_ref[row:row + 1, 0:SG_STATE] = hr
            p_ref[row:row + 1, SG_STATE:2 * SG_STATE] = hi
            new.append((ar * hr - ai * hi + pr, ar * hi + ai * hr + pi))
        return new

    tn = scr_ref.shape[0] // CHUNK
    units = [(r0, jo) for r0 in range(0, steps * nseq, tn) for jo in range(N_TT)]
    steps_per_unit = steps // len(units)
    z = jnp.zeros((1, SG_STATE), F32)
    carry = [(z, z)] * nseq
    for ui, (r0, jo) in enumerate(units):
        rows = slice(r0, r0 + tn)
        acc = jnp.dot(u_ref[0, rows, 0:MXU_TILE], tt_ref[jo], preferred_element_type=F32)
        for ji in range(1, jo + 1):
            acc += jnp.dot(u_ref[0, rows, ji * MXU_TILE:(ji + 1) * MXU_TILE], tt_ref[jo - ji],
                           preferred_element_type=F32)
        toep_ref[rows, jo * MXU_TILE:(jo + 1) * MXU_TILE] = acc
        for c in range(ui * steps_per_unit, (ui + 1) * steps_per_unit):
            carry = scan_step(c, carry)

    for r0 in range(0, steps * nseq, tn):
        rows = slice(r0, r0 + tn)
        hb = p_ref[rows, :].astype(BF16)
        for jo in range(N_TT):
            cols = slice(jo * MXU_TILE, (jo + 1) * MXU_TILE)
            acc = toep_ref[rows, cols] + jnp.dot(hb, wc_ref[:, cols], preferred_element_type=F32)
            yg = jax.nn.gelu(acc, approximate=True)
            for e in range(MXU_TILE // LANES):
                j = jo * (MXU_TILE // LANES) + e
                scr_ref[pl.ds(j, tn, stride=CHUNK), :] = yg[:, e * LANES:(e + 1) * LANES]
        y_ref[r0 * CHUNK:(r0 + tn) * CHUNK, :] = scr_ref[...].astype(BF16)


def _ssm(uflat, params, bsz):
    nc = uflat.shape[1]
    blk = lambda a: pl.BlockSpec((1,) + a.shape[1:], lambda s: (s,) + (0,) * (a.ndim - 1))
    return pl.pallas_call(
        functools.partial(_ssm_kernel, steps=nc // bsz, nseq=bsz),
        grid=(N_SG,),
        in_specs=[blk(uflat)] + [blk(a) for a in params],
        out_specs=pl.BlockSpec((nc * CHUNK, LANES), lambda s: (0, s)),
        out_shape=jax.ShapeDtypeStruct((nc * CHUNK, SSM_WIDTH), BF16),
        scratch_shapes=[pltpu.VMEM((N_TT, MXU_TILE, MXU_TILE), BF16),
                        pltpu.VMEM((FLAT, 2 * SG_STATE), BF16),
                        pltpu.VMEM((2 * SG_STATE, FLAT), BF16),
                        pltpu.VMEM((nc, 2 * SG_STATE), F32),
                        pltpu.VMEM((SSM_Y_ROWS * CHUNK, LANES), F32),
                        pltpu.VMEM((nc, FLAT), F32)],
        compiler_params=_cparams(1),
        name="ssm",
    )(uflat, *params)


ATT_UNITS_IN_FLIGHT = 4
ATT_STEP_QUERIES = 2048


def _attn_kernel(*refs, n_cast):
    q_ref, kp_ref, kc_ref, vp_ref, vc_ref = refs[:5]
    cast_in = refs[5:5 + n_cast]
    o_ref, lse_ref = refs[5 + n_cast:7 + n_cast]
    cast_out = refs[7 + n_cast:7 + 2 * n_cast]
    ks_ref, vs_ref = refs[7 + 2 * n_cast:]
    for src, dst in zip(cast_in, cast_out):
        dst[...] = src[...].astype(BF16)

    nsb, qb = q_ref.shape[0], q_ref.shape[1]
    blk = ATT_SPAN
    first = pl.program_id(1) == 0

    lane = lax.broadcasted_iota(jnp.int32, (1, ATT_MERGED), 1)
    head_sel = (lane % LANES < ATT_HEAD_DIM, lane % LANES >= ATT_HEAD_DIM)
    for sq in range(nsb):
        for src_p, src_c, dst in ((kp_ref, kc_ref, ks_ref), (vp_ref, vc_ref, vs_ref)):
            for e in range(2):
                xp, xc = src_p[sq], src_c[sq]
                dst[sq, e, 0:blk] = jnp.where(head_sel[e], xp, jnp.zeros_like(xp))
                dst[sq, e, blk:blk + qb] = jnp.where(head_sel[e], xc, jnp.zeros_like(xc))

    qi = lax.broadcasted_iota(jnp.int32, (blk, 2 * blk), 0)
    kk = lax.broadcasted_iota(jnp.int32, (blk, 2 * blk), 1)
    band = jnp.logical_and(kk >= qi, kk <= qi + blk)
    band_first = jnp.logical_and(band, jnp.logical_or(kk >= blk, jnp.logical_not(first)))
    low = lax.broadcasted_iota(jnp.int32, (blk, LANES), 1) < ATT_HEAD_DIM
    dn = (((1,), (1,)), ((), ()))

    units = [(sq, j, pair) for sq in range(nsb) for j in range(qb // blk) for pair in range(ATT_MERGED // LANES)]
    for g0 in range(0, len(units), ATT_UNITS_IN_FLIGHT):
        group = units[g0:g0 + ATT_UNITS_IN_FLIGHT]
        scores = []
        for sq, j, pair in group:
            cols = slice(pair * LANES, (pair + 1) * LANES)
            keys = slice(j * blk, (j + 2) * blk)
            kcat = jnp.concatenate([ks_ref[sq, 0, keys, cols], ks_ref[sq, 1, keys, cols]], axis=0)
            s = lax.dot_general(q_ref[sq, j * blk:(j + 1) * blk, cols], kcat, dn,
                                preferred_element_type=F32)
            mask = band_first if j == 0 else band
            scores.append([jnp.where(mask, s[:, e * 2 * blk:(e + 1) * 2 * blk], NEG_INF) for e in range(2)])
        maxes = [[jnp.max(jnp.maximum(se[:, :blk], se[:, blk:]), axis=-1, keepdims=True) for se in su]
                 for su in scores]
        probs = [[jnp.exp2(se - me) for se, me in zip(su, mu)] for su, mu in zip(scores, maxes)]
        dens = [[jnp.sum(pe[:, :blk] + pe[:, blk:], axis=-1, keepdims=True) for pe in pu] for pu in probs]
        for (sq, j, pair), pu, mu, du in zip(group, probs, maxes, dens):
            cols = slice(pair * LANES, (pair + 1) * LANES)
            rows = slice(j * blk, (j + 1) * blk)
            keys = slice(j * blk, (j + 2) * blk)
            vcat = jnp.concatenate([vs_ref[sq, 0, keys, cols], vs_ref[sq, 1, keys, cols]], axis=0)
            pcat = jnp.concatenate([pu[0].astype(BF16), pu[1].astype(BF16)], axis=1)
            num = jnp.dot(pcat, vcat, preferred_element_type=F32)
            o_ref[sq, rows, cols] = (num * jnp.where(low, 1.0 / du[0], 1.0 / du[1])).astype(BF16)
            lse_ref[sq, rows, cols] = jnp.where(low, mu[0] * LN2 + jnp.log(du[0]), mu[1] * LN2 + jnp.log(du[1]))


def _banded_attention(q, k, v, casts=()):
    nseq, length, w = q.shape
    blk = ATT_SPAN
    qb = min(length, ATT_STEP_QUERIES)
    nsb = ATT_STEP_QUERIES // qb
    per = qb // blk
    grid = (nseq // nsb, length // qb)
    n_steps = grid[0] * grid[1]
    cur = pl.BlockSpec((nsb, qb, w), lambda s, n: (s, n, 0))
    prev = pl.BlockSpec((nsb, blk, w), lambda s, n: (s, jnp.maximum(n * per - 1, 0), 0))
    cast_in_specs, cast_out_specs, cast_shapes = [], [], []
    for arr, window in casts:
        rows = arr.shape[0] // n_steps
        col_blk, width = (0, arr.shape[1]) if window is None else window
        cast_in_specs.append(pl.BlockSpec((rows, width), lambda s, n, cb=col_blk: (s * grid[1] + n, cb)))
        cast_out_specs.append(pl.BlockSpec((rows, width), lambda s, n: (s * grid[1] + n, 0)))
        cast_shapes.append(jax.ShapeDtypeStruct((arr.shape[0], width), BF16))
    outs = pl.pallas_call(
        functools.partial(_attn_kernel, n_cast=len(casts)),
        grid=grid,
        in_specs=[cur, prev, cur, prev, cur] + cast_in_specs,
        out_specs=[cur, cur] + cast_out_specs,
        out_shape=[jax.ShapeDtypeStruct(q.shape, BF16), jax.ShapeDtypeStruct(q.shape, F32)] + cast_shapes,
        scratch_shapes=[pltpu.VMEM((nsb, 2, blk + qb, w), BF16)] * 2,
        compiler_params=_cparams(2),
        name="attn",
    )(q, k, k, v, v, *[arr for arr, _ in casts])
    return outs[0], outs[1], outs[2:]


def _mix_kernel(y_ref, o0_ref, o1_ref, o2_ref, l0_ref, l1_ref, l2_ref, h_ref, wgs_ref, wga_ref, bgate_ref,
                wglu_ref, bglu_ref, wup_ref, wmix_ref, bmix_ref, g_ref, b_ref, out_ref, scr_ref, tmp_ref, acc_ref):
    o_refs = (o0_ref, o1_ref, o2_ref)
    l_refs = (l0_ref, l1_ref, l2_ref)

    def tile(sub, rows):
        def natural(ref, dil):
            n = SUB_ROWS // dil
            if dil == 1:
                return ref[0, 0, rows, :].astype(F32)
            for e in range(ATT_MERGED // LANES):
                piece = lambda r: ref[0, r, sub * n:(sub + 1) * n, e * LANES:(e + 1) * LANES].astype(F32)
                if dil == RELAYOUT_STRIDE:
                    for r in range(dil):
                        scr_ref[sub, e, pl.ds(r, n, stride=dil), :] = piece(r)
                else:
                    for m in range(RELAYOUT_STRIDE):
                        for q in range(dil // RELAYOUT_STRIDE):
                            tmp_ref[sub, e, m, pl.ds(q, n, stride=RELAYOUT_STRIDE), :] = (
                                piece(RELAYOUT_STRIDE * q + m))
                        scr_ref[sub, e, pl.ds(m, SUB_ROWS // RELAYOUT_STRIDE, stride=RELAYOUT_STRIDE), :] = (
                            tmp_ref[sub, e, m])
            return jnp.concatenate([scr_ref[sub, e] for e in range(ATT_MERGED // LANES)], axis=-1)

        ls = [natural(l_refs[gi], dil) for gi, dil in enumerate(DILATIONS)]
        m = jnp.maximum(jnp.maximum(ls[0], ls[1]), ls[2])
        es = [jnp.exp(l - m) for l in ls]
        att = es[0] * natural(o_refs[0], DILATIONS[0])
        for gi in (1, 2):
            att += es[gi] * natural(o_refs[gi], DILATIONS[gi])
        att = (att / (es[0] + es[1] + es[2])).astype(BF16)
        y = y_ref[rows, :]
        hb = h_ref[rows, :].astype(BF16)
        yield

        for c in range(D_MODEL // MIX_CHUNK):
            cols = slice(c * MIX_CHUNK, (c + 1) * MIX_CHUNK)
            gate_cols = slice(D_MODEL + c * MIX_CHUNK, D_MODEL + (c + 1) * MIX_CHUNK)
            val = jnp.dot(y, wglu_ref[:, cols], preferred_element_type=F32) + bglu_ref[:, cols]
            gate = jnp.dot(y, wglu_ref[:, gate_cols], preferred_element_type=F32) + bglu_ref[:, gate_cols]
            b_att = jnp.dot(att, wup_ref[:, cols], preferred_element_type=F32)
            g_ssm = jnp.dot(hb, wgs_ref[:, cols], preferred_element_type=F32) + bgate_ref[:, cols]
            g_att = jnp.dot(hb, wga_ref[:, cols], preferred_element_type=F32) + bgate_ref[:, gate_cols]
            mixed = _sigmoid(g_ssm) * (val * _sigmoid(gate)) + _sigmoid(g_att) * b_att
            acc_ref[rows, cols] = mixed.astype(BF16)
            yield

        r = jnp.dot(acc_ref[rows, :], wmix_ref[...], preferred_element_type=F32) + bmix_ref[...]
        yield
        out_ref[rows, :] = _layer_norm(DEEPNORM_ALPHA * h_ref[rows, :] + r, g_ref[...], b_ref[...])

    _stagger([tile(i, rows) for i, rows in _sub_tiles(h_ref.shape[0])], lag=STAGE_LAG)


def _mix(yg, outs, lses, h, wgs, wga, bgate, wglu, bglu, wup, wmix, bmix, g, b, seq):
    t = h.shape[0]
    tm = 2 * TOKEN_TILE
    per_b = seq // tm
    row = lambda n: pl.BlockSpec((tm, n), lambda i: (i, 0))
    full = lambda a: pl.BlockSpec(a.shape, lambda i: (0,) * a.ndim, pipeline_mode=pl.Buffered(1))
    grp = lambda d: pl.BlockSpec((1, d, tm // d, ATT_MERGED), lambda i: (i // per_b, 0, i % per_b, 0))
    consts = (wgs, wga, bgate, wglu, bglu, wup, wmix, bmix, g, b)
    return pl.pallas_call(
        _mix_kernel,
        grid=(t // tm,),
        in_specs=([row(SSM_WIDTH)] + [grp(d) for d in DILATIONS] * 2 + [row(D_MODEL)]
                  + [full(a) for a in consts]),
        out_specs=row(D_MODEL),
        out_shape=jax.ShapeDtypeStruct((t, D_MODEL), F32),
        scratch_shapes=[pltpu.VMEM((tm // SUB_ROWS, ATT_MERGED // LANES, SUB_ROWS, LANES), F32),
                        pltpu.VMEM((tm // SUB_ROWS, ATT_MERGED // LANES, RELAYOUT_STRIDE,
                                    SUB_ROWS // RELAYOUT_STRIDE, LANES), F32),
                        pltpu.VMEM((tm, D_MODEL), BF16)],
        compiler_params=_cparams(1),
        name="mix",
    )(yg, *outs, *lses, h, *consts)


def _tail_kernel(h_ref, mem_ref, wkv_ref, wq_ref, wo_ref, g2_ref, b2_ref,
                 w1_ref, bf1_ref, w2_ref, bf2_ref, g3_ref, b3_ref, out_ref, kv_ref, acc_ref, *, steps_per_batch):
    dn = (((1,), (1,)), ((), ()))

    @pl.when(pl.program_id(0) % steps_per_batch == 0)
    def _():
        kv_ref[...] = jnp.dot(mem_ref[0].astype(BF16), wkv_ref[...], preferred_element_type=F32).astype(BF16)

    def tile(sub, rows):
        h = h_ref[rows, :]
        hb = h.astype(BF16)
        yield
        q = (jnp.dot(hb, wq_ref[...], preferred_element_type=F32) * XATT_SCALE).astype(BF16)
        yield
        outs = []
        for hh in range(XATT_HEADS):
            sl = slice(hh * XATT_HEAD_DIM, (hh + 1) * XATT_HEAD_DIM)
            sv = slice(D_MODEL + hh * XATT_HEAD_DIM, D_MODEL + (hh + 1) * XATT_HEAD_DIM)
            s = lax.dot_general(q[:, sl], kv_ref[:, sl], dn, preferred_element_type=F32)
            e = jnp.exp(s - jnp.max(s, axis=-1, keepdims=True))
            p = e / jnp.sum(e, axis=-1, keepdims=True)
            outs.append(jnp.dot(p.astype(BF16), kv_ref[:, sv], preferred_element_type=F32).astype(BF16))
            yield
        xo = jnp.dot(jnp.concatenate(outs, axis=-1), wo_ref[...], preferred_element_type=F32)
        yield
        h = _layer_norm(DEEPNORM_ALPHA * h + xo, g2_ref[...], b2_ref[...])
        hb = h.astype(BF16)
        yield
        for c in range(D_FF // FFN_CHUNK):
            cols = slice(c * FFN_CHUNK, (c + 1) * FFN_CHUNK)
            a = jnp.dot(hb, w1_ref[:, cols], preferred_element_type=F32) + bf1_ref[:, cols]
            a = jnp.square(jnp.maximum(a, 0.0)).astype(BF16)
            part = jnp.dot(a, w2_ref[cols, :], preferred_element_type=F32)
            if c == 0:
                acc_ref[rows, :] = part
            else:
                acc_ref[rows, :] += part
            yield
        ff = acc_ref[rows, :] + bf2_ref[...]
        out_ref[rows, :] = _layer_norm(DEEPNORM_ALPHA * h + ff, g3_ref[...], b3_ref[...])

    _stagger([tile(i, rows) for i, rows in _sub_tiles(h_ref.shape[0])], lag=STAGE_LAG)


def _tail(h, mem, wkv, wq, wo, g2, b2, w1, bf1, w2, bf2, g3, b3, seq):
    t = h.shape[0]
    tm = 2 * TOKEN_TILE
    per_b = seq // tm
    row = pl.BlockSpec((tm, D_MODEL), lambda i: (i, 0))
    full = lambda a: pl.BlockSpec(a.shape, lambda i: (0,) * a.ndim, pipeline_mode=pl.Buffered(1))
    consts = (wkv, wq, wo, g2, b2, w1, bf1, w2, bf2, g3, b3)
    return pl.pallas_call(
        functools.partial(_tail_kernel, steps_per_batch=per_b),
        grid=(t // tm,),
        in_specs=[row, pl.BlockSpec((1,) + mem.shape[1:], lambda i: (i // per_b, 0, 0))] + [full(a) for a in consts],
        out_specs=row,
        out_shape=jax.ShapeDtypeStruct((t, D_MODEL), F32),
        scratch_shapes=[pltpu.VMEM((mem.shape[1], 2 * D_MODEL), BF16), pltpu.VMEM((tm, D_MODEL), F32)],
        compiler_params=pltpu.CompilerParams(dimension_semantics=("arbitrary",), vmem_limit_bytes=VMEM_LIMIT),
        name="tail",
    )(h, mem, *consts)


def _rope_selectors():
    half = ROT_DIM // 2
    sel = np.zeros((ROT_DIM, 3 * LANES), np.float32)
    one = np.zeros((1, LANES), np.float32)
    for lane in range(LANES):
        d = lane % ATT_HEAD_DIM
        if d >= ROT_DIM:
            one[0, lane] = 1.0
            continue
        sel[d % half, lane] = 1.0
        if d < half:
            sel[half + d, LANES + lane] = -1.0
        else:
            sel[d, 2 * LANES + lane] = 1.0
    return jnp.asarray(sel, BF16), jnp.asarray(one)


def _rope_cos_sin(positions):
    inv_freq = ROPE_THETA ** (-jnp.arange(0, ROT_DIM, 2, dtype=F32) / ROT_DIM)
    ang = positions.astype(F32).reshape(-1, 1) * inv_freq
    return jnp.concatenate([jnp.cos(ang), jnp.sin(ang)], axis=-1)


def kernel(x, mem, positions, ln_in_g, ln_in_b, w_in, b_in, ssm_log_dt, ssm_a_re, ssm_a_im, ssm_b_re, ssm_b_im, ssm_c_re, ssm_c_im, ssm_d, w_glu, b_glu, w_att_up, w_mix_out, b_mix_out, ln1_g, ln1_b, w_xq, w_xkv, w_xo, ln2_g, ln2_b, w_ff1, b_ff1, w_ff2, b_ff2, ln3_g, ln3_b):
    bsz, seq, _ = x.shape
    t = bsz * seq
    row2 = lambda a: a.reshape(1, -1).astype(F32)
    cs = _rope_cos_sin(positions)
    h = x.reshape(t, D_MODEL)
    for l in range(DEPTH):
        h, uflat, *qkv = _ln_proj(
            h, row2(ln_in_g), row2(ln_in_b), w_in[l, :, :GATE_OFF].astype(BF16), row2(b_in[l, :GATE_OFF]),
            cs, bsz, seq)

        ssm_params = _ssm_matrices(ssm_log_dt[l], ssm_a_re[l], ssm_a_im[l], ssm_b_re[l],
                                   ssm_b_im[l], ssm_c_re[l], ssm_c_im[l], ssm_d[l])
        yg = _ssm(uflat, ssm_params, bsz)

        gate_cb = GATE_OFF // D_MODEL
        casts = ([(w_ff1[l], None)],
                 [(w_ff2[l], None)],
                 [(w_in[l], (gate_cb, D_MODEL)), (w_in[l], (gate_cb + 1, D_MODEL)), (w_glu[l], None),
                  (w_att_up[l], None), (w_mix_out[l], None), (w_xkv[l], None), (w_xq[l], None), (w_xo[l], None)])
        outs, lses, cast = [], [], []
        for gi, dil in enumerate(DILATIONS):
            qg, kg, vg = (a.reshape(bsz * dil, seq // dil, ATT_MERGED) for a in qkv[3 * gi:3 * gi + 3])
            o_g, lse_g, cast_g = _banded_attention(qg, kg, vg, casts[gi])
            outs.append(o_g.reshape(bsz, dil, seq // dil, ATT_MERGED))
            lses.append(lse_g.reshape(bsz, dil, seq // dil, ATT_MERGED))
            cast.extend(cast_g)
        wb_ff1, wb_ff2, wb_gs, wb_ga, wb_glu, wb_up, wb_mix, wb_xkv, wb_xq, wb_xo = cast

        h = _mix(yg, outs, lses, h, wb_gs, wb_ga, row2(b_in[l, GATE_OFF:]), wb_glu, row2(b_glu[l]),
                 wb_up, wb_mix, row2(b_mix_out[l]), row2(ln1_g[l]), row2(ln1_b[l]), seq)

        h = _tail(h, mem, wb_xkv, wb_xq, wb_xo, row2(ln2_g[l]), row2(ln2_b[l]), wb_ff1, row2(b_ff1[l]),
                  wb_ff2, row2(b_ff2[l]), row2(ln3_g[l]), row2(ln3_b[l]), seq)
    return h.reshape(bsz, seq, D_MODEL)
```

```python
import functools
import math

import jax
import jax.numpy as jnp
import numpy as np
from jax import lax
from jax.experimental import pallas as pl
from jax.experimental.pallas import tpu as pltpu

F32 = jnp.float32
BF16 = jnp.bfloat16

D_MODEL = 1024
SSM_GROUP = 16
SSM_WIDTH = 768
SSM_GROUPS = SSM_WIDTH // SSM_GROUP
SSM_STATE = 64
ATT_HEAD_DIM = 64
ATT_HEADS_PER_GROUP = 4
DILATIONS = (1, 4, 16)
ATT_SPAN = 128
ATT_WIDTH = 768
ATT_MERGED = ATT_HEADS_PER_GROUP * ATT_HEAD_DIM
ATT_SCALE = ATT_HEAD_DIM ** -0.5
ROT_DIM = ATT_HEAD_DIM // 4
ROPE_THETA = 500000.0
XATT_HEADS = 4
XATT_HEAD_DIM = D_MODEL // XATT_HEADS
XATT_SCALE = XATT_HEAD_DIM ** -0.5
D_FF = 4 * D_MODEL
DEPTH = 1
DEEPNORM_ALPHA = (2 * DEPTH) ** 0.25
LN_EPS = 1e-5
NEG_INF = -1e30
LOG2E = math.log2(math.e)
LN2 = math.log(2.0)

LANES = 128
CHUNK = 16
SG_GROUPS = LANES // SSM_GROUP
N_SG = SSM_GROUPS // SG_GROUPS
FLAT = CHUNK * LANES
SG_STATE = SG_GROUPS * SSM_STATE
MXU_TILE = 256
N_TT = FLAT // MXU_TILE

SSM_Y_ROWS = 512
SUB_ROWS = 256
TOKEN_TILE = 512
RELAYOUT_STRIDE = 4
STAGE_LAG = 1
MIX_CHUNK = 256
FFN_CHUNK = 1024
QKV_OFF = SSM_WIDTH
GATE_OFF = SSM_WIDTH + 3 * ATT_WIDTH
VMEM_LIMIT = 56 * 1024 * 1024


def _cparams(n_axes):
    return pltpu.CompilerParams(dimension_semantics=("parallel",) * n_axes,
                                vmem_limit_bytes=VMEM_LIMIT)


def _layer_norm(x, g, b):
    mu = jnp.mean(x, axis=-1, keepdims=True)
    xc = x - mu
    var = jnp.mean(xc * xc, axis=-1, keepdims=True)
    return xc * lax.rsqrt(var + LN_EPS) * g + b


def _sigmoid(x):
    return 0.5 * jnp.tanh(0.5 * x) + 0.5


def _stagger(tiles, lag):
    tiles = list(tiles)
    live = [True] * len(tiles)
    rnd = 0
    while any(live):
        for i in reversed(range(len(tiles))):
            if live[i] and rnd >= i * lag:
                try:
                    next(tiles[i])
                except StopIteration:
                    live[i] = False
        rnd += 1


def _sub_tiles(tm):
    return [(i, slice(i * SUB_ROWS, (i + 1) * SUB_ROWS)) for i in range(tm // SUB_ROWS)]


def _ln_proj_kernel(x_ref, g_ref, b_ref, w_ref, bi_ref, cs_ref, sel_ref, one_ref,
                    h_ref, u_ref, q0_ref, k0_ref, v0_ref, q1_ref, k1_ref, v1_ref, q2_ref, k2_ref, v2_ref,
                    scr_ref, tmp_ref):
    half = ROT_DIM // 2
    qkv_refs = ((q0_ref, k0_ref, v0_ref), (q1_ref, k1_ref, v1_ref), (q2_ref, k2_ref, v2_ref))
    sections = [(0, SSM_WIDTH), (QKV_OFF, ATT_WIDTH), (QKV_OFF + ATT_WIDTH, ATT_WIDTH),
                (QKV_OFF + 2 * ATT_WIDTH, ATT_WIDTH)]

    def tile(sub, rows):
        h = _layer_norm(x_ref[rows, :], g_ref[...], b_ref[...])
        h_ref[rows, :] = h
        hb = h.astype(BF16)
        cs = cs_ref[rows, :]
        cs_hi = cs.astype(BF16)
        cs_lo = (cs - cs_hi.astype(F32)).astype(BF16)
        tab = (jnp.dot(cs_hi, sel_ref[...], preferred_element_type=F32)
               + jnp.dot(cs_lo, sel_ref[...], preferred_element_type=F32))
        cc = tab[:, 0:LANES] + one_ref[...]
        s1 = tab[:, LANES:2 * LANES]
        s2 = tab[:, 2 * LANES:3 * LANES]
        yield

        def stage(off, val):
            for i in range(val.shape[1] // LANES):
                scr_ref[sub, off // LANES + i] = val[:, i * LANES:(i + 1) * LANES]

        def rope(val, scale):
            tc, t1, t2 = (cc, s1, s2) if scale == 1.0 else (cc * scale, s1 * scale, s2 * scale)
            out = []
            for i in range(ATT_WIDTH // LANES):
                t = val[:, i * LANES:(i + 1) * LANES]
                out.append(t * tc + pltpu.roll(t, LANES - half, 1) * t1 + pltpu.roll(t, half, 1) * t2)
            return jnp.concatenate(out, axis=-1)

        def rows_mod16(cb, slot, emit):
            for m in range(RELAYOUT_STRIDE):
                tmp_ref[sub, slot, m] = scr_ref[sub, cb, pl.ds(m, SUB_ROWS // RELAYOUT_STRIDE,
                                                               stride=RELAYOUT_STRIDE), :]
            for j in range(CHUNK):
                q, m = divmod(j, RELAYOUT_STRIDE)
                emit(j, tmp_ref[sub, slot, m, pl.ds(q, SUB_ROWS // CHUNK, stride=RELAYOUT_STRIDE), :])

        def emit_groups(a):
            for gi, dil in enumerate(DILATIONS):
                ref = qkv_refs[gi][a]
                cb0 = (QKV_OFF + a * ATT_WIDTH + gi * ATT_MERGED) // LANES
                n = SUB_ROWS // dil
                for e in range(ATT_MERGED // LANES):
                    def put(r, blk, ref=ref, n=n, e=e):
                        ref[0, r, sub * n:(sub + 1) * n, e * LANES:(e + 1) * LANES] = blk.astype(BF16)
                    if dil == 1:
                        put(0, scr_ref[sub, cb0 + e])
                    elif dil == RELAYOUT_STRIDE:
                        for r in range(dil):
                            put(r, scr_ref[sub, cb0 + e, pl.ds(r, n, stride=dil), :])
                    else:
                        rows_mod16(cb0 + e, a * (ATT_MERGED // LANES) + e, put)

        for c, (off, width) in enumerate(sections):
            val = (jnp.dot(hb, w_ref[:, off:off + width], preferred_element_type=F32)
                   + bi_ref[:, off:off + width])
            if c == 0:
                stage(off, val)
                n = SUB_ROWS // CHUNK
                for sg in range(N_SG):
                    def put_u(j, blk, sg=sg):
                        u_ref[sg, sub * n:(sub + 1) * n, j * LANES:(j + 1) * LANES] = blk.astype(BF16)
                    rows_mod16(sg, sg, put_u)
            elif c in (1, 2):
                stage(off, rope(val, ATT_SCALE * LOG2E if c == 1 else 1.0))
                emit_groups(c - 1)
            else:
                stage(off, val)
                emit_groups(2)
            yield

    _stagger([tile(i, rows) for i, rows in _sub_tiles(x_ref.shape[0])], lag=STAGE_LAG)


def _ln_proj(x2, g, b, w, bi, cs, bsz, seq):
    t = x2.shape[0]
    tm = 2 * TOKEN_TILE
    per_b = seq // tm
    sel, one = _rope_selectors()
    row = lambda n: pl.BlockSpec((tm, n), lambda i: (i, 0))
    full = lambda a: pl.BlockSpec(a.shape, lambda i: (0,) * a.ndim, pipeline_mode=pl.Buffered(1))
    grp = lambda d: pl.BlockSpec((1, d, tm // d, ATT_MERGED), lambda i: (i // per_b, 0, i % per_b, 0))
    outs = [jax.ShapeDtypeStruct((t, D_MODEL), F32),
            jax.ShapeDtypeStruct((N_SG, t // CHUNK, FLAT), BF16)]
    out_specs = [row(D_MODEL), pl.BlockSpec((N_SG, tm // CHUNK, FLAT), lambda i: (0, i, 0))]
    for d in DILATIONS:
        outs += [jax.ShapeDtypeStruct((bsz, d, seq // d, ATT_MERGED), BF16)] * 3
        out_specs += [grp(d)] * 3
    return pl.pallas_call(
        _ln_proj_kernel,
        grid=(t // tm,),
        in_specs=[row(D_MODEL), full(g), full(b), full(w), full(bi), row(ROT_DIM), full(sel), full(one)],
        out_specs=out_specs,
        out_shape=outs,
        scratch_shapes=[pltpu.VMEM((tm // SUB_ROWS, GATE_OFF // LANES, SUB_ROWS, LANES), F32),
                        pltpu.VMEM((tm // SUB_ROWS, SSM_WIDTH // LANES, RELAYOUT_STRIDE,
                                    SUB_ROWS // RELAYOUT_STRIDE, LANES), F32)],
        compiler_params=_cparams(1),
        name="ln_proj",
    )(x2, g, b, w, bi, cs, sel, one)


def _ssm_matrices(log_dt, a_re, a_im, b_re, b_im, c_re, c_im, d):
    g, n, c = SSM_GROUPS, SSM_STATE, SSM_GROUP
    dt = jnp.exp(log_dt.astype(F32))[:, None]
    a_re = a_re.astype(F32)
    a_im = a_im.astype(F32)
    ks = jnp.arange(CHUNK + 1, dtype=F32)
    mag = jnp.exp((a_re * dt)[..., None] * ks)
    ang = (a_im * dt)[..., None] * ks
    pw_re = mag * jnp.cos(ang)
    pw_im = mag * jnp.sin(ang)
    ab_re, ab_im = pw_re[..., 1], pw_im[..., 1]
    den = jnp.square(a_re) + jnp.square(a_im)
    nr = ab_re - 1.0
    f_re = (nr * a_re + ab_im * a_im) / den
    f_im = (ab_im * a_re - nr * a_im) / den
    b_re = b_re.astype(F32)
    b_im = b_im.astype(F32)
    bb_re = f_re[..., None] * b_re - f_im[..., None] * b_im
    bb_im = f_re[..., None] * b_im + f_im[..., None] * b_re
    pw = jnp.stack([pw_re, pw_im])
    pwrep = jnp.transpose(pw.reshape(2, N_SG, SG_GROUPS, n, CHUNK + 1), (1, 0, 4, 2, 3))
    pwrep = jnp.broadcast_to(pwrep[:, :, :, :, None, :], (N_SG, 2, CHUNK + 1, SG_GROUPS, c, n))
    pwrep = pwrep.reshape(N_SG, 2, CHUNK + 1, LANES, n)
    pwcol = jnp.transpose(pw.reshape(2, N_SG, SG_STATE, CHUNK + 1), (1, 0, 2, 3))
    bbt = jnp.transpose(jnp.stack([bb_re, bb_im]), (0, 1, 3, 2)).reshape(2, N_SG, LANES, n).transpose(1, 0, 2, 3)
    cs = jnp.transpose(jnp.stack([c_re.astype(F32), c_im.astype(F32)]), (0, 1, 3, 2))
    cs = cs.reshape(2, N_SG, SG_STATE, c).transpose(1, 0, 2, 3)
    dmat = (d.astype(F32).reshape(g, c, 1) * jnp.eye(c, dtype=F32)[None]).reshape(N_SG, LANES, c)
    ar = pw_re[..., CHUNK].reshape(N_SG, 1, SG_STATE)
    ai = pw_im[..., CHUNK].reshape(N_SG, 1, SG_STATE)
    return pwrep, pwcol, bbt, cs, dmat, ar, ai


def _ssm_kernel(u_ref, pwrep_ref, pwcol_ref, bbt_ref, cs_ref, dmat_ref, ar_ref, ai_ref, y_ref,
                tt_ref, wp_ref, wc_ref, p_ref, scr_ref, toep_ref, tmp_ref, *, steps, nseq):
    def iota(shape, dim):
        return lax.broadcasted_iota(jnp.int32, shape, dim)

    def spread(nrows):
        r, cidx = iota((nrows, nrows * SG_GROUPS), 0), iota((nrows, nrows * SG_GROUPS), 1)
        return jnp.where(((r >> 4) == (cidx >> 7)) & ((r & 15) == (cidx & 15)), 1.0, 0.0).astype(BF16)

    grp16 = lambda idx: (idx >> 4) & (SG_GROUPS - 1)
    grp64 = lambda idx: (idx >> 6) & (SG_GROUPS - 1)


    r, cidx = iota((SSM_STATE, SG_STATE), 0), iota((SSM_STATE, SG_STATE), 1)
    e_p = jnp.where(r == (cidx & (SSM_STATE - 1)), 1.0, 0.0).astype(BF16)
    keep = grp16(iota((LANES, SG_STATE), 0)) == grp64(iota((LANES, SG_STATE), 1))
    b_r, b_i = bbt_ref[0, 0], bbt_ref[0, 1]
    for s in range(CHUNK):
        a_r, a_i = pwrep_ref[0, 0, CHUNK - 1 - s], pwrep_ref[0, 1, CHUNK - 1 - s]
        for part, v in enumerate((a_r * b_r - a_i * b_i, a_r * b_i + a_i * b_r)):
            full = jnp.dot(v.astype(BF16), e_p, preferred_element_type=F32)
            wp_ref[s * LANES:(s + 1) * LANES, part * SG_STATE:(part + 1) * SG_STATE] = (
                jnp.where(keep, full, 0.0).astype(BF16))

    c_r, c_i = cs_ref[0, 0], cs_ref[0, 1]
    cstack = jnp.concatenate([c_r, -c_i], axis=0).astype(BF16)
    e_t = spread(SSM_GROUP)
    keep = grp16(iota((LANES, LANES), 0)) == grp16(iota((LANES, LANES), 1))
    lag_blocks = []
    for k in range(CHUNK):
        slab = wp_ref[(CHUNK - 1 - k) * LANES:(CHUNK - k) * LANES, :]
        kk = jnp.dot(slab, cstack, preferred_element_type=F32)
        if k == 0:
            kk = kk + dmat_ref[0]
        lag_blocks.append(jnp.where(keep, jnp.dot(kk.astype(BF16), e_t, preferred_element_type=F32), 0.0)
                          .astype(BF16))
    for dlt in range(N_TT):
        for s in range(2):
            for t in range(2):
                k = 2 * dlt + t - s
                blk = lag_blocks[k] if k >= 0 else jnp.zeros((LANES, LANES), BF16)
                tt_ref[dlt, s * LANES:(s + 1) * LANES, t * LANES:(t + 1) * LANES] = blk

    keep = grp64(iota((SG_STATE, LANES), 0)) == grp16(iota((SG_STATE, LANES), 1))
    for t in range(CHUNK):
        a_r = pwcol_ref[0, 0, :, t + 1:t + 2]
        a_i = pwcol_ref[0, 1, :, t + 1:t + 2]
        for part, v in enumerate((c_r * a_r - c_i * a_i, -(c_r * a_i + c_i * a_r))):
            full = jnp.dot(v.astype(BF16), e_t, preferred_element_type=F32)
            wc_ref[part * SG_STATE:(part + 1) * SG_STATE, t * LANES:(t + 1) * LANES] = (
                jnp.where(keep, full, 0.0).astype(BF16))

    p_ref[...] = jnp.dot(u_ref[0], wp_ref[...], preferred_element_type=F32)

    ar = ar_ref[0]
    ai = ai_ref[0]

    def scan_step(c, carry):
        new = []
        for b in range(nseq):
            hr, hi = carry[b]
            row = b * steps + c
            pr = p_ref[row:row + 1, 0:SG_STATE]
            pi = p_ref[row:row + 1, SG_STATE:2 * SG_STATE]
            p_ref[row:row + 1, 0:SG_STATE] = hr
            p_ref[row:row + 1, SG_STATE:2 * SG_STATE] = hi
            new.append((ar * hr - ai * hi + pr, ar * hi + ai * hr + pi))
        return new

    tn = scr_ref.shape[0] // CHUNK
    units = [(r0, jo) for r0 in range(0, steps * nseq, tn) for jo in range(N_TT)]
    steps_per_unit = steps // len(units)
    z = jnp.zeros((1, SG_STATE), F32)
    carry = [(z, z)] * nseq
    for ui, (r0, jo) in enumerate(units):
        rows = slice(r0, r0 + tn)
        acc = jnp.dot(u_ref[0, rows, 0:MXU_TILE], tt_ref[jo], preferred_element_type=F32)
        for ji in range(1, jo + 1):
            acc += jnp.dot(u_ref[0, rows, ji * MXU_TILE:(ji + 1) * MXU_TILE], tt_ref[jo - ji],
                           preferred_element_type=F32)
        toep_ref[rows, jo * MXU_TILE:(jo + 1) * MXU_TILE] = acc
        for c in range(ui * steps_per_unit, (ui + 1) * steps_per_unit):
            carry = scan_step(c, carry)

    for r0 in range(0, steps * nseq, tn):
        rows = slice(r0, r0 + tn)
        hb = p_ref[rows, :].astype(BF16)
        for jo in range(N_TT):
            cols = slice(jo * MXU_TILE, (jo + 1) * MXU_TILE)
            acc = toep_ref[rows, cols] + jnp.dot(hb, wc_ref[:, cols], preferred_element_type=F32)
            yg = jax.nn.gelu(acc, approximate=True)
            for e in range(MXU_TILE // LANES):
                q, m = divmod(jo * (MXU_TILE // LANES) + e, RELAYOUT_STRIDE)
                tmp_ref[m, pl.ds(q, tn, stride=RELAYOUT_STRIDE), :] = yg[:, e * LANES:(e + 1) * LANES]
        for m in range(RELAYOUT_STRIDE):
            scr_ref[pl.ds(m, tn * CHUNK // RELAYOUT_STRIDE, stride=RELAYOUT_STRIDE), :] = tmp_ref[m]
        y_ref[r0 * CHUNK:(r0 + tn) * CHUNK, :] = scr_ref[...].astype(BF16)


def _ssm(uflat, params, bsz):
    nc = uflat.shape[1]
    blk = lambda a: pl.BlockSpec((1,) + a.shape[1:], lambda s: (s,) + (0,) * (a.ndim - 1))
    return pl.pallas_call(
        functools.partial(_ssm_kernel, steps=nc // bsz, nseq=bsz),
        grid=(N_SG,),
        in_specs=[blk(uflat)] + [blk(a) for a in params],
        out_specs=pl.BlockSpec((nc * CHUNK, LANES), lambda s: (0, s)),
        out_shape=jax.ShapeDtypeStruct((nc * CHUNK, SSM_WIDTH), BF16),
        scratch_shapes=[pltpu.VMEM((N_TT, MXU_TILE, MXU_TILE), BF16),
                        pltpu.VMEM((FLAT, 2 * SG_STATE), BF16),
                        pltpu.VMEM((2 * SG_STATE, FLAT), BF16),
                        pltpu.VMEM((nc, 2 * SG_STATE), F32),
                        pltpu.VMEM((SSM_Y_ROWS * CHUNK, LANES), F32),
                        pltpu.VMEM((nc, FLAT), F32),
                        pltpu.VMEM((RELAYOUT_STRIDE, SSM_Y_ROWS * CHUNK // RELAYOUT_STRIDE, LANES), F32)],
        compiler_params=_cparams(1),
        name="ssm",
    )(uflat, *params)


ATT_UNITS_IN_FLIGHT = 4
ATT_STEP_QUERIES = 2048


def _attn_kernel(*refs, n_cast):
    q_ref, kp_ref, kc_ref, vp_ref, vc_ref = refs[:5]
    cast_in = refs[5:5 + n_cast]
    o_ref, lse_ref = refs[5 + n_cast:7 + n_cast]
    cast_out = refs[7 + n_cast:7 + 2 * n_cast]
    ks_ref, vs_ref = refs[7 + 2 * n_cast:]
    for src, dst in zip(cast_in, cast_out):
        dst[...] = src[...].astype(BF16)

    nsb, qb = q_ref.shape[0], q_ref.shape[1]
    blk = ATT_SPAN
    first = pl.program_id(1) == 0

    lane = lax.broadcasted_iota(jnp.int32, (1, ATT_MERGED), 1)
    head_sel = (lane % LANES < ATT_HEAD_DIM, lane % LANES >= ATT_HEAD_DIM)
    for sq in range(nsb):
        for src_p, src_c, dst in ((kp_ref, kc_ref, ks_ref), (vp_ref, vc_ref, vs_ref)):
            for e in range(2):
                xp, xc = src_p[sq], src_c[sq]
                dst[sq, e, 0:blk] = jnp.where(head_sel[e], xp, jnp.zeros_like(xp))
                dst[sq, e, blk:blk + qb] = jnp.where(head_sel[e], xc, jnp.zeros_like(xc))

    qi = lax.broadcasted_iota(jnp.int32, (blk, 2 * blk), 0)
    kk = lax.broadcasted_iota(jnp.int32, (blk, 2 * blk), 1)
    band = jnp.logical_and(kk >= qi, kk <= qi + blk)
    band_first = jnp.logical_and(band, jnp.logical_or(kk >= blk, jnp.logical_not(first)))
    low = lax.broadcasted_iota(jnp.int32, (blk, LANES), 1) < ATT_HEAD_DIM
    dn = (((1,), (1,)), ((), ()))

    units = [(sq, j, pair) for sq in range(nsb) for j in range(qb // blk) for pair in range(ATT_MERGED // LANES)]
    for g0 in range(0, len(units), ATT_UNITS_IN_FLIGHT):
        group = units[g0:g0 + ATT_UNITS_IN_FLIGHT]
        scores = []
        for sq, j, pair in group:
            cols = slice(pair * LANES, (pair + 1) * LANES)
            keys = slice(j * blk, (j + 2) * blk)
            kcat = jnp.concatenate([ks_ref[sq, 0, keys, cols], ks_ref[sq, 1, keys, cols]], axis=0)
            s = lax.dot_general(q_ref[sq, j * blk:(j + 1) * blk, cols], kcat, dn,
                                preferred_element_type=F32)
            mask = band_first if j == 0 else band
            scores.append([jnp.where(mask, s[:, e * 2 * blk:(e + 1) * 2 * blk], NEG_INF) for e in range(2)])
        maxes = [[jnp.max(jnp.maximum(se[:, :blk], se[:, blk:]), axis=-1, keepdims=True) for se in su]
                 for su in scores]
        probs = [[jnp.exp2(se - me) for se, me in zip(su, mu)] for su, mu in zip(scores, maxes)]
        dens = [[jnp.sum(pe[:, :blk] + pe[:, blk:], axis=-1, keepdims=True) for pe in pu] for pu in probs]
        for (sq, j, pair), pu, mu, du in zip(group, probs, maxes, dens):
            cols = slice(pair * LANES, (pair + 1) * LANES)
            rows = slice(j * blk, (j + 1) * blk)
            keys = slice(j * blk, (j + 2) * blk)
            vcat = jnp.concatenate([vs_ref[sq, 0, keys, cols], vs_ref[sq, 1, keys, cols]], axis=0)
            pcat = jnp.concatenate([pu[0].astype(BF16), pu[1].astype(BF16)], axis=1)
            num = jnp.dot(pcat, vcat, preferred_element_type=F32)
            o_ref[sq, rows, cols] = (num * jnp.where(low, 1.0 / du[0], 1.0 / du[1])).astype(BF16)
            lse_ref[sq, rows, cols] = jnp.where(low, mu[0] * LN2 + jnp.log(du[0]), mu[1] * LN2 + jnp.log(du[1]))


def _banded_attention(q, k, v, casts=()):
    nseq, length, w = q.shape
    blk = ATT_SPAN
    qb = min(length, ATT_STEP_QUERIES)
    nsb = ATT_STEP_QUERIES // qb
    per = qb // blk
    grid = (nseq // nsb, length // qb)
    n_steps = grid[0] * grid[1]
    cur = pl.BlockSpec((nsb, qb, w), lambda s, n: (s, n, 0))
    prev = pl.BlockSpec((nsb, blk, w), lambda s, n: (s, jnp.maximum(n * per - 1, 0), 0))
    cast_in_specs, cast_out_specs, cast_shapes = [], [], []
    for arr, window in casts:
        rows = arr.shape[0] // n_steps
        col_blk, width = (0, arr.shape[1]) if window is None else window
        cast_in_specs.append(pl.BlockSpec((rows, width), lambda s, n, cb=col_blk: (s * grid[1] + n, cb)))
        cast_out_specs.append(pl.BlockSpec((rows, width), lambda s, n: (s * grid[1] + n, 0)))
        cast_shapes.append(jax.ShapeDtypeStruct((arr.shape[0], width), BF16))
    outs = pl.pallas_call(
        functools.partial(_attn_kernel, n_cast=len(casts)),
        grid=grid,
        in_specs=[cur, prev, cur, prev, cur] + cast_in_specs,
        out_specs=[cur, cur] + cast_out_specs,
        out_shape=[jax.ShapeDtypeStruct(q.shape, BF16), jax.ShapeDtypeStruct(q.shape, F32)] + cast_shapes,
        scratch_shapes=[pltpu.VMEM((nsb, 2, blk + qb, w), BF16)] * 2,
        compiler_params=_cparams(2),
        name="attn",
    )(q, k, k, v, v, *[arr for arr, _ in casts])
    return outs[0], outs[1], outs[2:]


def _mix_kernel(y_ref, o0_ref, o1_ref, o2_ref, l0_ref, l1_ref, l2_ref, h_ref, wgs_ref, wga_ref, bgate_ref,
                wglu_ref, bglu_ref, wup_ref, wmix_ref, bmix_ref, g_ref, b_ref, out_ref, scr_ref, tmp_ref, acc_ref):
    o_refs = (o0_ref, o1_ref, o2_ref)
    l_refs = (l0_ref, l1_ref, l2_ref)

    def tile(sub, rows):
        def natural(ref, dil):
            n = SUB_ROWS // dil
            if dil == 1:
                return ref[0, 0, rows, :].astype(F32)
            for e in range(ATT_MERGED // LANES):
                piece = lambda r: ref[0, r, sub * n:(sub + 1) * n, e * LANES:(e + 1) * LANES].astype(F32)
                if dil == RELAYOUT_STRIDE:
                    for r in range(dil):
                        scr_ref[sub, e, pl.ds(r, n, stride=dil), :] = piece(r)
                else:
                    for m in range(RELAYOUT_STRIDE):
                        for q in range(dil // RELAYOUT_STRIDE):
                            tmp_ref[sub, e, m, pl.ds(q, n, stride=RELAYOUT_STRIDE), :] = (
                                piece(RELAYOUT_STRIDE * q + m))
                        scr_ref[sub, e, pl.ds(m, SUB_ROWS // RELAYOUT_STRIDE, stride=RELAYOUT_STRIDE), :] = (
                            tmp_ref[sub, e, m])
            return jnp.concatenate([scr_ref[sub, e] for e in range(ATT_MERGED // LANES)], axis=-1)

        ls = [natural(l_refs[gi], dil) for gi, dil in enumerate(DILATIONS)]
        m = jnp.maximum(jnp.maximum(ls[0], ls[1]), ls[2])
        es = [jnp.exp(l - m) for l in ls]
        att = es[0] * natural(o_refs[0], DILATIONS[0])
        for gi in (1, 2):
            att += es[gi] * natural(o_refs[gi], DILATIONS[gi])
        att = (att / (es[0] + es[1] + es[2])).astype(BF16)
        y = y_ref[rows, :]
        hb = h_ref[rows, :].astype(BF16)
        yield

        for c in range(D_MODEL // MIX_CHUNK):
            cols = slice(c * MIX_CHUNK, (c + 1) * MIX_CHUNK)
            gate_cols = slice(D_MODEL + c * MIX_CHUNK, D_MODEL + (c + 1) * MIX_CHUNK)
            val = jnp.dot(y, wglu_ref[:, cols], preferred_element_type=F32) + bglu_ref[:, cols]
            gate = jnp.dot(y, wglu_ref[:, gate_cols], preferred_element_type=F32) + bglu_ref[:, gate_cols]
            b_att = jnp.dot(att, wup_ref[:, cols], preferred_element_type=F32)
            g_ssm = jnp.dot(hb, wgs_ref[:, cols], preferred_element_type=F32) + bgate_ref[:, cols]
            g_att = jnp.dot(hb, wga_ref[:, cols], preferred_element_type=F32) + bgate_ref[:, gate_cols]
            mixed = _sigmoid(g_ssm) * (val * _sigmoid(gate)) + _sigmoid(g_att) * b_att
            acc_ref[rows, cols] = mixed.astype(BF16)
            yield

        r = jnp.dot(acc_ref[rows, :], wmix_ref[...], preferred_element_type=F32) + bmix_ref[...]
        yield
        out_ref[rows, :] = _layer_norm(DEEPNORM_ALPHA * h_ref[rows, :] + r, g_ref[...], b_ref[...])

    _stagger([tile(i, rows) for i, rows in _sub_tiles(h_ref.shape[0])], lag=STAGE_LAG)


def _mix(yg, outs, lses, h, wgs, wga, bgate, wglu, bglu, wup, wmix, bmix, g, b, seq):
    t = h.shape[0]
    tm = 2 * TOKEN_TILE
    per_b = seq // tm
    row = lambda n: pl.BlockSpec((tm, n), lambda i: (i, 0))
    full = lambda a: pl.BlockSpec(a.shape, lambda i: (0,) * a.ndim, pipeline_mode=pl.Buffered(1))
    grp = lambda d: pl.BlockSpec((1, d, tm // d, ATT_MERGED), lambda i: (i // per_b, 0, i % per_b, 0))
    consts = (wgs, wga, bgate, wglu, bglu, wup, wmix, bmix, g, b)
    return pl.pallas_call(
        _mix_kernel,
        grid=(t // tm,),
        in_specs=([row(SSM_WIDTH)] + [grp(d) for d in DILATIONS] * 2 + [row(D_MODEL)]
                  + [full(a) for a in consts]),
        out_specs=row(D_MODEL),
        out_shape=jax.ShapeDtypeStruct((t, D_MODEL), F32),
        scratch_shapes=[pltpu.VMEM((tm // SUB_ROWS, ATT_MERGED // LANES, SUB_ROWS, LANES), F32),
                        pltpu.VMEM((tm // SUB_ROWS, ATT_MERGED // LANES, RELAYOUT_STRIDE,
                                    SUB_ROWS // RELAYOUT_STRIDE, LANES), F32),
                        pltpu.VMEM((tm, D_MODEL), BF16)],
        compiler_params=_cparams(1),
        name="mix",
    )(yg, *outs, *lses, h, *consts)


def _tail_kernel(h_ref, mem_ref, wkv_ref, wq_ref, wo_ref, g2_ref, b2_ref,
                 w1_ref, bf1_ref, w2_ref, bf2_ref, g3_ref, b3_ref, out_ref, kv_ref, acc_ref, *, steps_per_batch):
    dn = (((1,), (1,)), ((), ()))

    @pl.when(pl.program_id(0) % steps_per_batch == 0)
    def _():
        kv_ref[...] = jnp.dot(mem_ref[0].astype(BF16), wkv_ref[...], preferred_element_type=F32).astype(BF16)

    def tile(sub, rows):
        h = h_ref[rows, :]
        hb = h.astype(BF16)
        yield
        q = (jnp.dot(hb, wq_ref[...], preferred_element_type=F32) * XATT_SCALE).astype(BF16)
        yield
        outs = []
        for hh in range(XATT_HEADS):
            sl = slice(hh * XATT_HEAD_DIM, (hh + 1) * XATT_HEAD_DIM)
            sv = slice(D_MODEL + hh * XATT_HEAD_DIM, D_MODEL + (hh + 1) * XATT_HEAD_DIM)
            s = lax.dot_general(q[:, sl], kv_ref[:, sl], dn, preferred_element_type=F32)
            e = jnp.exp(s - jnp.max(s, axis=-1, keepdims=True))
            p = e / jnp.sum(e, axis=-1, keepdims=True)
            outs.append(jnp.dot(p.astype(BF16), kv_ref[:, sv], preferred_element_type=F32).astype(BF16))
            yield
        xo = jnp.dot(jnp.concatenate(outs, axis=-1), wo_ref[...], preferred_element_type=F32)
        yield
        h = _layer_norm(DEEPNORM_ALPHA * h + xo, g2_ref[...], b2_ref[...])
        hb = h.astype(BF16)
        yield
        for c in range(D_FF // FFN_CHUNK):
            cols = slice(c * FFN_CHUNK, (c + 1) * FFN_CHUNK)
            a = jnp.dot(hb, w1_ref[:, cols], preferred_element_type=F32) + bf1_ref[:, cols]
            a = jnp.square(jnp.maximum(a, 0.0)).astype(BF16)
            part = jnp.dot(a, w2_ref[cols, :], preferred_element_type=F32)
            if c == 0:
                acc_ref[rows, :] = part
            else:
                acc_ref[rows, :] += part
            yield
        ff = acc_ref[rows, :] + bf2_ref[...]
        out_ref[rows, :] = _layer_norm(DEEPNORM_ALPHA * h + ff, g3_ref[...], b3_ref[...])

    _stagger([tile(i, rows) for i, rows in _sub_tiles(h_ref.shape[0])], lag=STAGE_LAG)


def _tail(h, mem, wkv, wq, wo, g2, b2, w1, bf1, w2, bf2, g3, b3, seq):
    t = h.shape[0]
    tm = 2 * TOKEN_TILE
    per_b = seq // tm
    row = pl.BlockSpec((tm, D_MODEL), lambda i: (i, 0))
    full = lambda a: pl.BlockSpec(a.shape, lambda i: (0,) * a.ndim, pipeline_mode=pl.Buffered(1))
    consts = (wkv, wq, wo, g2, b2, w1, bf1, w2, bf2, g3, b3)
    return pl.pallas_call(
        functools.partial(_tail_kernel, steps_per_batch=per_b),
        grid=(t // tm,),
        in_specs=[row, pl.BlockSpec((1,) + mem.shape[1:], lambda i: (i // per_b, 0, 0))] + [full(a) for a in consts],
        out_specs=row,
        out_shape=jax.ShapeDtypeStruct((t, D_MODEL), F32),
        scratch_shapes=[pltpu.VMEM((mem.shape[1], 2 * D_MODEL), BF16), pltpu.VMEM((tm, D_MODEL), F32)],
        compiler_params=pltpu.CompilerParams(dimension_semantics=("arbitrary",), vmem_limit_bytes=VMEM_LIMIT),
        name="tail",
    )(h, mem, *consts)


def _rope_selectors():
    half = ROT_DIM // 2
    sel = np.zeros((ROT_DIM, 3 * LANES), np.float32)
    one = np.zeros((1, LANES), np.float32)
    for lane in range(LANES):
        d = lane % ATT_HEAD_DIM
        if d >= ROT_DIM:
            one[0, lane] = 1.0
            continue
        sel[d % half, lane] = 1.0
        if d < half:
            sel[half + d, LANES + lane] = -1.0
        else:
            sel[d, 2 * LANES + lane] = 1.0
    return jnp.asarray(sel, BF16), jnp.asarray(one)


def _rope_cos_sin(positions):
    inv_freq = ROPE_THETA ** (-jnp.arange(0, ROT_DIM, 2, dtype=F32) / ROT_DIM)
    ang = positions.astype(F32).reshape(-1, 1) * inv_freq
    return jnp.concatenate([jnp.cos(ang), jnp.sin(ang)], axis=-1)


def kernel(x, mem, positions, ln_in_g, ln_in_b, w_in, b_in, ssm_log_dt, ssm_a_re, ssm_a_im, ssm_b_re, ssm_b_im, ssm_c_re, ssm_c_im, ssm_d, w_glu, b_glu, w_att_up, w_mix_out, b_mix_out, ln1_g, ln1_b, w_xq, w_xkv, w_xo, ln2_g, ln2_b, w_ff1, b_ff1, w_ff2, b_ff2, ln3_g, ln3_b):
    bsz, seq, _ = x.shape
    t = bsz * seq
    row2 = lambda a: a.reshape(1, -1).astype(F32)
    cs = _rope_cos_sin(positions)
    h = x.reshape(t, D_MODEL)
    for l in range(DEPTH):
        h, uflat, *qkv = _ln_proj(
            h, row2(ln_in_g), row2(ln_in_b), w_in[l, :, :GATE_OFF].astype(BF16), row2(b_in[l, :GATE_OFF]),
            cs, bsz, seq)

        ssm_params = _ssm_matrices(ssm_log_dt[l], ssm_a_re[l], ssm_a_im[l], ssm_b_re[l],
                                   ssm_b_im[l], ssm_c_re[l], ssm_c_im[l], ssm_d[l])
        yg = _ssm(uflat, ssm_params, bsz)

        gate_cb = GATE_OFF // D_MODEL
        casts = ([(w_ff1[l], None)],
                 [(w_ff2[l], None)],
                 [(w_in[l], (gate_cb, D_MODEL)), (w_in[l], (gate_cb + 1, D_MODEL)), (w_glu[l], None),
                  (w_att_up[l], None), (w_mix_out[l], None), (w_xkv[l], None), (w_xq[l], None), (w_xo[l], None)])
        outs, lses, cast = [], [], []
        for gi, dil in enumerate(DILATIONS):
            qg, kg, vg = (a.reshape(bsz * dil, seq // dil, ATT_MERGED) for a in qkv[3 * gi:3 * gi + 3])
            o_g, lse_g, cast_g = _banded_attention(qg, kg, vg, casts[gi])
            outs.append(o_g.reshape(bsz, dil, seq // dil, ATT_MERGED))
            lses.append(lse_g.reshape(bsz, dil, seq // dil, ATT_MERGED))
            cast.extend(cast_g)
        wb_ff1, wb_ff2, wb_gs, wb_ga, wb_glu, wb_up, wb_mix, wb_xkv, wb_xq, wb_xo = cast

        h = _mix(yg, outs, lses, h, wb_gs, wb_ga, row2(b_in[l, GATE_OFF:]), wb_glu, row2(b_glu[l]),
                 wb_up, wb_mix, row2(b_mix_out[l]), row2(ln1_g[l]), row2(ln1_b[l]), seq)

        h = _tail(h, mem, wb_xkv, wb_xq, wb_xo, row2(ln2_g[l]), row2(ln2_b[l]), wb_ff1, row2(b_ff1[l]),
                  wb_ff2, row2(b_ff2[l]), row2(ln3_g[l]), row2(ln3_b[l]), seq)
    return h.reshape(bsz, seq, D_MODEL)
```

```python
import functools
import math

import jax
import jax.numpy as jnp
import numpy as np
from jax import lax
from jax.experimental import pallas as pl
from jax.experimental.pallas import tpu as pltpu

F32 = jnp.float32
BF16 = jnp.bfloat16

D_MODEL = 1024
SSM_GROUP = 16
SSM_WIDTH = 768
SSM_GROUPS = SSM_WIDTH // SSM_GROUP
SSM_STATE = 64
ATT_HEAD_DIM = 64
ATT_HEADS_PER_GROUP = 4
DILATIONS = (1, 4, 16)
ATT_SPAN = 128
ATT_WIDTH = 768
ATT_MERGED = ATT_HEADS_PER_GROUP * ATT_HEAD_DIM
ATT_SCALE = ATT_HEAD_DIM ** -0.5
ROT_DIM = ATT_HEAD_DIM // 4
ROPE_THETA = 500000.0
XATT_HEADS = 4
XATT_HEAD_DIM = D_MODEL // XATT_HEADS
XATT_SCALE = XATT_HEAD_DIM ** -0.5
D_FF = 4 * D_MODEL
DEPTH = 1
DEEPNORM_ALPHA = (2 * DEPTH) ** 0.25
LN_EPS = 1e-5
NEG_INF = -1e30
LOG2E = math.log2(math.e)
LN2 = math.log(2.0)

LANES = 128
CHUNK = 16
SG_GROUPS = LANES // SSM_GROUP
N_SG = SSM_GROUPS // SG_GROUPS
FLAT = CHUNK * LANES
SG_STATE = SG_GROUPS * SSM_STATE
MXU_TILE = 256
N_TT = FLAT // MXU_TILE

SSM_Y_ROWS = 512
SUB_ROWS = 256
TOKEN_TILE = 512
RELAYOUT_STRIDE = 4
STAGE_LAG = 1
MIX_CHUNK = 256
FFN_CHUNK = 1024
QKV_OFF = SSM_WIDTH
GATE_OFF = SSM_WIDTH + 3 * ATT_WIDTH
VMEM_LIMIT = 56 * 1024 * 1024


def _cparams(n_axes, fuse_inputs=None):
    return pltpu.CompilerParams(dimension_semantics=("parallel",) * n_axes,
                                vmem_limit_bytes=VMEM_LIMIT, allow_input_fusion=fuse_inputs)


def _layer_norm(x, g, b):
    mu = jnp.mean(x, axis=-1, keepdims=True)
    xc = x - mu
    var = jnp.mean(xc * xc, axis=-1, keepdims=True)
    return xc * lax.rsqrt(var + LN_EPS) * g + b


def _sigmoid(x):
    return 0.5 * jnp.tanh(0.5 * x) + 0.5


def _stagger(tiles, lag):
    tiles = list(tiles)
    live = [True] * len(tiles)
    rnd = 0
    while any(live):
        for i in reversed(range(len(tiles))):
            if live[i] and rnd >= i * lag:
                try:
                    next(tiles[i])
                except StopIteration:
                    live[i] = False
        rnd += 1


def _sub_tiles(tm):
    return [(i, slice(i * SUB_ROWS, (i + 1) * SUB_ROWS)) for i in range(tm // SUB_ROWS)]


def _ln_proj_kernel(x_ref, g_ref, b_ref, w_ref, bi_ref, cs_ref, sel_ref, one_ref,
                    h_ref, u_ref, q0_ref, k0_ref, v0_ref, q1_ref, k1_ref, v1_ref, q2_ref, k2_ref, v2_ref,
                    scr_ref, tmp_ref):
    half = ROT_DIM // 2
    qkv_refs = ((q0_ref, k0_ref, v0_ref), (q1_ref, k1_ref, v1_ref), (q2_ref, k2_ref, v2_ref))
    sections = [(0, SSM_WIDTH), (QKV_OFF, ATT_WIDTH), (QKV_OFF + ATT_WIDTH, ATT_WIDTH),
                (QKV_OFF + 2 * ATT_WIDTH, ATT_WIDTH)]

    def tile(sub, rows):
        h = _layer_norm(x_ref[rows, :], g_ref[...], b_ref[...])
        h_ref[rows, :] = h
        hb = h.astype(BF16)
        cs = cs_ref[rows, :]
        cs_hi = cs.astype(BF16)
        cs_lo = (cs - cs_hi.astype(F32)).astype(BF16)
        tab = (jnp.dot(cs_hi, sel_ref[...], preferred_element_type=F32)
               + jnp.dot(cs_lo, sel_ref[...], preferred_element_type=F32))
        cc = tab[:, 0:LANES] + one_ref[...]
        s1 = tab[:, LANES:2 * LANES]
        s2 = tab[:, 2 * LANES:3 * LANES]
        yield

        def stage(off, val):
            for i in range(val.shape[1] // LANES):
                scr_ref[sub, off // LANES + i] = val[:, i * LANES:(i + 1) * LANES]

        def rope(val, scale):
            tc, t1, t2 = (cc, s1, s2) if scale == 1.0 else (cc * scale, s1 * scale, s2 * scale)
            out = []
            for i in range(ATT_WIDTH // LANES):
                t = val[:, i * LANES:(i + 1) * LANES]
                out.append(t * tc + pltpu.roll(t, LANES - half, 1) * t1 + pltpu.roll(t, half, 1) * t2)
            return jnp.concatenate(out, axis=-1)

        def rows_mod16(cb, slot, emit):
            for m in range(RELAYOUT_STRIDE):
                tmp_ref[sub, slot, m] = scr_ref[sub, cb, pl.ds(m, SUB_ROWS // RELAYOUT_STRIDE,
                                                               stride=RELAYOUT_STRIDE), :]
            for j in range(CHUNK):
                q, m = divmod(j, RELAYOUT_STRIDE)
                emit(j, tmp_ref[sub, slot, m, pl.ds(q, SUB_ROWS // CHUNK, stride=RELAYOUT_STRIDE), :])

        def emit_groups(a):
            for gi, dil in enumerate(DILATIONS):
                ref = qkv_refs[gi][a]
                cb0 = (QKV_OFF + a * ATT_WIDTH + gi * ATT_MERGED) // LANES
                n = SUB_ROWS // dil
                for e in range(ATT_MERGED // LANES):
                    def put(r, blk, ref=ref, n=n, e=e):
                        ref[0, r, sub * n:(sub + 1) * n, e * LANES:(e + 1) * LANES] = blk.astype(BF16)
                    if dil == 1:
                        put(0, scr_ref[sub, cb0 + e])
                    elif dil == RELAYOUT_STRIDE:
                        for r in range(dil):
                            put(r, scr_ref[sub, cb0 + e, pl.ds(r, n, stride=dil), :])
                    else:
                        rows_mod16(cb0 + e, a * (ATT_MERGED // LANES) + e, put)

        for c, (off, width) in enumerate(sections):
            val = (jnp.dot(hb, w_ref[:, off:off + width], preferred_element_type=F32)
                   + bi_ref[:, off:off + width])
            if c == 0:
                stage(off, val)
                n = SUB_ROWS // CHUNK
                for sg in range(N_SG):
                    def put_u(j, blk, sg=sg):
                        u_ref[sg, sub * n:(sub + 1) * n, j * LANES:(j + 1) * LANES] = blk.astype(BF16)
                    rows_mod16(sg, sg, put_u)
            elif c in (1, 2):
                stage(off, rope(val, ATT_SCALE * LOG2E if c == 1 else 1.0))
                emit_groups(c - 1)
            else:
                stage(off, val)
                emit_groups(2)
            yield

    _stagger([tile(i, rows) for i, rows in _sub_tiles(x_ref.shape[0])], lag=STAGE_LAG)


def _ln_proj(x2, g, b, w, bi, cs, bsz, seq):
    t = x2.shape[0]
    tm = 2 * TOKEN_TILE
    per_b = seq // tm
    sel, one = _rope_selectors()
    row = lambda n: pl.BlockSpec((tm, n), lambda i: (i, 0))
    full = lambda a: pl.BlockSpec(a.shape, lambda i: (0,) * a.ndim, pipeline_mode=pl.Buffered(1))
    grp = lambda d: pl.BlockSpec((1, d, tm // d, ATT_MERGED), lambda i: (i // per_b, 0, i % per_b, 0))
    outs = [jax.ShapeDtypeStruct((t, D_MODEL), F32),
            jax.ShapeDtypeStruct((N_SG, t // CHUNK, FLAT), BF16)]
    out_specs = [row(D_MODEL), pl.BlockSpec((N_SG, tm // CHUNK, FLAT), lambda i: (0, i, 0))]
    for d in DILATIONS:
        outs += [jax.ShapeDtypeStruct((bsz, d, seq // d, ATT_MERGED), BF16)] * 3
        out_specs += [grp(d)] * 3
    return pl.pallas_call(
        _ln_proj_kernel,
        grid=(t // tm,),
        in_specs=[row(D_MODEL), full(g), full(b), full(w), full(bi), row(ROT_DIM), full(sel), full(one)],
        out_specs=out_specs,
        out_shape=outs,
        scratch_shapes=[pltpu.VMEM((tm // SUB_ROWS, GATE_OFF // LANES, SUB_ROWS, LANES), F32),
                        pltpu.VMEM((tm // SUB_ROWS, SSM_WIDTH // LANES, RELAYOUT_STRIDE,
                                    SUB_ROWS // RELAYOUT_STRIDE, LANES), F32)],
        compiler_params=_cparams(1, fuse_inputs=[False, False, False, True, False, True, False, False]),
        name="ln_proj",
    )(x2, g, b, w, bi, cs, sel, one)


def _ssm_matrices(log_dt, a_re, a_im, b_re, b_im, c_re, c_im, d):
    g, n, c = SSM_GROUPS, SSM_STATE, SSM_GROUP
    dt = jnp.exp(log_dt.astype(F32))[:, None]
    a_re = a_re.astype(F32)
    a_im = a_im.astype(F32)
    ks = jnp.arange(CHUNK + 1, dtype=F32)
    mag = jnp.exp((a_re * dt)[..., None] * ks)
    ang = (a_im * dt)[..., None] * ks
    pw_re = mag * jnp.cos(ang)
    pw_im = mag * jnp.sin(ang)
    ab_re, ab_im = pw_re[..., 1], pw_im[..., 1]
    den = jnp.square(a_re) + jnp.square(a_im)
    nr = ab_re - 1.0
    f_re = (nr * a_re + ab_im * a_im) / den
    f_im = (ab_im * a_re - nr * a_im) / den
    b_re = b_re.astype(F32)
    b_im = b_im.astype(F32)
    bb_re = f_re[..., None] * b_re - f_im[..., None] * b_im
    bb_im = f_re[..., None] * b_im + f_im[..., None] * b_re
    pw = jnp.stack([pw_re, pw_im])
    pwrep = jnp.transpose(pw.reshape(2, N_SG, SG_GROUPS, n, CHUNK + 1), (1, 0, 4, 2, 3))
    pwrep = jnp.broadcast_to(pwrep[:, :, :, :, None, :], (N_SG, 2, CHUNK + 1, SG_GROUPS, c, n))
    pwrep = pwrep.reshape(N_SG, 2, CHUNK + 1, LANES, n)
    pwcol = jnp.transpose(pw.reshape(2, N_SG, SG_STATE, CHUNK + 1), (1, 0, 2, 3))
    bbt = jnp.transpose(jnp.stack([bb_re, bb_im]), (0, 1, 3, 2)).reshape(2, N_SG, LANES, n).transpose(1, 0, 2, 3)
    cs = jnp.transpose(jnp.stack([c_re.astype(F32), c_im.astype(F32)]), (0, 1, 3, 2))
    cs = cs.reshape(2, N_SG, SG_STATE, c).transpose(1, 0, 2, 3)
    dmat = (d.astype(F32).reshape(g, c, 1) * jnp.eye(c, dtype=F32)[None]).reshape(N_SG, LANES, c)
    ar = pw_re[..., CHUNK].reshape(N_SG, 1, SG_STATE)
    ai = pw_im[..., CHUNK].reshape(N_SG, 1, SG_STATE)
    return pwrep, pwcol, bbt, cs, dmat, ar, ai


def _ssm_kernel(u_ref, pwrep_ref, pwcol_ref, bbt_ref, cs_ref, dmat_ref, ar_ref, ai_ref, y_ref,
                tt_ref, wp_ref, wc_ref, p_ref, scr_ref, toep_ref, tmp_ref, *, steps, nseq):
    def iota(shape, dim):
        return lax.broadcasted_iota(jnp.int32, shape, dim)

    def spread(nrows):
        r, cidx = iota((nrows, nrows * SG_GROUPS), 0), iota((nrows, nrows * SG_GROUPS), 1)
        return jnp.where(((r >> 4) == (cidx >> 7)) & ((r & 15) == (cidx & 15)), 1.0, 0.0).astype(BF16)

    grp16 = lambda idx: (idx >> 4) & (SG_GROUPS - 1)
    grp64 = lambda idx: (idx >> 6) & (SG_GROUPS - 1)


    r, cidx = iota((SSM_STATE, SG_STATE), 0), iota((SSM_STATE, SG_STATE), 1)
    e_p = jnp.where(r == (cidx & (SSM_STATE - 1)), 1.0, 0.0).astype(BF16)
    keep = grp16(iota((LANES, SG_STATE), 0)) == grp64(iota((LANES, SG_STATE), 1))
    b_r, b_i = bbt_ref[0, 0], bbt_ref[0, 1]
    for s in range(CHUNK):
        a_r, a_i = pwrep_ref[0, 0, CHUNK - 1 - s], pwrep_ref[0, 1, CHUNK - 1 - s]
        for part, v in enumerate((a_r * b_r - a_i * b_i, a_r * b_i + a_i * b_r)):
            full = jnp.dot(v.astype(BF16), e_p, preferred_element_type=F32)
            wp_ref[s * LANES:(s + 1) * LANES, part * SG_STATE:(part + 1) * SG_STATE] = (
                jnp.where(keep, full, 0.0).astype(BF16))

    c_r, c_i = cs_ref[0, 0], cs_ref[0, 1]
    cstack = jnp.concatenate([c_r, -c_i], axis=0).astype(BF16)
    e_t = spread(SSM_GROUP)
    keep = grp16(iota((LANES, LANES), 0)) == grp16(iota((LANES, LANES), 1))
    lag_blocks = []
    for k in range(CHUNK):
        slab = wp_ref[(CHUNK - 1 - k) * LANES:(CHUNK - k) * LANES, :]
        kk = jnp.dot(slab, cstack, preferred_element_type=F32)
        if k == 0:
            kk = kk + dmat_ref[0]
        lag_blocks.append(jnp.where(keep, jnp.dot(kk.astype(BF16), e_t, preferred_element_type=F32), 0.0)
                          .astype(BF16))
    for dlt in range(N_TT):
        for s in range(2):
            for t in range(2):
                k = 2 * dlt + t - s
                blk = lag_blocks[k] if k >= 0 else jnp.zeros((LANES, LANES), BF16)
                tt_ref[dlt, s * LANES:(s + 1) * LANES, t * LANES:(t + 1) * LANES] = blk

    keep = grp64(iota((SG_STATE, LANES), 0)) == grp16(iota((SG_STATE, LANES), 1))
    for t in range(CHUNK):
        a_r = pwcol_ref[0, 0, :, t + 1:t + 2]
        a_i = pwcol_ref[0, 1, :, t + 1:t + 2]
        for part, v in enumerate((c_r * a_r - c_i * a_i, -(c_r * a_i + c_i * a_r))):
            full = jnp.dot(v.astype(BF16), e_t, preferred_element_type=F32)
            wc_ref[part * SG_STATE:(part + 1) * SG_STATE, t * LANES:(t + 1) * LANES] = (
                jnp.where(keep, full, 0.0).astype(BF16))

    p_ref[...] = jnp.dot(u_ref[0], wp_ref[...], preferred_element_type=F32)

    ar = ar_ref[0]
    ai = ai_ref[0]

    def scan_step(c, carry):
        new = []
        for b in range(nseq):
            hr, hi = carry[b]
            row = b * steps + c
            pr = p_ref[row:row + 1, 0:SG_STATE]
            pi = p_ref[row:row + 1, SG_STATE:2 * SG_STATE]
            p_ref[row:row + 1, 0:SG_STATE] = hr
            p_ref[row:row + 1, SG_STATE:2 * SG_STATE] = hi
            new.append((ar * hr - ai * hi + pr, ar * hi + ai * hr + pi))
        return new

    tn = scr_ref.shape[0] // CHUNK
    units = [(r0, jo) for r0 in range(0, steps * nseq, tn) for jo in range(N_TT)]
    steps_per_unit = steps // len(units)
    z = jnp.zeros((1, SG_STATE), F32)
    carry = [(z, z)] * nseq
    for ui, (r0, jo) in enumerate(units):
        rows = slice(r0, r0 + tn)
        acc = jnp.dot(u_ref[0, rows, 0:MXU_TILE], tt_ref[jo], preferred_element_type=F32)
        for ji in range(1, jo + 1):
            acc += jnp.dot(u_ref[0, rows, ji * MXU_TILE:(ji + 1) * MXU_TILE], tt_ref[jo - ji],
                           preferred_element_type=F32)
        toep_ref[rows, jo * MXU_TILE:(jo + 1) * MXU_TILE] = acc
        for c in range(ui * steps_per_unit, (ui + 1) * steps_per_unit):
            carry = scan_step(c, carry)

    for r0 in range(0, steps * nseq, tn):
        rows = slice(r0, r0 + tn)
        hb = p_ref[rows, :].astype(BF16)
        for jo in range(N_TT):
            cols = slice(jo * MXU_TILE, (jo + 1) * MXU_TILE)
            acc = toep_ref[rows, cols] + jnp.dot(hb, wc_ref[:, cols], preferred_element_type=F32)
            yg = jax.nn.gelu(acc, approximate=True)
            for e in range(MXU_TILE // LANES):
                q, m = divmod(jo * (MXU_TILE // LANES) + e, RELAYOUT_STRIDE)
                tmp_ref[m, pl.ds(q, tn, stride=RELAYOUT_STRIDE), :] = yg[:, e * LANES:(e + 1) * LANES]
        for m in range(RELAYOUT_STRIDE):
            scr_ref[pl.ds(m, tn * CHUNK // RELAYOUT_STRIDE, stride=RELAYOUT_STRIDE), :] = tmp_ref[m]
        y_ref[r0 * CHUNK:(r0 + tn) * CHUNK, :] = scr_ref[...].astype(BF16)


def _ssm(uflat, params, bsz):
    nc = uflat.shape[1]
    blk = lambda a: pl.BlockSpec((1,) + a.shape[1:], lambda s: (s,) + (0,) * (a.ndim - 1))
    return pl.pallas_call(
        functools.partial(_ssm_kernel, steps=nc // bsz, nseq=bsz),
        grid=(N_SG,),
        in_specs=[blk(uflat)] + [blk(a) for a in params],
        out_specs=pl.BlockSpec((nc * CHUNK, LANES), lambda s: (0, s)),
        out_shape=jax.ShapeDtypeStruct((nc * CHUNK, SSM_WIDTH), BF16),
        scratch_shapes=[pltpu.VMEM((N_TT, MXU_TILE, MXU_TILE), BF16),
                        pltpu.VMEM((FLAT, 2 * SG_STATE), BF16),
                        pltpu.VMEM((2 * SG_STATE, FLAT), BF16),
                        pltpu.VMEM((nc, 2 * SG_STATE), F32),
                        pltpu.VMEM((SSM_Y_ROWS * CHUNK, LANES), F32),
                        pltpu.VMEM((nc, FLAT), F32),
                        pltpu.VMEM((RELAYOUT_STRIDE, SSM_Y_ROWS * CHUNK // RELAYOUT_STRIDE, LANES), F32)],
        compiler_params=_cparams(1),
        name="ssm",
    )(uflat, *params)


ATT_UNITS_IN_FLIGHT = 4
ATT_STEP_QUERIES = 2048


def _attn_kernel(*refs, n_cast):
    q_ref, kp_ref, kc_ref, vp_ref, vc_ref = refs[:5]
    cast_in = refs[5:5 + n_cast]
    o_ref, lse_ref = refs[5 + n_cast:7 + n_cast]
    cast_out = refs[7 + n_cast:7 + 2 * n_cast]
    ks_ref, vs_ref = refs[7 + 2 * n_cast:]
    for src, dst in zip(cast_in, cast_out):
        dst[...] = src[...].astype(BF16)

    nsb, qb = q_ref.shape[0], q_ref.shape[1]
    blk = ATT_SPAN
    first = pl.program_id(1) == 0

    lane = lax.broadcasted_iota(jnp.int32, (1, ATT_MERGED), 1)
    head_sel = (lane % LANES < ATT_HEAD_DIM, lane % LANES >= ATT_HEAD_DIM)
    for sq in range(nsb):
        for src_p, src_c, dst in ((kp_ref, kc_ref, ks_ref), (vp_ref, vc_ref, vs_ref)):
            for e in range(2):
                xp, xc = src_p[sq], src_c[sq]
                dst[sq, e, 0:blk] = jnp.where(head_sel[e], xp, jnp.zeros_like(xp))
                dst[sq, e, blk:blk + qb] = jnp.where(head_sel[e], xc, jnp.zeros_like(xc))

    qi = lax.broadcasted_iota(jnp.int32, (blk, 2 * blk), 0)
    kk = lax.broadcasted_iota(jnp.int32, (blk, 2 * blk), 1)
    band = jnp.logical_and(kk >= qi, kk <= qi + blk)
    band_first = jnp.logical_and(band, jnp.logical_or(kk >= blk, jnp.logical_not(first)))
    low = lax.broadcasted_iota(jnp.int32, (blk, LANES), 1) < ATT_HEAD_DIM
    dn = (((1,), (1,)), ((), ()))

    units = [(sq, j, pair) for sq in range(nsb) for j in range(qb // blk) for pair in range(ATT_MERGED // LANES)]
    for g0 in range(0, len(units), ATT_UNITS_IN_FLIGHT):
        group = units[g0:g0 + ATT_UNITS_IN_FLIGHT]
        scores = []
        for sq, j, pair in group:
            cols = slice(pair * LANES, (pair + 1) * LANES)
            keys = slice(j * blk, (j + 2) * blk)
            kcat = jnp.concatenate([ks_ref[sq, 0, keys, cols], ks_ref[sq, 1, keys, cols]], axis=0)
            s = lax.dot_general(q_ref[sq, j * blk:(j + 1) * blk, cols], kcat, dn,
                                preferred_element_type=F32)
            mask = band_first if j == 0 else band
            scores.append([jnp.where(mask, s[:, e * 2 * blk:(e + 1) * 2 * blk], NEG_INF) for e in range(2)])
        maxes = [[jnp.max(jnp.maximum(se[:, :blk], se[:, blk:]), axis=-1, keepdims=True) for se in su]
                 for su in scores]
        probs = [[jnp.exp2(se - me) for se, me in zip(su, mu)] for su, mu in zip(scores, maxes)]
        dens = [[jnp.sum(pe[:, :blk] + pe[:, blk:], axis=-1, keepdims=True) for pe in pu] for pu in probs]
        for (sq, j, pair), pu, mu, du in zip(group, probs, maxes, dens):
            cols = slice(pair * LANES, (pair + 1) * LANES)
            rows = slice(j * blk, (j + 1) * blk)
            keys = slice(j * blk, (j + 2) * blk)
            vcat = jnp.concatenate([vs_ref[sq, 0, keys, cols], vs_ref[sq, 1, keys, cols]], axis=0)
            pcat = jnp.concatenate([pu[0].astype(BF16), pu[1].astype(BF16)], axis=1)
            num = jnp.dot(pcat, vcat, preferred_element_type=F32)
            o_ref[sq, rows, cols] = (num * jnp.where(low, 1.0 / du[0], 1.0 / du[1])).astype(BF16)
            lse_ref[sq, rows, cols] = jnp.where(low, mu[0] * LN2 + jnp.log(du[0]), mu[1] * LN2 + jnp.log(du[1]))


def _banded_attention(q, k, v, casts=()):
    nseq, length, w = q.shape
    blk = ATT_SPAN
    qb = min(length, ATT_STEP_QUERIES)
    nsb = ATT_STEP_QUERIES // qb
    per = qb // blk
    grid = (nseq // nsb, length // qb)
    n_steps = grid[0] * grid[1]
    cur = pl.BlockSpec((nsb, qb, w), lambda s, n: (s, n, 0))
    prev = pl.BlockSpec((nsb, blk, w), lambda s, n: (s, jnp.maximum(n * per - 1, 0), 0))
    cast_in_specs, cast_out_specs, cast_shapes = [], [], []
    for arr, window in casts:
        rows = arr.shape[0] // n_steps
        col_blk, width = (0, arr.shape[1]) if window is None else window
        cast_in_specs.append(pl.BlockSpec((rows, width), lambda s, n, cb=col_blk: (s * grid[1] + n, cb)))
        cast_out_specs.append(pl.BlockSpec((rows, width), lambda s, n: (s * grid[1] + n, 0)))
        cast_shapes.append(jax.ShapeDtypeStruct((arr.shape[0], width), BF16))
    outs = pl.pallas_call(
        functools.partial(_attn_kernel, n_cast=len(casts)),
        grid=grid,
        in_specs=[cur, prev, cur, prev, cur] + cast_in_specs,
        out_specs=[cur, cur] + cast_out_specs,
        out_shape=[jax.ShapeDtypeStruct(q.shape, BF16), jax.ShapeDtypeStruct(q.shape, F32)] + cast_shapes,
        scratch_shapes=[pltpu.VMEM((nsb, 2, blk + qb, w), BF16)] * 2,
        compiler_params=_cparams(2),
        name="attn",
    )(q, k, k, v, v, *[arr for arr, _ in casts])
    return outs[0], outs[1], outs[2:]


def _mix_kernel(y_ref, o0_ref, o1_ref, o2_ref, l0_ref, l1_ref, l2_ref, h_ref, wgs_ref, wga_ref, bgate_ref,
                wglu_ref, bglu_ref, wup_ref, wmix_ref, bmix_ref, g_ref, b_ref, out_ref, scr_ref, tmp_ref, acc_ref):
    o_refs = (o0_ref, o1_ref, o2_ref)
    l_refs = (l0_ref, l1_ref, l2_ref)

    def tile(sub, rows):
        def natural(ref, dil):
            n = SUB_ROWS // dil
            if dil == 1:
                return ref[0, 0, rows, :].astype(F32)
            for e in range(ATT_MERGED // LANES):
                piece = lambda r: ref[0, r, sub * n:(sub + 1) * n, e * LANES:(e + 1) * LANES].astype(F32)
                if dil == RELAYOUT_STRIDE:
                    for r in range(dil):
                        scr_ref[sub, e, pl.ds(r, n, stride=dil), :] = piece(r)
                else:
                    for m in range(RELAYOUT_STRIDE):
                        for q in range(dil // RELAYOUT_STRIDE):
                            tmp_ref[sub, e, m, pl.ds(q, n, stride=RELAYOUT_STRIDE), :] = (
                                piece(RELAYOUT_STRIDE * q + m))
                        scr_ref[sub, e, pl.ds(m, SUB_ROWS // RELAYOUT_STRIDE, stride=RELAYOUT_STRIDE), :] = (
                            tmp_ref[sub, e, m])
            return jnp.concatenate([scr_ref[sub, e] for e in range(ATT_MERGED // LANES)], axis=-1)

        ls = [natural(l_refs[gi], dil) for gi, dil in enumerate(DILATIONS)]
        m = jnp.maximum(jnp.maximum(ls[0], ls[1]), ls[2])
        es = [jnp.exp(l - m) for l in ls]
        att = es[0] * natural(o_refs[0], DILATIONS[0])
        for gi in (1, 2):
            att += es[gi] * natural(o_refs[gi], DILATIONS[gi])
        att = (att / (es[0] + es[1] + es[2])).astype(BF16)
        y = y_ref[rows, :]
        hb = h_ref[rows, :].astype(BF16)
        yield

        for c in range(D_MODEL // MIX_CHUNK):
            cols = slice(c * MIX_CHUNK, (c + 1) * MIX_CHUNK)
            gate_cols = slice(D_MODEL + c * MIX_CHUNK, D_MODEL + (c + 1) * MIX_CHUNK)
            val = jnp.dot(y, wglu_ref[:, cols], preferred_element_type=F32) + bglu_ref[:, cols]
            gate = jnp.dot(y, wglu_ref[:, gate_cols], preferred_element_type=F32) + bglu_ref[:, gate_cols]
            b_att = jnp.dot(att, wup_ref[:, cols], preferred_element_type=F32)
            g_ssm = jnp.dot(hb, wgs_ref[:, cols], preferred_element_type=F32) + bgate_ref[:, cols]
            g_att = jnp.dot(hb, wga_ref[:, cols], preferred_element_type=F32) + bgate_ref[:, gate_cols]
            mixed = _sigmoid(g_ssm) * (val * _sigmoid(gate)) + _sigmoid(g_att) * b_att
            acc_ref[rows, cols] = mixed.astype(BF16)
            yield

        r = jnp.dot(acc_ref[rows, :], wmix_ref[...], preferred_element_type=F32) + bmix_ref[...]
        yield
        out_ref[rows, :] = _layer_norm(DEEPNORM_ALPHA * h_ref[rows, :] + r, g_ref[...], b_ref[...])

    _stagger([tile(i, rows) for i, rows in _sub_tiles(h_ref.shape[0])], lag=STAGE_LAG)


def _mix(yg, outs, lses, h, wgs, wga, bgate, wglu, bglu, wup, wmix, bmix, g, b, seq):
    t = h.shape[0]
    tm = 2 * TOKEN_TILE
    per_b = seq // tm
    row = lambda n: pl.BlockSpec((tm, n), lambda i: (i, 0))
    full = lambda a: pl.BlockSpec(a.shape, lambda i: (0,) * a.ndim, pipeline_mode=pl.Buffered(1))
    grp = lambda d: pl.BlockSpec((1, d, tm // d, ATT_MERGED), lambda i: (i // per_b, 0, i % per_b, 0))
    consts = (wgs, wga, bgate, wglu, bglu, wup, wmix, bmix, g, b)
    return pl.pallas_call(
        _mix_kernel,
        grid=(t // tm,),
        in_specs=([row(SSM_WIDTH)] + [grp(d) for d in DILATIONS] * 2 + [row(D_MODEL)]
                  + [full(a) for a in consts]),
        out_specs=row(D_MODEL),
        out_shape=jax.ShapeDtypeStruct((t, D_MODEL), F32),
        scratch_shapes=[pltpu.VMEM((tm // SUB_ROWS, ATT_MERGED // LANES, SUB_ROWS, LANES), F32),
                        pltpu.VMEM((tm // SUB_ROWS, ATT_MERGED // LANES, RELAYOUT_STRIDE,
                                    SUB_ROWS // RELAYOUT_STRIDE, LANES), F32),
                        pltpu.VMEM((tm, D_MODEL), BF16)],
        compiler_params=_cparams(1),
        name="mix",
    )(yg, *outs, *lses, h, *consts)


def _tail_kernel(h_ref, mem_ref, wkv_ref, wq_ref, wo_ref, g2_ref, b2_ref,
                 w1_ref, bf1_ref, w2_ref, bf2_ref, g3_ref, b3_ref, out_ref, kv_ref, acc_ref, *, steps_per_batch):
    dn = (((1,), (1,)), ((), ()))

    @pl.when(pl.program_id(0) % steps_per_batch == 0)
    def _():
        kv_ref[...] = jnp.dot(mem_ref[0].astype(BF16), wkv_ref[...], preferred_element_type=F32).astype(BF16)

    def tile(sub, rows):
        h = h_ref[rows, :]
        hb = h.astype(BF16)
        yield
        q = (jnp.dot(hb, wq_ref[...], preferred_element_type=F32) * XATT_SCALE).astype(BF16)
        yield
        outs = []
        for hh in range(XATT_HEADS):
            sl = slice(hh * XATT_HEAD_DIM, (hh + 1) * XATT_HEAD_DIM)
            sv = slice(D_MODEL + hh * XATT_HEAD_DIM, D_MODEL + (hh + 1) * XATT_HEAD_DIM)
            s = lax.dot_general(q[:, sl], kv_ref[:, sl], dn, preferred_element_type=F32)
            e = jnp.exp(s - jnp.max(s, axis=-1, keepdims=True))
            p = e / jnp.sum(e, axis=-1, keepdims=True)
            outs.append(jnp.dot(p.astype(BF16), kv_ref[:, sv], preferred_element_type=F32).astype(BF16))
            yield
        xo = jnp.dot(jnp.concatenate(outs, axis=-1), wo_ref[...], preferred_element_type=F32)
        yield
        h = _layer_norm(DEEPNORM_ALPHA * h + xo, g2_ref[...], b2_ref[...])
        hb = h.astype(BF16)
        yield
        for c in range(D_FF // FFN_CHUNK):
            cols = slice(c * FFN_CHUNK, (c + 1) * FFN_CHUNK)
            a = jnp.dot(hb, w1_ref[:, cols], preferred_element_type=F32) + bf1_ref[:, cols]
            a = jnp.square(jnp.maximum(a, 0.0)).astype(BF16)
            part = jnp.dot(a, w2_ref[cols, :], preferred_element_type=F32)
            if c == 0:
                acc_ref[rows, :] = part
            else:
                acc_ref[rows, :] += part
            yield
        ff = acc_ref[rows, :] + bf2_ref[...]
        out_ref[rows, :] = _layer_norm(DEEPNORM_ALPHA * h + ff, g3_ref[...], b3_ref[...])

    _stagger([tile(i, rows) for i, rows in _sub_tiles(h_ref.shape[0])], lag=STAGE_LAG)


def _tail(h, mem, wkv, wq, wo, g2, b2, w1, bf1, w2, bf2, g3, b3, seq):
    t = h.shape[0]
    tm = 2 * TOKEN_TILE
    per_b = seq // tm
    row = pl.BlockSpec((tm, D_MODEL), lambda i: (i, 0))
    full = lambda a: pl.BlockSpec(a.shape, lambda i: (0,) * a.ndim, pipeline_mode=pl.Buffered(1))
    consts = (wkv, wq, wo, g2, b2, w1, bf1, w2, bf2, g3, b3)
    return pl.pallas_call(
        functools.partial(_tail_kernel, steps_per_batch=per_b),
        grid=(t // tm,),
        in_specs=[row, pl.BlockSpec((1,) + mem.shape[1:], lambda i: (i // per_b, 0, 0))] + [full(a) for a in consts],
        out_specs=row,
        out_shape=jax.ShapeDtypeStruct((t, D_MODEL), F32),
        scratch_shapes=[pltpu.VMEM((mem.shape[1], 2 * D_MODEL), BF16), pltpu.VMEM((tm, D_MODEL), F32)],
        compiler_params=pltpu.CompilerParams(dimension_semantics=("arbitrary",), vmem_limit_bytes=VMEM_LIMIT),
        name="tail",
    )(h, mem, *consts)


def _rope_selectors():
    half = ROT_DIM // 2
    sel = np.zeros((ROT_DIM, 3 * LANES), np.float32)
    one = np.zeros((1, LANES), np.float32)
    for lane in range(LANES):
        d = lane % ATT_HEAD_DIM
        if d >= ROT_DIM:
            one[0, lane] = 1.0
            continue
        sel[d % half, lane] = 1.0
        if d < half:
            sel[half + d, LANES + lane] = -1.0
        else:
            sel[d, 2 * LANES + lane] = 1.0
    return jnp.asarray(sel, BF16), jnp.asarray(one)


def _rope_cos_sin(positions):
    inv_freq = ROPE_THETA ** (-jnp.arange(0, ROT_DIM, 2, dtype=F32) / ROT_DIM)
    ang = positions.astype(F32).reshape(-1, 1) * inv_freq
    return jnp.concatenate([jnp.cos(ang), jnp.sin(ang)], axis=-1)


def kernel(x, mem, positions, ln_in_g, ln_in_b, w_in, b_in, ssm_log_dt, ssm_a_re, ssm_a_im, ssm_b_re, ssm_b_im, ssm_c_re, ssm_c_im, ssm_d, w_glu, b_glu, w_att_up, w_mix_out, b_mix_out, ln1_g, ln1_b, w_xq, w_xkv, w_xo, ln2_g, ln2_b, w_ff1, b_ff1, w_ff2, b_ff2, ln3_g, ln3_b):
    bsz, seq, _ = x.shape
    t = bsz * seq
    row2 = lambda a: a.reshape(1, -1).astype(F32)
    cs = _rope_cos_sin(positions)
    h = x.reshape(t, D_MODEL)
    for l in range(DEPTH):
        h, uflat, *qkv = _ln_proj(
            h, row2(ln_in_g), row2(ln_in_b), w_in[l, :, :GATE_OFF].astype(BF16), row2(b_in[l, :GATE_OFF]),
            cs, bsz, seq)

        ssm_params = _ssm_matrices(ssm_log_dt[l], ssm_a_re[l], ssm_a_im[l], ssm_b_re[l],
                                   ssm_b_im[l], ssm_c_re[l], ssm_c_im[l], ssm_d[l])
        yg = _ssm(uflat, ssm_params, bsz)

        gate_cb = GATE_OFF // D_MODEL
        casts = ([(w_ff1[l], None)],
                 [(w_ff2[l], None)],
                 [(w_in[l], (gate_cb, D_MODEL)), (w_in[l], (gate_cb + 1, D_MODEL)), (w_glu[l], None),
                  (w_att_up[l], None), (w_mix_out[l], None), (w_xkv[l], None), (w_xq[l], None), (w_xo[l], None)])
        outs, lses, cast = [], [], []
        for gi, dil in enumerate(DILATIONS):
            qg, kg, vg = (a.reshape(bsz * dil, seq // dil, ATT_MERGED) for a in qkv[3 * gi:3 * gi + 3])
            o_g, lse_g, cast_g = _banded_attention(qg, kg, vg, casts[gi])
            outs.append(o_g.reshape(bsz, dil, seq // dil, ATT_MERGED))
            lses.append(lse_g.reshape(bsz, dil, seq // dil, ATT_MERGED))
            cast.extend(cast_g)
        wb_ff1, wb_ff2, wb_gs, wb_ga, wb_glu, wb_up, wb_mix, wb_xkv, wb_xq, wb_xo = cast

        h = _mix(yg, outs, lses, h, wb_gs, wb_ga, row2(b_in[l, GATE_OFF:]), wb_glu, row2(b_glu[l]),
                 wb_up, wb_mix, row2(b_mix_out[l]), row2(ln1_g[l]), row2(ln1_b[l]), seq)

        h = _tail(h, mem, wb_xkv, wb_xq, wb_xo, row2(ln2_g[l]), row2(ln2_b[l]), wb_ff1, row2(b_ff1[l]),
                  wb_ff2, row2(b_ff2[l]), row2(ln3_g[l]), row2(ln3_b[l]), seq)
    return h.reshape(bsz, seq, D_MODEL)
```
